```python
import jax
import jax.numpy as jnp
from jax import lax
import numpy as np

D_MODEL = 1024
BATCH = 2
SEQ = 16384
DEPTH = 4

GRID_W = 64
CTX_LEN = 256
N_MIXERS = 4
N_PER_TYPE = tuple(len(range(m, DEPTH, N_MIXERS)) for m in range(N_MIXERS))
NORM_EPS = 1e-6
CHUNK = 64

HG_DK = 128
HG_HEADS = D_MODEL // HG_DK
HG_DV = D_MODEL // HG_HEADS
HG_WIDTH = HG_HEADS * HG_DK

AT_HD = 64
AT_HEADS = D_MODEL // AT_HD
AT_KV_HEADS = AT_HEADS // 4
AT_GROUP = AT_HEADS // AT_KV_HEADS
AT_Q_DIM = AT_HEADS * AT_HD
AT_KV_DIM = AT_KV_HEADS * AT_HD
AT_QKV_DIM = AT_Q_DIM + 2 * AT_KV_DIM
AT_WINDOW = 128
AT_BLOCK = 128
ROPE_BASE = 10000.0
AXIS_DIM = AT_HD // 2

RT_DK = 256
RT_HEADS = D_MODEL // RT_DK
RT_DV = 2 * RT_DK
RT_QK_DIM = RT_HEADS * RT_DK
RT_V_DIM = RT_HEADS * RT_DV
RT_IN_DIM = 2 * RT_QK_DIM + 2 * RT_V_DIM

RW_HEAD = 64
RW_HEADS = D_MODEL // RW_HEAD
RW_DECAY_LORA = max(32, int(round(1.8 * D_MODEL ** 0.5 / 32)) * 32)
RW_AAA_LORA = max(32, int(round(1.8 * D_MODEL ** 0.5 / 32)) * 32)
RW_GATE_LORA = max(32, int(round(0.6 * D_MODEL ** 0.8 / 32)) * 32)
RW_GN_EPS = 64e-5

D_FF = 2816

kernel_name = 'hybrid_interleaved_dit_trunk'

F32 = jnp.float32


def rms_norm(x, g=None, eps=NORM_EPS):
    xf = x.astype(F32)
    y = xf * lax.rsqrt(jnp.mean(xf * xf, axis=-1, keepdims=True) + eps)
    if g is not None:
        y = y * g.astype(F32)
    return y.astype(x.dtype)


def head_layer_norm(y, w, b, n_heads, eps):
    bsz, s, d = y.shape
    yh = y.reshape(bsz, s, n_heads, d // n_heads).astype(F32)
    mu = jnp.mean(yh, axis=-1, keepdims=True)
    var = jnp.mean(jnp.square(yh - mu), axis=-1, keepdims=True)
    yh = (yh - mu) * lax.rsqrt(var + eps)
    return yh.reshape(bsz, s, d) * w.astype(F32) + b.astype(F32)


def adaln(cvec, w, b):
    m = jax.nn.silu(cvec) @ w + b
    return jnp.split(m[:, None, :], 6, axis=-1)


def modulate(h, shift, scale):
    return h * (1.0 + scale) + shift


def rotate_half(x, n_seg):
    sh = x.shape
    xs = x.reshape(sh[:-1] + (n_seg, 2, sh[-1] // (2 * n_seg)))
    return jnp.concatenate([-xs[..., 1:, :], xs[..., :1, :]], axis=-2).reshape(sh)


def apply_rope(x, cos, sin, n_seg):
    xf = x.astype(F32)
    return (xf * cos + rotate_half(xf, n_seg) * sin).astype(x.dtype)


def axial_rope_tables(n_tokens):
    t = jnp.arange(n_tokens)
    row = (t // GRID_W).astype(F32)
    col = (t % GRID_W).astype(F32)
    inv = ROPE_BASE ** (-jnp.arange(0, AXIS_DIM, 2, dtype=F32) / AXIS_DIM)
    ang_r = row[:, None] * inv[None, :]
    ang_c = col[:, None] * inv[None, :]
    ang = jnp.concatenate([ang_r, ang_r, ang_c, ang_c], axis=-1)
    return jnp.cos(ang), jnp.sin(ang)


def retention_rope_tables(n_tokens):
    t = jnp.arange(n_tokens, dtype=F32)
    inv = ROPE_BASE ** (-jnp.linspace(0.0, 1.0, RT_DK // 2, dtype=F32))
    ang = t[:, None] * inv[None, :]
    ang = jnp.concatenate([ang, ang], axis=-1)
    return jnp.cos(ang), jnp.sin(ang)


def to_chunks(x, n_heads, chunk):
    *lead, b, l, hd = x.shape
    nl = len(lead)
    x = x.reshape(*lead, b, l // chunk, chunk, n_heads, hd // n_heads)
    return x.transpose(*range(nl), nl + 1, nl, nl + 3, nl + 2, nl + 4)


def from_chunks(o):
    n, b, h, c, d = o.shape
    return o.transpose(1, 0, 3, 2, 4).reshape(b, n * c, h * d)


def bidir_chunk_scan(chunk_step, s0, shared, per_dir):
    n = shared[0].shape[0]

    def body(state, j):
        jb = n - 1 - j
        sh = [jnp.stack([a[j], jnp.flip(a[jb], axis=-2)]) for a in shared]
        pd = [jnp.stack([a[0, j], jnp.flip(a[1, jb], axis=-2)]) for a in per_dir]
        return chunk_step(state, *sh, *pd)

    s_final, o = lax.scan(body, s0, jnp.arange(n))
    return o[:, 0] + jnp.flip(o[:, 1], axis=(0, -2)), s_final


def gla_chunk_step(state, q, v, k, g):
    c = q.shape[-2]
    lower = jnp.tril(jnp.ones((c, c), bool))
    b = jnp.cumsum(g, axis=-2)
    o_inter = jnp.einsum('...td,...de->...te', q * jnp.exp(b), state)
    diff = b[..., :, None, :] - b[..., None, :, :]
    decay = jnp.exp(jnp.where(lower[:, :, None], diff, -jnp.inf))
    scores = jnp.einsum('...td,...sd,...tsd->...ts', q, k, decay)
    o_intra = jnp.einsum('...ts,...se->...te', scores, v)
    b_end = b[..., -1:, :]
    state = jnp.exp(b_end[..., 0, :])[..., :, None] * state + jnp.einsum('...sd,...se->...de', k * jnp.exp(b_end - b), v)
    return state, o_inter + o_intra


def make_retention_step(log_gamma):
    lg = log_gamma[..., None]
    idx = jnp.arange(CHUNK, dtype=F32)
    rel = idx[:, None] - idx[None, :]
    intra = jnp.where(rel >= 0, jnp.exp(lg[..., None] * jnp.maximum(rel, 0.0)), 0.0)
    q_in = jnp.exp(lg * (idx + 1.0))[..., None]
    k_out = jnp.exp(lg * (CHUNK - 1.0 - idx))[..., None]
    carry = jnp.exp(lg * CHUNK)[..., None]

    def step(state, q, k, v):
        o = jnp.einsum('...td,...de->...te', q * q_in, state) + jnp.einsum('...ts,...se->...te', jnp.einsum('...td,...sd->...ts', q, k) * intra, v)
        state = carry * state + jnp.einsum('...sd,...se->...de', k * k_out, v)
        return state, o

    return step


def rwkv7_step(state, r, v, kk, w, k, b):
    r, v, kk, w, k, b = (t[..., 0, :] for t in (r, v, kk, w, k, b))
    removed = jnp.einsum('...vk,...k->...v', state, kk)
    state = state * w[..., None, :] - removed[..., :, None] * b[..., None, :] + v[..., :, None] * k[..., None, :]
    y = jnp.einsum('...vk,...k->...v', state, r)
    return state, y[..., None, :]


def qshift_grid(x):
    b, l, d = x.shape
    g = x.reshape(b, l // GRID_W, GRID_W, 4, d // 4)
    p = jnp.pad(g, ((0, 0), (1, 1), (1, 1), (0, 0), (0, 0)))
    return jnp.stack([p[:, 1:-1, :-2, 0], p[:, 1:-1, 2:, 1], p[:, :-2, 1:-1, 2], p[:, 2:, 1:-1, 3]], axis=3).reshape(b, l, d)


def bishift_seq(x):
    h = x.shape[-1] // 2
    prev = jnp.pad(x[:, :-1, :h], ((0, 0), (1, 0), (0, 0)))
    nxt = jnp.pad(x[:, 1:, h:], ((0, 0), (0, 1), (0, 0)))
    return jnp.concatenate([prev, nxt], axis=-1)


def hgrn2_mixer(xc, xl, w_in, lb, norm_g, w_out, need_ctx):
    bsz = xl.shape[0]
    lb = lb.astype(F32)

    def project(x):
        s = x.shape[1]
        q, i_in, gate, zf = jnp.split(x @ w_in, [HG_WIDTH, 2 * HG_WIDTH, 3 * HG_WIDTH], axis=-1)
        zf = zf.reshape(bsz, s, 2, HG_WIDTH).transpose(2, 0, 1, 3).astype(F32)
        log_f = jnp.log(lb + (1.0 - lb) * jax.nn.sigmoid(zf))
        k = (1.0 - lb) * jax.nn.sigmoid(-zf)
        shared = (to_chunks(q.astype(F32), HG_HEADS, CHUNK), to_chunks(i_in.astype(F32), HG_HEADS, CHUNK))
        per_dir = (to_chunks(k, HG_HEADS, CHUNK), to_chunks(log_f, HG_HEADS, CHUNK))
        return shared, per_dir, gate

    def readout(o, gate):
        b_, s, _ = o.shape
        o = rms_norm(o.reshape(b_, s, HG_HEADS, HG_DV), norm_g).reshape(b_, s, HG_WIDTH)
        return (o.astype(xl.dtype) * jax.nn.silu(gate)) @ w_out

    s0 = jnp.zeros((2, bsz, HG_HEADS, HG_DK, HG_DV), F32)
    sh, pd, gate_c = project(xc)
    o_c, s_ctx = bidir_chunk_scan(gla_chunk_step, s0, sh, pd)
    sh, pd, gate_l = project(xl)
    o_l, _ = bidir_chunk_scan(gla_chunk_step, s_ctx, sh, pd)
    out_c = readout(from_chunks(o_c), gate_c) if need_ctx else None
    return out_c, readout(from_chunks(o_l), gate_l)


def sink_attend(q, keys, vals, sink, mask):
    logits = jnp.einsum('bqgrd,bkgd->bgrqk', q, keys).astype(F32) * (AT_HD ** -0.5)
    if mask is not None:
        logits = jnp.where(mask, logits, -jnp.inf)
    s = sink.astype(F32)[None, :, :, None]
    m = jnp.maximum(jnp.max(logits, axis=-1), s)
    p = jnp.exp(logits - m[..., None])
    denom = jnp.sum(p, axis=-1) + jnp.exp(s - m)
    o = jnp.einsum('bgrqk,bkgd->bqgrd', p / denom[..., None], vals.astype(F32))
    return o.astype(q.dtype)


def window_gqa_mixer(xc, xl, w_qkv, b_qkv, sink, w_out, cos, sin, need_ctx):
    bsz, seq, _ = xl.shape
    n_ctx = xc.shape[1]
    sink = sink.reshape(AT_KV_HEADS, AT_GROUP)
    kvc = xc @ w_qkv[:, AT_Q_DIM:] + b_qkv[AT_Q_DIM:]
    kc, vc = [a.reshape(bsz, n_ctx, AT_KV_HEADS, AT_HD) for a in jnp.split(kvc, 2, axis=-1)]
    out_c = None
    if need_ctx:
        qc = (xc @ w_qkv[:, :AT_Q_DIM] + b_qkv[:AT_Q_DIM]).reshape(bsz, n_ctx, AT_KV_HEADS, AT_GROUP, AT_HD)
        out_c = sink_attend(qc, kc, vc, sink, None).reshape(bsz, n_ctx, AT_Q_DIM) @ w_out
    ql, kl, vl = jnp.split(xl @ w_qkv + b_qkv, [AT_Q_DIM, AT_Q_DIM + AT_KV_DIM], axis=-1)
    ql = apply_rope(ql.reshape(bsz, seq, AT_KV_HEADS, AT_GROUP, AT_HD), cos[:, None, None], sin[:, None, None], 2)
    kl = apply_rope(kl.reshape(bsz, seq, AT_KV_HEADS, AT_HD), cos[:, None], sin[:, None], 2)
    vl = vl.reshape(bsz, seq, AT_KV_HEADS, AT_HD)
    pad = ((0, 0), (AT_WINDOW, AT_WINDOW), (0, 0), (0, 0))
    kp, vp = jnp.pad(kl, pad), jnp.pad(vl, pad)
    span = AT_BLOCK + 2 * AT_WINDOW
    n_blk = seq // AT_BLOCK
    ctx_mask = jnp.ones((AT_BLOCK, n_ctx), bool)

    def block(args):
        qb, bi = args
        start = bi * AT_BLOCK
        kw = lax.dynamic_slice_in_dim(kp, start, span, axis=1)
        vw = lax.dynamic_slice_in_dim(vp, start, span, axis=1)
        qpos = start + jnp.arange(AT_BLOCK)
        kpos = start - AT_WINDOW + jnp.arange(span)
        win = (jnp.abs(qpos[:, None] - kpos[None, :]) <= AT_WINDOW) & (kpos >= 0)[None, :] & (kpos < seq)[None, :]
        mask = jnp.concatenate([ctx_mask, win], axis=1)
        return sink_attend(qb, jnp.concatenate([kc, kw], axis=1), jnp.concatenate([vc, vw], axis=1), sink, mask)

    qblocks = ql.reshape(bsz, n_blk, AT_BLOCK, AT_KV_HEADS, AT_GROUP, AT_HD).swapaxes(0, 1)
    ob = lax.map(block, (qblocks, jnp.arange(n_blk)))
    out_l = ob.swapaxes(0, 1).reshape(bsz, seq, AT_Q_DIM) @ w_out
    return out_c, out_l


def retention_mixer(xc, xl, w_in, decay_exp, w_out, cos, sin, need_ctx):
    bsz = xl.shape[0]
    log_gamma = jnp.log1p(-jnp.exp2(decay_exp.astype(F32))).reshape(2, 1, RT_HEADS)
    step = make_retention_step(log_gamma)

    def project(x, rotate):
        s = x.shape[1]
        q, k, v, gate = jnp.split(x @ w_in, [RT_QK_DIM, 2 * RT_QK_DIM, 2 * RT_QK_DIM + RT_V_DIM], axis=-1)
        q = q.reshape(bsz, s, RT_HEADS, RT_DK)
        k = k.reshape(bsz, s, RT_HEADS, RT_DK) * (RT_DK ** -0.5)
        if rotate:
            q = apply_rope(q, cos[:, None], sin[:, None], 1)
            k = apply_rope(k, cos[:, None], sin[:, None], 1)
        shared = (to_chunks(q.reshape(bsz, s, RT_QK_DIM).astype(F32), RT_HEADS, CHUNK),
                  to_chunks(k.reshape(bsz, s, RT_QK_DIM).astype(F32), RT_HEADS, CHUNK),
                  to_chunks(v.astype(F32), RT_HEADS, CHUNK))
        return shared, gate

    def readout(o, gate):
        b_, s, _ = o.shape
        o = rms_norm(o.reshape(b_, s, RT_HEADS, RT_DV)).reshape(b_, s, RT_V_DIM)
        return (jax.nn.silu(gate) * o.astype(xl.dtype)) @ w_out

    s0 = jnp.zeros((2, bsz, RT_HEADS, RT_DK, RT_DV), F32)
    sh, gate_c = project(xc, False)
    o_c, s_ctx = bidir_chunk_scan(step, s0, sh, ())
    sh, gate_l = project(xl, True)
    o_l, _ = bidir_chunk_scan(step, s_ctx, sh, ())
    out_c = readout(from_chunks(o_c), gate_c) if need_ctx else None
    return out_c, readout(from_chunks(o_l), gate_l)


def rwkv7_mixer(xc, xl, mix, w_rkv, w0, w1, w2, a0, a1, a2, g1, g2, k_k, k_a, r_k, ln_w, ln_b, w_out, need_ctx):
    bsz = xl.shape[0]

    def lora(h, wa, wb, act):
        return jnp.einsum('zbsr,zrd->zbsd', act(jnp.einsum('bsd,zdr->zbsr', h, wa)), wb)

    def project(x, shifted):
        s = x.shape[1]
        xx = shifted - x
        r = ((x + xx * mix[0]) @ w_rkv[0]).astype(F32)
        k = ((x + xx * mix[2]) @ w_rkv[1]).astype(F32)
        v = ((x + xx * mix[3]) @ w_rkv[2]).astype(F32)
        w_raw = (w0[:, None, None, :] + lora(x + xx * mix[1], w1, w2, jnp.tanh)).astype(F32)
        decay = jnp.exp(-jnp.exp(-jax.nn.softplus(-w_raw) - 0.5))
        a = jax.nn.sigmoid((a0[:, None, None, :] + lora(x + xx * mix[4], a1, a2, lambda t: t)).astype(F32))
        kkh = (k * k_k.astype(F32)).reshape(bsz, s, RW_HEADS, RW_HEAD)
        kk = (kkh / jnp.maximum(jnp.linalg.norm(kkh, axis=-1, keepdims=True), 1e-12)).reshape(bsz, s, D_MODEL)
        k_dir = k[None] * (1.0 + (a - 1.0) * k_a.astype(F32))
        shared = (to_chunks(r, RW_HEADS, 1), to_chunks(v, RW_HEADS, 1), to_chunks(kk, RW_HEADS, 1))
        per_dir = (to_chunks(decay, RW_HEADS, 1), to_chunks(k_dir, RW_HEADS, 1), to_chunks(kk[None] * a, RW_HEADS, 1))
        return shared, per_dir, (r, k_dir, v, x, xx)

    def readout(y, aux):
        r, k_dir, v, x, xx = aux
        s = y.shape[1]
        rkr = (r * (k_dir[0] + k_dir[1]) * r_k.reshape(-1).astype(F32)).reshape(bsz, s, RW_HEADS, RW_HEAD)
        bonus = (jnp.sum(rkr, axis=-1, keepdims=True) * v.reshape(bsz, s, RW_HEADS, RW_HEAD)).reshape(bsz, s, D_MODEL)
        gate = jax.nn.sigmoid((x + xx * mix[5]) @ g1) @ g2
        y = head_layer_norm(y, ln_w, ln_b, RW_HEADS, RW_GN_EPS) + bonus
        return (y.astype(xl.dtype) * gate) @ w_out

    s0 = jnp.zeros((2, bsz, RW_HEADS, RW_HEAD, RW_HEAD), F32)
    sh, pd, aux_c = project(xc, bishift_seq(xc))
    y_c, s_ctx = bidir_chunk_scan(rwkv7_step, s0, sh, pd)
    sh, pd, aux_l = project(xl, qshift_grid(xl))
    y_l, _ = bidir_chunk_scan(rwkv7_step, s_ctx, sh, pd)
    out_c = readout(from_chunks(y_c), aux_c) if need_ctx else None
    return out_c, readout(from_chunks(y_l), aux_l)


def conv_ffn(h, w_up, conv_w, conv_b, w_down, on_grid):
    bsz, s, _ = h.shape
    u, v = jnp.split(h @ w_up, 2, axis=-1)
    if on_grid:
        u = u.reshape(bsz, s // GRID_W, GRID_W, D_FF)
        rhs = conv_w[:, :, None, :]
    else:
        u = u.reshape(bsz, 1, s, D_FF)
        rhs = conv_w[1:2, :, None, :]
    u = lax.conv_general_dilated(u, rhs.astype(u.dtype), (1, 1), 'SAME', dimension_numbers=('NHWC', 'HWIO', 'NHWC'), feature_group_count=D_FF)
    u = u.reshape(bsz, s, D_FF) + conv_b
    return (jax.nn.gelu(u, approximate=False) * v) @ w_down


def setup_inputs(seed: int = 0) -> dict:
    key = jax.random.key(seed)
    ks = jax.random.split(key, 64)
    counter = iter(range(64))

    def nrm(shape, std):
        return std * jax.random.normal(ks[next(counter)], shape, jnp.float32)

    def unif(shape):
        return jax.random.uniform(ks[next(counter)], shape, jnp.float32)

    d = D_MODEL
    n_a, n_b, n_c, n_d = N_PER_TYPE
    ratio = jnp.arange(d, dtype=jnp.float32) / (d - 1)
    return {
        'x': nrm((BATCH, SEQ, d), 1.0),
        'c': nrm((BATCH, d), 1.0),
        'ctx': nrm((BATCH, CTX_LEN, d), 1.0),
        'c_ctx': nrm((d,), 1.0),
        'ada_w': nrm((DEPTH, d, 6 * d), 0.5 * d ** -0.5),
        'ada_b': nrm((DEPTH, 6 * d), 0.02),
        'norm1_g': 1.0 + nrm((DEPTH, d), 0.1),
        'norm2_g': 1.0 + nrm((DEPTH, d), 0.1),
        'ffn_w_up': nrm((DEPTH, d, 2 * D_FF), d ** -0.5),
        'ffn_conv_w': nrm((DEPTH, 3, 3, D_FF), 1.0 / 3.0),
        'ffn_conv_b': nrm((DEPTH, D_FF), 0.02),
        'ffn_w_down': nrm((DEPTH, D_FF, d), D_FF ** -0.5),
        'hg_w_in': nrm((n_a, d, 5 * HG_WIDTH), d ** -0.5),
        'hg_lb_logits': nrm((DEPTH + 1, HG_WIDTH), 0.5),
        'hg_norm_g': 1.0 + nrm((n_a, HG_DV), 0.1),
        'hg_w_out': nrm((n_a, HG_WIDTH, d), HG_WIDTH ** -0.5),
        'at_w_qkv': nrm((n_b, d, AT_QKV_DIM), d ** -0.5),
        'at_b_qkv': nrm((n_b, AT_QKV_DIM), 0.02),
        'at_sink': nrm((n_b, AT_HEADS), 0.5),
        'at_w_out': nrm((n_b, AT_Q_DIM, d), AT_Q_DIM ** -0.5),
        'rt_w_in': nrm((n_c, d, RT_IN_DIM), d ** -0.5),
        'rt_decay_exp': (-5.0 - jnp.arange(RT_HEADS, dtype=jnp.float32)) + nrm((n_c, 2, RT_HEADS), 0.1),
        'rt_w_out': nrm((n_c, RT_V_DIM, d), RT_V_DIM ** -0.5),
        'rw_mix': unif((n_d, 6, d)),
        'rw_w_rkv': nrm((n_d, 3, d, d), d ** -0.5),
        'rw_w0': (-6.5 + 5.0 * ratio ** 0.85) + nrm((n_d, 2, d), 0.1),
        'rw_w1': nrm((n_d, 2, d, RW_DECAY_LORA), d ** -0.5),
        'rw_w2': nrm((n_d, 2, RW_DECAY_LORA, d), 0.1 * RW_DECAY_LORA ** -0.5),
        'rw_a0': nrm((n_d, 2, d), 0.1),
        'rw_a1': nrm((n_d, 2, d, RW_AAA_LORA), d ** -0.5),
        'rw_a2': nrm((n_d, 2, RW_AAA_LORA, d), 0.5 * RW_AAA_LORA ** -0.5),
        'rw_g1': nrm((n_d, d, RW_GATE_LORA), d ** -0.5),
        'rw_g2': nrm((n_d, RW_GATE_LORA, d), RW_GATE_LORA ** -0.5),
        'rw_k_k': 0.85 + nrm((n_d, d), 0.05),
        'rw_k_a': 1.0 + nrm((n_d, d), 0.05),
        'rw_r_k': -0.04 + nrm((n_d, RW_HEADS, RW_HEAD), 0.02),
        'rw_ln_w': 1.0 + nrm((n_d, d), 0.1),
        'rw_ln_b': nrm((n_d, d), 0.02),
        'rw_w_out': nrm((n_d, d, d), d ** -0.5),
        'final_norm_g': 1.0 + nrm((d,), 0.1),
    }


def reference(x, c, ctx, c_ctx, ada_w, ada_b, norm1_g, norm2_g, ffn_w_up, ffn_conv_w, ffn_conv_b, ffn_w_down,
              hg_w_in, hg_lb_logits, hg_norm_g, hg_w_out, at_w_qkv, at_b_qkv, at_sink, at_w_out,
              rt_w_in, rt_decay_exp, rt_w_out, rw_mix, rw_w_rkv, rw_w0, rw_w1, rw_w2, rw_a0, rw_a1, rw_a2,
              rw_g1, rw_g2, rw_k_k, rw_k_a, rw_r_k, rw_ln_w, rw_ln_b, rw_w_out, final_norm_g):
    seq = x.shape[1]
    cos2, sin2 = axial_rope_tables(seq)
    cos1, sin1 = retention_rope_tables(seq)
    lb_all = jnp.cumsum(jax.nn.softmax(hg_lb_logits.astype(F32), axis=0), axis=0)
    c_ctx_row = c_ctx[None, :]
    xl, xc = x, ctx
    for i in range(DEPTH):
        kind, j = i % N_MIXERS, i // N_MIXERS
        need_ctx = i < DEPTH - 1
        sh1, sc1, g1, sh2, sc2, g2 = adaln(c, ada_w[i], ada_b[i])
        csh1, csc1, cg1, csh2, csc2, cg2 = adaln(c_ctx_row, ada_w[i], ada_b[i])
        hl = modulate(rms_norm(xl, norm1_g[i]), sh1, sc1)
        hc = modulate(rms_norm(xc, norm1_g[i]), csh1, csc1)
        if kind == 0:
            oc, ol = hgrn2_mixer(hc, hl, hg_w_in[j], lb_all[i], hg_norm_g[j], hg_w_out[j], need_ctx)
        elif kind == 1:
            oc, ol = window_gqa_mixer(hc, hl, at_w_qkv[j], at_b_qkv[j], at_sink[j], at_w_out[j], cos2, sin2, need_ctx)
        elif kind == 2:
            oc, ol = retention_mixer(hc, hl, rt_w_in[j], rt_decay_exp[j], rt_w_out[j], cos1, sin1, need_ctx)
        else:
            oc, ol = rwkv7_mixer(hc, hl, rw_mix[j], rw_w_rkv[j], rw_w0[j], rw_w1[j], rw_w2[j], rw_a0[j], rw_a1[j],
                                 rw_a2[j], rw_g1[j], rw_g2[j], rw_k_k[j], rw_k_a[j], rw_r_k[j], rw_ln_w[j],
                                 rw_ln_b[j], rw_w_out[j], need_ctx)
        xl = xl + g1 * ol
        xl = xl + g2 * conv_ffn(modulate(rms_norm(xl, norm2_g[i]), sh2, sc2), ffn_w_up[i], ffn_conv_w[i], ffn_conv_b[i], ffn_w_down[i], True)
        if need_ctx:
            xc = xc + cg1 * oc
            xc = xc + cg2 * conv_ffn(modulate(rms_norm(xc, norm2_g[i]), csh2, csc2), ffn_w_up[i], ffn_conv_w[i], ffn_conv_b[i], ffn_w_down[i], False)
    return rms_norm(xl, final_norm_g)
```

```python
import functools
import math

import numpy as np
import jax
import jax.numpy as jnp
from jax import lax
from jax.experimental import pallas as pl
from jax.experimental.pallas import tpu as pltpu

F32 = jnp.float32
BF16 = jnp.bfloat16

D_MODEL = 1024
DEPTH = 4
GRID_W = 64
NORM_EPS = 1e-6

HG_DK = 128
HG_HEADS = D_MODEL // HG_DK
HG_CHUNK = 128

AT_HD = 64
AT_HEADS = D_MODEL // AT_HD
AT_KV_HEADS = AT_HEADS // 4
AT_GROUP = 4
AT_KV_DIM = AT_KV_HEADS * AT_HD
AT_BLOCK = 128
ROPE_BASE = 10000.0
AXIS_DIM = AT_HD // 2

RT_DK = 256
RT_HEADS = D_MODEL // RT_DK
RT_DV = 2 * RT_DK
RT_V_DIM = RT_HEADS * RT_DV
RT_CHUNK = 256

RW_HEAD = 64
RW_HEADS = D_MODEL // RW_HEAD
RW_GN_EPS = 64e-5
RW_CHUNK = 64
RW_GROUP_HEADS = 4
RW_GROUP = RW_GROUP_HEADS * RW_HEAD
RW_NGROUPS = D_MODEL // RW_GROUP

D_FF = 2816
FF_CHUNK = 256

VMEM_LIMIT = 56 * 1024 * 1024


def _cp(*sem):
    return pltpu.CompilerParams(dimension_semantics=sem, vmem_limit_bytes=VMEM_LIMIT)


def _bdot(a, b):
    return jnp.dot(a.astype(BF16), b.astype(BF16), preferred_element_type=F32)


def _bdot_nt(a, b):
    return lax.dot_general(a.astype(BF16), b.astype(BF16), (((1,), (1,)), ((), ())), preferred_element_type=F32)


def _bdot_tn(a, b):
    return lax.dot_general(a.astype(BF16), b.astype(BF16), (((0,), (0,)), ((), ())), preferred_element_type=F32)


def _split3(x):
    hi = x.astype(BF16)
    r1 = x - hi.astype(F32)
    mid = r1.astype(BF16)
    lo = (r1 - mid.astype(F32)).astype(BF16)
    return hi, mid, lo


def _exact_dot(m_bf16, x):
    hi, mid, lo = _split3(x)
    d = lambda v: jnp.dot(m_bf16, v, preferred_element_type=F32)
    return d(hi) + d(mid) + d(lo)


def _exact_dot_r(x, m_bf16):
    hi, mid, lo = _split3(x)
    d = lambda v: jnp.dot(v, m_bf16, preferred_element_type=F32)
    return d(hi) + d(mid) + d(lo)


def _sigmoid(x):
    return 1.0 / (1.0 + jnp.exp(-x))


def _silu(x):
    return x * _sigmoid(x)


def _rms(x):
    return x * lax.rsqrt(jnp.mean(x * x, axis=-1, keepdims=True) + NORM_EPS)


def _adaln_kernel(c_ref, w_ref, b_ref, o_ref):
    o_ref[...] = _bdot(_silu(c_ref[...]), w_ref[...]) + b_ref[...]


def _adaln(cvecs, ada_w, ada_b):
    depth, d, n = ada_w.shape
    tn = 1536
    return pl.pallas_call(
        _adaln_kernel,
        out_shape=jax.ShapeDtypeStruct((depth, 8, n), F32),
        grid=(depth, n // tn),
        in_specs=[pl.BlockSpec((8, d), lambda l, j: (0, 0)),
                  pl.BlockSpec((None, d, tn), lambda l, j: (l, 0, j)),
                  pl.BlockSpec((None, 1, tn), lambda l, j: (l, 0, j))],
        out_specs=pl.BlockSpec((None, 8, tn), lambda l, j: (l, 0, j)),
        compiler_params=_cp("parallel", "parallel"),
        name="adaln",
    )(cvecs, ada_w.astype(BF16), ada_b.reshape(depth, 1, n))


def _proj_kernel(x_ref, mul_ref, add_ref, w_ref, b_ref, o_ref, h_scr):
    @pl.when(pl.program_id(1) == 0)
    def _():
        h_scr[...] = (_rms(x_ref[...]) * mul_ref[...] + add_ref[...]).astype(BF16)

    o_ref[...] = jnp.dot(h_scr[...], w_ref[...], preferred_element_type=F32) + b_ref[...]


def _proj(x2d, mul, add, w, bias, *, seq, tm, name):
    m, d = x2d.shape
    n = w.shape[1]
    tn = 512 if n % 512 == 0 else 256
    tpb = seq // tm
    return pl.pallas_call(
        _proj_kernel,
        out_shape=jax.ShapeDtypeStruct((m, n), F32),
        grid=(m // tm, n // tn),
        in_specs=[pl.BlockSpec((tm, d), lambda i, j: (i, 0)),
                  pl.BlockSpec((None, 1, d), lambda i, j: (i // tpb, 0, 0)),
                  pl.BlockSpec((None, 1, d), lambda i, j: (i // tpb, 0, 0)),
                  pl.BlockSpec((d, tn), lambda i, j: (0, j)),
                  pl.BlockSpec((1, tn), lambda i, j: (0, j))],
        out_specs=pl.BlockSpec((tm, tn), lambda i, j: (i, j)),
        scratch_shapes=[pltpu.VMEM((tm, d), BF16)],
        compiler_params=_cp("parallel", "arbitrary"),
        name=name,
    )(x2d, mul, add, w, bias)


def _mm_res_kernel(a_ref, w_ref, x_ref, g_ref, o_ref):
    o_ref[...] = x_ref[...] + g_ref[...] * _bdot(a_ref[...], w_ref[...])


def _mm_res(act, w, x2d, gate, *, seq, tm, name):
    m, k = act.shape
    d = w.shape[1]
    tpb = seq // tm
    return pl.pallas_call(
        _mm_res_kernel,
        out_shape=jax.ShapeDtypeStruct((m, d), F32),
        grid=(m // tm,),
        in_specs=[pl.BlockSpec((tm, k), lambda i: (i, 0)),
                  pl.BlockSpec((k, d), lambda i: (0, 0)),
                  pl.BlockSpec((tm, d), lambda i: (i, 0)),
                  pl.BlockSpec((None, 1, d), lambda i: (i // tpb, 0, 0))],
        out_specs=pl.BlockSpec((tm, d), lambda i: (i, 0)),
        compiler_params=_cp("parallel"),
        name=name,
    )(act, w, x2d, gate)


def _ffn_down_kernel(u_ref, up_ref, un_ref, v_ref, cw_ref, cb_ref, wd_ref, x_ref, g_ref, fg_ref, o_ref, act_scr,
                     *, tm, width, tpi, final_norm):
    i = pl.program_id(0)
    first = (i % tpi) == 0
    last = (i % tpi) == tpi - 1
    col = lax.broadcasted_iota(jnp.int32, (tm, 1), 0) & (width - 1)
    not_left = col != 0
    not_right = col != width - 1
    inv_sqrt2 = 1.0 / math.sqrt(2.0)
    for c0 in range(0, D_FF, FF_CHUNK):
        sl = slice(c0, c0 + FF_CHUNK)
        main = u_ref[:, sl]
        prev = jnp.where(first, 0.0, up_ref[:, sl])
        nxt = jnp.where(last, 0.0, un_ref[:, sl])
        if tm > width:
            above = jnp.concatenate([prev, main[: tm - width]], axis=0)
            below = jnp.concatenate([main[width:], nxt], axis=0)
        else:
            above, below = prev, nxt
        acc = jnp.zeros((tm, FF_CHUNK), F32) + cb_ref[:, sl]
        for a, src in enumerate((above, main, below)):
            acc = acc + cw_ref[3 * a + 1:3 * a + 2, sl] * src
            acc = acc + cw_ref[3 * a:3 * a + 1, sl] * jnp.where(not_left, pltpu.roll(src, 1, 0), 0.0)
            acc = acc + cw_ref[3 * a + 2:3 * a + 3, sl] * jnp.where(not_right, pltpu.roll(src, tm - 1, 0), 0.0)
        gelu = 0.5 * acc * (1.0 + lax.erf(acc * inv_sqrt2))
        act_scr[:, sl] = (gelu * v_ref[:, sl]).astype(BF16)
    y = x_ref[...] + g_ref[...] * jnp.dot(act_scr[...], wd_ref[...], preferred_element_type=F32)
    if final_norm:
        y = _rms(y) * fg_ref[...]
    o_ref[...] = y


def _ffn_down(p_up, conv_w, conv_b, w_down, x2d, gate, final_g, *, seq, width, tm, final_norm, name):
    m = x2d.shape[0]
    tpi = seq // tm
    rpt = tm // width
    nrow = m // width
    kern = functools.partial(_ffn_down_kernel, tm=tm, width=width, tpi=tpi, final_norm=final_norm)
    return pl.pallas_call(
        kern,
        out_shape=jax.ShapeDtypeStruct((m, D_MODEL), F32),
        grid=(m // tm,),
        in_specs=[pl.BlockSpec((tm, D_FF), lambda i: (i, 0)),
                  pl.BlockSpec((width, D_FF), lambda i: (jnp.maximum(i * rpt - 1, 0), 0)),
                  pl.BlockSpec((width, D_FF), lambda i: (jnp.minimum((i + 1) * rpt, nrow - 1), 0)),
                  pl.BlockSpec((tm, D_FF), lambda i: (i, 1)),
                  pl.BlockSpec((9, D_FF), lambda i: (0, 0)),
                  pl.BlockSpec((1, D_FF), lambda i: (0, 0)),
                  pl.BlockSpec((D_FF, D_MODEL), lambda i: (0, 0)),
                  pl.BlockSpec((tm, D_MODEL), lambda i: (i, 0)),
                  pl.BlockSpec((None, 1, D_MODEL), lambda i: (i // tpi, 0, 0)),
                  pl.BlockSpec((1, D_MODEL), lambda i: (0, 0))],
        out_specs=pl.BlockSpec((tm, D_MODEL), lambda i: (i, 0)),
        scratch_shapes=[pltpu.VMEM((tm, D_FF), BF16)],
        compiler_params=_cp("parallel"),
        name=name,
    )(p_up, p_up, p_up, p_up, conv_w, conv_b, w_down, x2d, gate, final_g)


def _hgrn_tables(c):
    levels = []
    m = c // 2
    while m >= 1:
        levels.append(m)
        m //= 2
    nl = len(levels)
    e = np.zeros((nl + 2, c, c), np.float32)
    msk = np.zeros((nl + 1, c, c), np.float32)
    t = np.arange(c)
    for l, m in enumerate(levels):
        blk = t // (2 * m)
        second = (t % (2 * m)) >= m
        ref = 2 * m * blk + m - 1
        for tt in range(c):
            if second[tt]:
                e[l, tt, ref[tt] + 1:tt + 1] = 1.0
            else:
                e[l, tt, tt + 1:ref[tt] + 1] = 1.0
        msk[l] = (blk[:, None] == blk[None, :]) & second[:, None] & (~second)[None, :]
    msk[nl] = np.eye(c)
    e[nl] = np.tril(np.ones((c, c)))
    e[nl + 1] = (t[None, :] > t[:, None])
    e1 = e[:, ::-1, ::-1]
    m1 = msk[:, ::-1, ::-1]
    e_all = np.stack([e.reshape(-1, c), e1.reshape(-1, c)])
    m_all = np.stack([msk, m1])
    return jnp.asarray(e_all, BF16), jnp.asarray(m_all, F32), nl


def _hgrn_scan_kernel(q_ref, v_ref, z_ref, lb_ref, e_ref, m_ref, s0_ref, o_ref, sf_ref, st_scr, *, c, nl):
    j = pl.program_id(2)

    @pl.when(j == 0)
    def _():
        st_scr[...] = s0_ref[...]

    lb = lb_ref[...]
    z = z_ref[...]
    logf = jnp.log(lb + (1.0 - lb) * _sigmoid(z))
    kin = (1.0 - lb) * _sigmoid(-z)
    ex = jnp.exp(_exact_dot(e_ref[...], logf))
    dec_tot = jnp.exp(jnp.sum(logf, axis=0, keepdims=True))
    q = q_ref[...]
    v = v_ref[...]
    for h in range(HG_HEADS):
        sl = slice(h * HG_DK, (h + 1) * HG_DK)
        qh, kh, vh = q[:, sl], kin[:, sl], v[:, sl]
        a = m_ref[nl] * _bdot_nt(qh, kh)
        for l in range(nl):
            f = ex[l * c:(l + 1) * c, sl]
            a = a + m_ref[l] * _bdot_nt(qh * f, kh * f)
        st = st_scr[h]
        o_ref[:, sl] = _bdot_nt(qh * ex[nl * c:(nl + 1) * c, sl], st) + _bdot(a, vh)
        st_scr[h] = st * dec_tot[:, sl] + _bdot_tn(vh, kh * ex[(nl + 1) * c:(nl + 2) * c, sl])

    @pl.when(j == pl.num_programs(2) - 1)
    def _():
        sf_ref[...] = st_scr[...]


def _hgrn_scan(p, lb_row, s0, *, bsz, seq, name):
    c = min(HG_CHUNK, seq)
    nc = seq // c
    e_all, m_all, nl = _hgrn_tables(c)
    row = lambda b, d, j: b * nc + j + d * (nc - 1 - 2 * j)
    kern = functools.partial(_hgrn_scan_kernel, c=c, nl=nl)
    st_shape = (HG_HEADS, HG_DK, HG_DK)
    return pl.pallas_call(
        kern,
        out_shape=(jax.ShapeDtypeStruct((2, bsz * seq, D_MODEL), F32),
                   jax.ShapeDtypeStruct((bsz, 2) + st_shape, F32)),
        grid=(bsz, 2, nc),
        in_specs=[pl.BlockSpec((c, D_MODEL), lambda b, d, j: (row(b, d, j), 0)),
                  pl.BlockSpec((c, D_MODEL), lambda b, d, j: (row(b, d, j), 1)),
                  pl.BlockSpec((c, D_MODEL), lambda b, d, j: (row(b, d, j), 3 + d)),
                  pl.BlockSpec((1, D_MODEL), lambda b, d, j: (0, 0)),
                  pl.BlockSpec((None,) + e_all.shape[1:], lambda b, d, j: (d, 0, 0)),
                  pl.BlockSpec((None,) + m_all.shape[1:], lambda b, d, j: (d, 0, 0, 0)),
                  pl.BlockSpec((None, None) + st_shape, lambda b, d, j: (b, d, 0, 0, 0))],
        out_specs=(pl.BlockSpec((None, c, D_MODEL), lambda b, d, j: (d, row(b, d, j), 0)),
                   pl.BlockSpec((None, None) + st_shape, lambda b, d, j: (b, d, 0, 0, 0))),
        scratch_shapes=[pltpu.VMEM(st_shape, F32)],
        compiler_params=_cp("parallel", "parallel", "arbitrary"),
        name=name,
    )(p, p, p, lb_row, e_all, m_all, s0)


def _hgrn_out_kernel(o0_ref, o1_ref, gate_ref, ng_ref, w_ref, x_ref, g_ref, out_ref, act_scr):
    o = o0_ref[...] + o1_ref[...]
    gate = gate_ref[...]
    ng = ng_ref[...]
    for h in range(HG_HEADS):
        sl = slice(h * HG_DK, (h + 1) * HG_DK)
        act_scr[:, sl] = (_rms(o[:, sl]) * ng[:, sl] * _silu(gate[:, sl])).astype(BF16)
    out_ref[...] = x_ref[...] + g_ref[...] * jnp.dot(act_scr[...], w_ref[...], preferred_element_type=F32)


def _hgrn_out(o, p, norm_row, w_out, x2d, gate, *, seq, tm, name):
    m = x2d.shape[0]
    tpb = seq // tm
    return pl.pallas_call(
        _hgrn_out_kernel,
        out_shape=jax.ShapeDtypeStruct((m, D_MODEL), F32),
        grid=(m // tm,),
        in_specs=[pl.BlockSpec((None, tm, D_MODEL), lambda i: (0, i, 0)),
                  pl.BlockSpec((None, tm, D_MODEL), lambda i: (1, i, 0)),
                  pl.BlockSpec((tm, D_MODEL), lambda i: (i, 2)),
                  pl.BlockSpec((1, D_MODEL), lambda i: (0, 0)),
                  pl.BlockSpec((D_MODEL, D_MODEL), lambda i: (0, 0)),
                  pl.BlockSpec((tm, D_MODEL), lambda i: (i, 0)),
                  pl.BlockSpec((None, 1, D_MODEL), lambda i: (i // tpb, 0, 0))],
        out_specs=pl.BlockSpec((tm, D_MODEL), lambda i: (i, 0)),
        scratch_shapes=[pltpu.VMEM((tm, D_MODEL), BF16)],
        compiler_params=_cp("parallel"),
        name=name,
    )(o, o, p, norm_row, w_out, x2d, gate)


def _rope_kernel(q_ref, qr_ref, k_ref, kr_ref, cos_ref, sin_ref, qo_ref, ko_ref):
    cos = cos_ref[...]
    sin = sin_ref[...]
    qo_ref[...] = (q_ref[...] * cos + qr_ref[...] * sin) * (AT_HD ** -0.5)
    ko_ref[...] = k_ref[...] * cos[:, :AT_KV_DIM] + kr_ref[...] * sin[:, :AT_KV_DIM]


def _rope(p, cos_t, sin_t, *, seq, tm):
    m = p.shape[0]
    tpb = seq // tm
    return pl.pallas_call(
        _rope_kernel,
        out_shape=(jax.ShapeDtypeStruct((m, D_MODEL), F32), jax.ShapeDtypeStruct((m, AT_KV_DIM), F32)),
        grid=(m // tm,),
        in_specs=[pl.BlockSpec((tm, D_MODEL), lambda i: (i, 0)),
                  pl.BlockSpec((tm, D_MODEL), lambda i: (i, 1)),
                  pl.BlockSpec((tm, AT_KV_DIM), lambda i: (i, 8)),
                  pl.BlockSpec((tm, AT_KV_DIM), lambda i: (i, 10)),
                  pl.BlockSpec((tm, D_MODEL), lambda i: (i % tpb, 0)),
                  pl.BlockSpec((tm, D_MODEL), lambda i: (i % tpb, 0))],
        out_specs=(pl.BlockSpec((tm, D_MODEL), lambda i: (i, 0)),
                   pl.BlockSpec((tm, AT_KV_DIM), lambda i: (i, 0))),
        compiler_params=_cp("parallel"),
        name="at_rope",
    )(p, p, p, p, cos_t, sin_t)


def _sink_attend(q, keys, vals, mask, sink_ref, o_ref):
    t = q.shape[0]
    for g in range(AT_KV_HEADS):
        kg = keys[:, g * AT_HD:(g + 1) * AT_HD]
        vg = vals[:, g * AT_HD:(g + 1) * AT_HD]
        for r in range(AT_GROUP):
            hd = g * AT_GROUP + r
            qh = q[:, hd * AT_HD:(hd + 1) * AT_HD]
            logits = _bdot_nt(qh, kg)
            if mask is not None:
                logits = jnp.where(mask, logits, -jnp.inf)
            s = sink_ref[hd]
            mx = jnp.maximum(jnp.max(logits, axis=-1, keepdims=True), s)
            p = jnp.exp(logits - mx)
            denom = jnp.sum(p, axis=-1, keepdims=True) + jnp.exp(s - mx)
            o_ref[:, hd * AT_HD:(hd + 1) * AT_HD] = _bdot(p, vg) / denom


def _attn_kernel(sink_ref, q_ref, kp_ref, kc_ref, kn_ref, vp_ref, vc_ref, vn_ref, ck_ref, cv_ref, o_ref, *, nctx):
    i = pl.program_id(1)
    nb = pl.num_programs(1)
    keys = jnp.concatenate([ck_ref[...], kp_ref[...], kc_ref[...], kn_ref[...]], axis=0)
    vals = jnp.concatenate([cv_ref[...], vp_ref[...], vc_ref[...], vn_ref[...]], axis=0)
    nk = nctx + 3 * AT_BLOCK
    ti = lax.broadcasted_iota(jnp.int32, (AT_BLOCK, nk), 0)
    kj = lax.broadcasted_iota(jnp.int32, (AT_BLOCK, nk), 1) - nctx
    lo = jnp.where(i > 0, 0, AT_BLOCK)
    hi = jnp.where(i < nb - 1, 3 * AT_BLOCK, 2 * AT_BLOCK)
    rel = kj - ti
    mask = (kj < 0) | ((rel >= 0) & (rel <= 2 * AT_BLOCK) & (kj >= lo) & (kj < hi))
    _sink_attend(q_ref[...], keys, vals, mask, sink_ref, o_ref)


def _attn(sink, q_r, k_r, p, p_ctx, *, bsz, seq, nctx):
    nb = seq // AT_BLOCK
    blk = lambda b, i: b * nb + i
    prv = lambda b, i: b * nb + jnp.maximum(i - 1, 0)
    nxt = lambda b, i: b * nb + jnp.minimum(i + 1, nb - 1)
    kspec = lambda f: pl.BlockSpec((AT_BLOCK, AT_KV_DIM), lambda b, i: (f(b, i), 0))
    vspec = lambda f: pl.BlockSpec((AT_BLOCK, AT_KV_DIM), lambda b, i: (f(b, i), 9))
    return pl.pallas_call(
        functools.partial(_attn_kernel, nctx=nctx),
        out_shape=jax.ShapeDtypeStruct((bsz * seq, D_MODEL), F32),
        grid=(bsz, nb),
        in_specs=[pl.BlockSpec(memory_space=pltpu.SMEM),
                  pl.BlockSpec((AT_BLOCK, D_MODEL), lambda b, i: (blk(b, i), 0)),
                  kspec(prv), kspec(blk), kspec(nxt), vspec(prv), vspec(blk), vspec(nxt),
                  pl.BlockSpec((nctx, AT_KV_DIM), lambda b, i: (b, 8)),
                  pl.BlockSpec((nctx, AT_KV_DIM), lambda b, i: (b, 9))],
        out_specs=pl.BlockSpec((AT_BLOCK, D_MODEL), lambda b, i: (blk(b, i), 0)),
        compiler_params=_cp("parallel", "parallel"),
        name="at_window",
    )(sink, q_r, k_r, k_r, k_r, p, p, p, p_ctx, p_ctx)


def _attn_ctx_kernel(sink_ref, q_ref, k_ref, v_ref, o_ref):
    _sink_attend(q_ref[...] * (AT_HD ** -0.5), k_ref[...], v_ref[...], None, sink_ref, o_ref)


def _attn_ctx(sink, p_ctx, *, bsz, nctx):
    return pl.pallas_call(
        _attn_ctx_kernel,
        out_shape=jax.ShapeDtypeStruct((bsz * nctx, D_MODEL), F32),
        grid=(bsz,),
        in_specs=[pl.BlockSpec(memory_space=pltpu.SMEM),
                  pl.BlockSpec((nctx, D_MODEL), lambda b: (b, 0)),
                  pl.BlockSpec((nctx, AT_KV_DIM), lambda b: (b, 8)),
                  pl.BlockSpec((nctx, AT_KV_DIM), lambda b: (b, 9))],
        out_specs=pl.BlockSpec((nctx, D_MODEL), lambda b: (b, 0)),
        compiler_params=_cp("parallel"),
        name="at_ctx",
    )(sink, p_ctx, p_ctx, p_ctx)


def _ret_scan_kernel(q_ref, k_ref, v_ref, cos_ref, sin_ref, dm_ref, qin_ref, kout_ref, car_ref, s0_ref,
                     o_ref, sf_ref, st_scr):
    j = pl.program_id(2)

    @pl.when(j == 0)
    def _():
        st_scr[...] = s0_ref[...]

    cos = cos_ref[...]
    sin = sin_ref[...]
    half = RT_DK // 2

    def rope(x):
        rot = jnp.concatenate([-x[:, half:], x[:, :half]], axis=1)
        return x * cos + rot * sin

    for h in range(RT_HEADS):
        sk = slice(h * RT_DK, (h + 1) * RT_DK)
        sv = slice(h * RT_DV, (h + 1) * RT_DV)
        qh = rope(q_ref[:, sk])
        kh = rope(k_ref[:, sk] * (RT_DK ** -0.5))
        vh = v_ref[:, sv]
        s = _bdot_nt(qh, kh) * dm_ref[h]
        st = st_scr[h]
        o_ref[:, sv] = _bdot(qh * qin_ref[:, sk], st) + _bdot(s, vh)
        st_scr[h] = car_ref[:, sv] * st + _bdot_tn(kh * kout_ref[:, sk], vh)

    @pl.when(j == pl.num_programs(2) - 1)
    def _():
        sf_ref[...] = st_scr[...]


def _ret_tables(decay_exp, c):
    lg = jnp.log1p(-jnp.exp2(decay_exp.astype(F32)))
    idx = jnp.arange(c, dtype=F32)
    pos = jnp.stack([idx, c - 1.0 - idx])
    rel = pos[:, :, None] - pos[:, None, :]
    lgh = lg[:, :, None, None]
    dm = jnp.where(rel[:, None] >= 0, jnp.exp(lgh * jnp.maximum(rel[:, None], 0.0)), 0.0)
    qin = jnp.exp(lg[:, None, :] * (pos[:, :, None] + 1.0))
    kout = jnp.exp(lg[:, None, :] * (c - 1.0 - pos[:, :, None]))
    car = jnp.exp(lg * c)
    qin = jnp.repeat(qin, RT_DK, axis=-1)
    kout = jnp.repeat(kout, RT_DK, axis=-1)
    car = jnp.repeat(car, RT_DV, axis=-1)[:, None, :]
    return dm, qin, kout, car


def _ret_scan(p, cos_t, sin_t, decay_exp, s0, *, bsz, seq, name):
    c = min(RT_CHUNK, seq)
    nc = seq // c
    dm, qin, kout, car = _ret_tables(decay_exp, c)
    crow = lambda b, d, j: j + d * (nc - 1 - 2 * j)
    row = lambda b, d, j: b * nc + crow(b, d, j)
    st_shape = (RT_HEADS, RT_DK, RT_DV)
    return pl.pallas_call(
        _ret_scan_kernel,
        out_shape=(jax.ShapeDtypeStruct((2, bsz * seq, RT_V_DIM), F32),
                   jax.ShapeDtypeStruct((bsz, 2) + st_shape, F32)),
        grid=(bsz, 2, nc),
        in_specs=[pl.BlockSpec((c, D_MODEL), lambda b, d, j: (row(b, d, j), 0)),
                  pl.BlockSpec((c, D_MODEL), lambda b, d, j: (row(b, d, j), 1)),
                  pl.BlockSpec((c, RT_V_DIM), lambda b, d, j: (row(b, d, j), 1)),
                  pl.BlockSpec((c, RT_DK), lambda b, d, j: (crow(b, d, j), 0)),
                  pl.BlockSpec((c, RT_DK), lambda b, d, j: (crow(b, d, j), 0)),
                  pl.BlockSpec((None, RT_HEADS, c, c), lambda b, d, j: (d, 0, 0, 0)),
                  pl.BlockSpec((None, c, D_MODEL), lambda b, d, j: (d, 0, 0)),
                  pl.BlockSpec((None, c, D_MODEL), lambda b, d, j: (d, 0, 0)),
                  pl.BlockSpec((None, 1, RT_V_DIM), lambda b, d, j: (d, 0, 0)),
                  pl.BlockSpec((None, None) + st_shape, lambda b, d, j: (b, d, 0, 0, 0))],
        out_specs=(pl.BlockSpec((None, c, RT_V_DIM), lambda b, d, j: (d, row(b, d, j), 0)),
                   pl.BlockSpec((None, None) + st_shape, lambda b, d, j: (b, d, 0, 0, 0))),
        scratch_shapes=[pltpu.VMEM(st_shape, F32)],
        compiler_params=_cp("parallel", "parallel", "arbitrary"),
        name=name,
    )(p, p, p, cos_t, sin_t, dm, qin, kout, car, s0)


def _ret_out_kernel(o0_ref, o1_ref, gate_ref, w_ref, x_ref, g_ref, out_ref, act_scr):
    for h in range(RT_HEADS):
        sv = slice(h * RT_DV, (h + 1) * RT_DV)
        o = o0_ref[:, sv] + o1_ref[:, sv]
        act_scr[:, sv] = (_silu(gate_ref[:, sv]) * _rms(o)).astype(BF16)
    out_ref[...] = x_ref[...] + g_ref[...] * jnp.dot(act_scr[...], w_ref[...], preferred_element_type=F32)


def _ret_out(o, p, w_out, x2d, gate, *, seq, tm, name):
    m = x2d.shape[0]
    tpb = seq // tm
    return pl.pallas_call(
        _ret_out_kernel,
        out_shape=jax.ShapeDtypeStruct((m, D_MODEL), F32),
        grid=(m // tm,),
        in_specs=[pl.BlockSpec((None, tm, RT_V_DIM), lambda i: (0, i, 0)),
                  pl.BlockSpec((None, tm, RT_V_DIM), lambda i: (1, i, 0)),
                  pl.BlockSpec((tm, RT_V_DIM), lambda i: (i, 2)),
                  pl.BlockSpec((RT_V_DIM, D_MODEL), lambda i: (0, 0)),
                  pl.BlockSpec((tm, D_MODEL), lambda i: (i, 0)),
                  pl.BlockSpec((None, 1, D_MODEL), lambda i: (i // tpb, 0, 0))],
        out_specs=pl.BlockSpec((tm, D_MODEL), lambda i: (i, 0)),
        scratch_shapes=[pltpu.VMEM((tm, RT_V_DIM), BF16)],
        compiler_params=_cp("parallel"),
        name=name,
    )(o, o, p, w_out, x2d, gate)


def _seg_sum(x, e_down, e_up):
    hi = x.astype(BF16)
    lo = (x - hi.astype(F32)).astype(BF16)
    s = jnp.dot(hi, e_down, preferred_element_type=F32) + jnp.dot(lo, e_down, preferred_element_type=F32)
    return _exact_dot_r(s, e_up)


def _rw_proj_kernel(x_ref, xp_ref, xn_ref, mul_ref, add_ref, mix_ref, wrkv_ref, w0_ref, w1_ref, w2_ref,
                    a0_ref, a1_ref, a2_ref, g1_ref, g2_ref, kk_ref, ka_ref, rk_ref, ed_ref, eu_ref,
                    r_out, v_out, kn_out, gate_out, bonus_out, lw_out, kd_out, bb_out,
                    *, tm, width, tpi, quarters):
    i = pl.program_id(0)
    first = (i % tpi) == 0
    last = (i % tpi) == tpi - 1
    mul = mul_ref[...]
    add = add_ref[...]
    hmod = lambda x: _rms(x) * mul + add
    hm = hmod(x_ref[...])
    col = lax.broadcasted_iota(jnp.int32, (tm, 1), 0) & (width - 1)
    srcs = {}
    if 'l' in quarters:
        srcs['l'] = jnp.where(col != 0, pltpu.roll(hm, 1, 0), 0.0)
    if 'r' in quarters:
        srcs['r'] = jnp.where(col != width - 1, pltpu.roll(hm, tm - 1, 0), 0.0)
    if 'u' in quarters:
        hp = jnp.where(first, 0.0, hmod(xp_ref[...]))
        srcs['u'] = jnp.concatenate([hp, hm[: tm - width]], axis=0) if tm > width else hp
    if 'd' in quarters:
        hn = jnp.where(last, 0.0, hmod(xn_ref[...]))
        srcs['d'] = jnp.concatenate([hm[width:], hn], axis=0) if tm > width else hn
    qd = D_MODEL // 4
    shifted = jnp.concatenate([srcs[q][:, n * qd:(n + 1) * qd] for n, q in enumerate(quarters)], axis=1)
    xx = shifted - hm
    mixed = lambda n: hm + xx * mix_ref[n:n + 1, :]
    r = _bdot(mixed(0), wrkv_ref[0])
    k = _bdot(mixed(2), wrkv_ref[1])
    v = _bdot(mixed(3), wrkv_ref[2])
    xw = mixed(1)
    xa = mixed(4)
    ed = ed_ref[...]
    eu = eu_ref[...]
    kkh = k * kk_ref[...]
    nrm = jnp.sqrt(_seg_sum(kkh * kkh, ed, eu))
    kn = kkh / jnp.maximum(nrm, 1e-12)
    ksum = jnp.zeros_like(k)
    for z in range(2):
        w_raw = w0_ref[z] + _bdot(jnp.tanh(_bdot(xw, w1_ref[z])), w2_ref[z])
        t = -w_raw
        softplus = jnp.maximum(t, 0.0) + jnp.log1p(jnp.exp(-jnp.abs(t)))
        lw_out[z] = -jnp.exp(-softplus - 0.5)
        a = _sigmoid(a0_ref[z] + _bdot(_bdot(xa, a1_ref[z]), a2_ref[z]))
        kd = k * (1.0 + (a - 1.0) * ka_ref[...])
        kd_out[z] = kd
        bb_out[z] = kn * a
        ksum = ksum + kd
    r_out[...] = r
    v_out[...] = v
    kn_out[...] = kn
    gate_out[...] = _bdot(_sigmoid(_bdot(mixed(5), g1_ref[...])), g2_ref[...])
    bonus_out[...] = _seg_sum(r * ksum * rk_ref[...], ed, eu) * v


def _seg_mats():
    hid = np.arange(D_MODEL) // RW_HEAD
    e_down = (hid[:, None] == np.arange(128)[None, :]).astype(np.float32)
    return jnp.asarray(e_down, BF16), jnp.asarray(e_down.T, BF16)


def _rw_proj(x2d, mul, add, wts, *, seq, width, tm, quarters, name):
    m = x2d.shape[0]
    tpi = seq // tm
    rpt = tm // width
    nrow = m // width
    e_down, e_up = _seg_mats()
    full = lambda a: pl.BlockSpec(a.shape, lambda i: (0,) * a.ndim)
    consts = [wts['mix'], wts['w_rkv'], wts['w0'], wts['w1'], wts['w2'], wts['a0'], wts['a1'], wts['a2'],
              wts['g1'], wts['g2'], wts['k_k'], wts['k_a'], wts['r_k'], e_down, e_up]
    kern = functools.partial(_rw_proj_kernel, tm=tm, width=width, tpi=tpi, quarters=quarters)
    one = jax.ShapeDtypeStruct((m, D_MODEL), F32)
    two = jax.ShapeDtypeStruct((2, m, D_MODEL), F32)
    ospec1 = pl.BlockSpec((tm, D_MODEL), lambda i: (i, 0))
    ospec2 = pl.BlockSpec((2, tm, D_MODEL), lambda i: (0, i, 0))
    return pl.pallas_call(
        kern,
        out_shape=(one, one, one, one, one, two, two, two),
        grid=(m // tm,),
        in_specs=[pl.BlockSpec((tm, D_MODEL), lambda i: (i, 0)),
                  pl.BlockSpec((width, D_MODEL), lambda i: (jnp.maximum(i * rpt - 1, 0), 0)),
                  pl.BlockSpec((width, D_MODEL), lambda i: (jnp.minimum((i + 1) * rpt, nrow - 1), 0)),
                  pl.BlockSpec((None, 1, D_MODEL), lambda i: (i // tpi, 0, 0)),
                  pl.BlockSpec((None, 1, D_MODEL), lambda i: (i // tpi, 0, 0))] + [full(a) for a in consts],
        out_specs=(ospec1, ospec1, ospec1, ospec1, ospec1, ospec2, ospec2, ospec2),
        compiler_params=_cp("parallel"),
        name=name,
    )(x2d, x2d, x2d, mul, add, *consts)


def _rw_tables(c):
    r = c * RW_GROUP_HEADS
    t = np.arange(r) % c
    hrow = np.arange(r) // c
    same = hrow[:, None] == hrow[None, :]
    strict0 = same & (t[None, :] < t[:, None])
    incl0 = same & (t[None, :] <= t[:, None])
    strict1 = same & (t[None, :] > t[:, None])
    incl1 = same & (t[None, :] >= t[:, None])
    tri0 = np.tril(np.ones((c, c)))
    tri = np.stack([tri0, tri0.T])
    lane_h = np.arange(RW_GROUP) // RW_HEAD
    hmask = hrow[:, None] == lane_h[None, :]
    bdm = lane_h[:, None] == lane_h[None, :]
    ms = np.stack([strict0, strict1]).astype(np.float32)
    mi = np.stack([incl0, incl1]).astype(np.float32)
    return (jnp.asarray(tri, BF16), jnp.asarray(ms), jnp.asarray(mi), jnp.asarray(hmask.astype(np.float32)),
            jnp.asarray(bdm.astype(np.float32)))


def _rw_scan_kernel(r_ref, kn_ref, v_ref, lw_ref, kd_ref, bb_ref, tri_ref, ms_ref, mi_ref, hm_ref, bdm_ref, s0_ref,
                    y_ref, sf_ref, st_scr, *, c):
    j = pl.program_id(2)

    @pl.when(j == 0)
    def _():
        st_scr[...] = s0_ref[...]

    lw = lw_ref[...]
    g = _exact_dot(tri_ref[...], lw)
    gtot = jnp.sum(lw, axis=0, keepdims=True)
    e_incl = jnp.exp(g)
    e_excl = jnp.exp(g - lw)
    e_inv = jnp.exp(-g)
    e_end = jnp.exp(gtot - g)
    dec_tot = jnp.exp(gtot)
    kt = kn_ref[...] * e_excl
    rt = r_ref[...] * e_incl
    kd = kd_ref[...]
    bb = bb_ref[...]
    kh = kd * e_inv
    bh = bb * e_inv
    ke = kd * e_end
    be = bb * e_end
    v = v_ref[...]
    hm = hm_ref[...]
    ms = ms_ref[...]
    mi = mi_ref[...]
    bdm = bdm_ref[...]
    nrow = c * RW_GROUP_HEADS
    eye = (lax.broadcasted_iota(jnp.int32, (nrow, nrow), 0) == lax.broadcasted_iota(jnp.int32, (nrow, nrow), 1)).astype(F32)
    tile = lambda x: jnp.concatenate([x] * RW_GROUP_HEADS, axis=0)
    unstack = lambda x: functools.reduce(lambda a, b: a + b, [x[n * c:(n + 1) * c] for n in range(RW_GROUP_HEADS)])
    nsq = int(math.log2(c)) - 1
    for gi in range(RW_NGROUPS):
        sl = slice(gi * RW_GROUP, (gi + 1) * RW_GROUP)
        xk = tile(kt[:, sl]) * hm
        xr = tile(rt[:, sl]) * hm
        yb = tile(bh[:, sl])
        yk = tile(kh[:, sl])
        vm = tile(v[:, sl]) * hm
        l_b = _bdot_nt(xk, yb) * ms
        l_k = _bdot_nt(xk, yk) * ms
        m_k = _bdot_nt(xr, yk) * mi
        m_b = _bdot_nt(xr, yb) * mi
        n = -l_b
        p = eye + n
        for _ in range(nsq):
            n = _bdot(n, n)
            p = p + _bdot(p, n)
        st = st_scr[gi]
        u = _bdot(p, _bdot_nt(xk, st) + _bdot(l_k, vm))
        ym = _bdot_nt(xr, st) + _bdot(m_k, vm) - _bdot(m_b, u)
        y_ref[:, sl] = unstack(ym)
        lhs = jnp.concatenate([v[:, sl], -unstack(u)], axis=0)
        rhs = jnp.concatenate([ke[:, sl], be[:, sl]], axis=0)
        st_scr[gi] = st * dec_tot[:, sl] + bdm * _bdot_tn(lhs, rhs)

    @pl.when(j == pl.num_programs(2) - 1)
    def _():
        sf_ref[...] = st_scr[...]


def _rw_scan(r, kn, v, lw, kd, bb, s0, *, bsz, seq, name):
    c = min(RW_CHUNK, seq)
    nc = seq // c
    tri, ms, mi, hmask, bdm = _rw_tables(c)
    row = lambda b, d, j: b * nc + j + d * (nc - 1 - 2 * j)
    st_shape = (RW_NGROUPS, RW_GROUP, RW_GROUP)
    shared = pl.BlockSpec((c, D_MODEL), lambda b, d, j: (row(b, d, j), 0))
    perdir = pl.BlockSpec((None, c, D_MODEL), lambda b, d, j: (d, row(b, d, j), 0))
    nrow = c * RW_GROUP_HEADS
    return pl.pallas_call(
        functools.partial(_rw_scan_kernel, c=c),
        out_shape=(jax.ShapeDtypeStruct((2, bsz * seq, D_MODEL), F32),
                   jax.ShapeDtypeStruct((bsz, 2) + st_shape, F32)),
        grid=(bsz, 2, nc),
        in_specs=[shared, shared, shared, perdir, perdir, perdir,
                  pl.BlockSpec((None, c, c), lambda b, d, j: (d, 0, 0)),
                  pl.BlockSpec((None, nrow, nrow), lambda b, d, j: (d, 0, 0)),
                  pl.BlockSpec((None, nrow, nrow), lambda b, d, j: (d, 0, 0)),
                  pl.BlockSpec((nrow, RW_GROUP), lambda b, d, j: (0, 0)),
                  pl.BlockSpec((RW_GROUP, RW_GROUP), lambda b, d, j: (0, 0)),
                  pl.BlockSpec((None, None) + st_shape, lambda b, d, j: (b, d, 0, 0, 0))],
        out_specs=(perdir, pl.BlockSpec((None, None) + st_shape, lambda b, d, j: (b, d, 0, 0, 0))),
        scratch_shapes=[pltpu.VMEM(st_shape, F32)],
        compiler_params=_cp("parallel", "parallel", "arbitrary"),
        name=name,
    )(r, kn, v, lw, kd, bb, tri, ms, mi, hmask, bdm, s0)


def _rw_out_kernel(y0_ref, y1_ref, bonus_ref, gate_ref, lnw_ref, lnb_ref, ed_ref, eu_ref, w_ref, x_ref, g_ref,
                   out_ref):
    y = y0_ref[...] + y1_ref[...]
    ed = ed_ref[...]
    eu = eu_ref[...]
    mu = _seg_sum(y, ed, eu) * (1.0 / RW_HEAD)
    dlt = y - mu
    var = _seg_sum(dlt * dlt, ed, eu) * (1.0 / RW_HEAD)
    yn = dlt * lax.rsqrt(var + RW_GN_EPS) * lnw_ref[...] + lnb_ref[...] + bonus_ref[...]
    out_ref[...] = x_ref[...] + g_ref[...] * _bdot(yn * gate_ref[...], w_ref[...])


def _rw_out(y, bonus, gate_act, ln_w, ln_b, w_out, x2d, gate, *, seq, tm, name):
    m = x2d.shape[0]
    tpb = seq // tm
    e_down, e_up = _seg_mats()
    rowspec = pl.BlockSpec((tm, D_MODEL), lambda i: (i, 0))
    full = lambda a: pl.BlockSpec(a.shape, lambda i: (0,) * a.ndim)
    return pl.pallas_call(
        _rw_out_kernel,
        out_shape=jax.ShapeDtypeStruct((m, D_MODEL), F32),
        grid=(m // tm,),
        in_specs=[pl.BlockSpec((None, tm, D_MODEL), lambda i: (0, i, 0)),
                  pl.BlockSpec((None, tm, D_MODEL), lambda i: (1, i, 0)),
                  rowspec, rowspec, full(ln_w), full(ln_b), full(e_down), full(e_up), full(w_out), rowspec,
                  pl.BlockSpec((None, 1, D_MODEL), lambda i: (i // tpb, 0, 0))],
        out_specs=rowspec,
        compiler_params=_cp("parallel"),
        name=name,
    )(y, y, bonus, gate_act, ln_w, ln_b, e_down, e_up, w_out, x2d, gate)


def _rotate_half_cols(w, n_seg):
    sh = w.shape
    ws = w.reshape(sh[:-1] + (n_seg, 2, sh[-1] // (2 * n_seg)))
    return jnp.concatenate([-ws[..., 1:, :], ws[..., :1, :]], axis=-2).reshape(sh)


def _axial_tables(n_tokens):
    t = jnp.arange(n_tokens)
    row = (t // GRID_W).astype(F32)
    col = (t % GRID_W).astype(F32)
    inv = ROPE_BASE ** (-jnp.arange(0, AXIS_DIM, 2, dtype=F32) / AXIS_DIM)
    ang_r = row[:, None] * inv[None, :]
    ang_c = col[:, None] * inv[None, :]
    ang = jnp.concatenate([ang_r, ang_r, ang_c, ang_c], axis=-1)
    return jnp.tile(jnp.cos(ang), (1, AT_HEADS)), jnp.tile(jnp.sin(ang), (1, AT_HEADS))


def _retention_tables(n_tokens):
    t = jnp.arange(n_tokens, dtype=F32)
    inv = ROPE_BASE ** (-jnp.linspace(0.0, 1.0, RT_DK // 2, dtype=F32))
    ang = t[:, None] * inv[None, :]
    ang = jnp.concatenate([ang, ang], axis=-1)
    return jnp.cos(ang), jnp.sin(ang)


def _forward(x, c, ctx, c_ctx, ada_w, ada_b, norm1_g, norm2_g, ffn_w_up, ffn_conv_w, ffn_conv_b, ffn_w_down,
             hg_w_in, hg_lb_logits, hg_norm_g, hg_w_out, at_w_qkv, at_b_qkv, at_sink, at_w_out,
             rt_w_in, rt_decay_exp, rt_w_out, rw_mix, rw_w_rkv, rw_w0, rw_w1, rw_w2, rw_a0, rw_a1, rw_a2,
             rw_g1, rw_g2, rw_k_k, rw_k_a, rw_r_k, rw_ln_w, rw_ln_b, rw_w_out, final_norm_g, depth):
    bsz, seq, d = x.shape
    nctx = ctx.shape[1]
    tm_l = 512 if seq % 512 == 0 else seq
    tm_f = 256 if seq % 256 == 0 else seq
    tm_c = nctx

    cvecs = jnp.zeros((8, d), F32).at[:bsz].set(c).at[bsz].set(c_ctx)
    mod = _adaln(cvecs, ada_w[:depth], ada_b[:depth])
    lb_all = jnp.cumsum(jax.nn.softmax(hg_lb_logits.astype(F32), axis=0), axis=0)
    zero_b = lambda n: jnp.zeros((1, n), F32)

    xl = x.reshape(bsz * seq, d)
    xc = ctx.reshape(bsz * nctx, d)
    for i in range(depth):
        kind, j = i % 4, i // 4
        need_ctx = i < depth - 1
        ml = mod[i, :bsz].reshape(bsz, 6, 1, d)
        mc = jnp.broadcast_to(mod[i, bsz].reshape(1, 6, 1, d), (bsz, 6, 1, d))
        sh1, sc1, g1, sh2, sc2, g2 = (ml[:, n] for n in range(6))
        csh1, csc1, cg1, csh2, csc2, cg2 = (mc[:, n] for n in range(6))
        mul1, cmul1 = norm1_g[i] * (1.0 + sc1), norm1_g[i] * (1.0 + csc1)
        mul2, cmul2 = norm2_g[i] * (1.0 + sc2), norm2_g[i] * (1.0 + csc2)

        if kind == 0:
            w_in = hg_w_in[j].astype(BF16)
            w_out = hg_w_out[j].astype(BF16)
            lb_row = lb_all[i].reshape(1, d)
            ng_row = jnp.tile(hg_norm_g[j], HG_HEADS).reshape(1, d)
            pc = _proj(xc, cmul1, csh1, w_in, zero_b(5 * d), seq=nctx, tm=tm_c, name="hg_proj_ctx")
            pL = _proj(xl, mul1, sh1, w_in, zero_b(5 * d), seq=seq, tm=tm_l, name="hg_proj")
            s0 = jnp.zeros((bsz, 2, HG_HEADS, HG_DK, HG_DK), F32)
            oc, s_ctx = _hgrn_scan(pc, lb_row, s0, bsz=bsz, seq=nctx, name="hg_scan_ctx")
            ol, _ = _hgrn_scan(pL, lb_row, s_ctx, bsz=bsz, seq=seq, name="hg_scan")
            xl = _hgrn_out(ol, pL, ng_row, w_out, xl, g1, seq=seq, tm=tm_l, name="hg_out")
            if need_ctx:
                xc = _hgrn_out(oc, pc, ng_row, w_out, xc, cg1, seq=nctx, tm=tm_c, name="hg_out_ctx")
        elif kind == 1:
            wq = at_w_qkv[j]
            bq = at_b_qkv[j]
            qd = AT_HEADS * AT_HD
            wq_h = wq[:, :qd].reshape(d, AT_HEADS, AT_HD)
            wk_h = wq[:, qd:qd + AT_KV_DIM].reshape(d, AT_KV_HEADS, AT_HD)
            bq_h = bq[:qd].reshape(AT_HEADS, AT_HD)
            bk_h = bq[qd:qd + AT_KV_DIM].reshape(AT_KV_HEADS, AT_HD)
            w_ext = jnp.concatenate([wq[:, :qd], _rotate_half_cols(wq_h, 2).reshape(d, qd), wq[:, qd:],
                                     _rotate_half_cols(wk_h, 2).reshape(d, AT_KV_DIM)], axis=1).astype(BF16)
            b_ext = jnp.concatenate([bq[:qd], _rotate_half_cols(bq_h, 2).reshape(qd), bq[qd:],
                                     _rotate_half_cols(bk_h, 2).reshape(AT_KV_DIM)]).reshape(1, -1)
            w_out = at_w_out[j].astype(BF16)
            sink = at_sink[j].astype(F32)
            cos2, sin2 = _axial_tables(seq)
            pc = _proj(xc, cmul1, csh1, w_ext, b_ext, seq=nctx, tm=tm_c, name="at_proj_ctx")
            pL = _proj(xl, mul1, sh1, w_ext, b_ext, seq=seq, tm=tm_l, name="at_proj")
            q_r, k_r = _rope(pL, cos2, sin2, seq=seq, tm=tm_l)
            ol = _attn(sink, q_r, k_r, pL, pc, bsz=bsz, seq=seq, nctx=nctx)
            xl = _mm_res(ol, w_out, xl, g1, seq=seq, tm=tm_l, name="at_out")
            if need_ctx:
                oc = _attn_ctx(sink, pc, bsz=bsz, nctx=nctx)
                xc = _mm_res(oc, w_out, xc, cg1, seq=nctx, tm=tm_c, name="at_out_ctx")
        elif kind == 2:
            w_in = rt_w_in[j].astype(BF16)
            w_out = rt_w_out[j].astype(BF16)
            cos1, sin1 = _retention_tables(seq)
            ones_t, zeros_t = jnp.ones((nctx, RT_DK), F32), jnp.zeros((nctx, RT_DK), F32)
            pc = _proj(xc, cmul1, csh1, w_in, zero_b(6 * d), seq=nctx, tm=tm_c, name="rt_proj_ctx")
            pL = _proj(xl, mul1, sh1, w_in, zero_b(6 * d), seq=seq, tm=tm_l, name="rt_proj")
            s0 = jnp.zeros((bsz, 2, RT_HEADS, RT_DK, RT_DV), F32)
            oc, s_ctx = _ret_scan(pc, ones_t, zeros_t, rt_decay_exp[j], s0, bsz=bsz, seq=nctx, name="rt_scan_ctx")
            ol, _ = _ret_scan(pL, cos1, sin1, rt_decay_exp[j], s_ctx, bsz=bsz, seq=seq, name="rt_scan")
            xl = _ret_out(ol, pL, w_out, xl, g1, seq=seq, tm=tm_l, name="rt_out")
            if need_ctx:
                xc = _ret_out(oc, pc, w_out, xc, cg1, seq=nctx, tm=tm_c, name="rt_out_ctx")
        else:
            wts = dict(mix=rw_mix[j], w_rkv=rw_w_rkv[j].astype(BF16), w0=rw_w0[j].reshape(2, 1, d),
                       w1=rw_w1[j].astype(BF16), w2=rw_w2[j].astype(BF16), a0=rw_a0[j].reshape(2, 1, d),
                       a1=rw_a1[j].astype(BF16), a2=rw_a2[j].astype(BF16), g1=rw_g1[j].astype(BF16),
                       g2=rw_g2[j].astype(BF16), k_k=rw_k_k[j].reshape(1, d), k_a=rw_k_a[j].reshape(1, d),
                       r_k=rw_r_k[j].reshape(1, d))
            w_out = rw_w_out[j].astype(BF16)
            ln_w, ln_b = rw_ln_w[j].reshape(1, d), rw_ln_b[j].reshape(1, d)
            rc = _rw_proj(xc, cmul1, csh1, wts, seq=nctx, width=nctx, tm=tm_c, quarters=('l', 'l', 'r', 'r'),
                          name="rw_proj_ctx")
            rl = _rw_proj(xl, mul1, sh1, wts, seq=seq, width=GRID_W, tm=tm_f, quarters=('l', 'r', 'u', 'd'),
                          name="rw_proj")
            s0 = jnp.zeros((bsz, 2, RW_NGROUPS, RW_GROUP, RW_GROUP), F32)
            scan_in = lambda t: (t[0], t[2], t[1], t[5], t[6], t[7])
            yc, s_ctx = _rw_scan(*scan_in(rc), s0, bsz=bsz, seq=nctx, name="rw_scan_ctx")
            yl, _ = _rw_scan(*scan_in(rl), s_ctx, bsz=bsz, seq=seq, name="rw_scan")
            xl = _rw_out(yl, rl[4], rl[3], ln_w, ln_b, w_out, xl, g1, seq=seq, tm=tm_l, name="rw_out")
            if need_ctx:
                xc = _rw_out(yc, rc[4], rc[3], ln_w, ln_b, w_out, xc, cg1, seq=nctx, tm=tm_c, name="rw_out_ctx")

        w_up = ffn_w_up[i].astype(BF16)
        w_down = ffn_w_down[i].astype(BF16)
        conv_w = ffn_conv_w[i].reshape(9, D_FF)
        conv_b = ffn_conv_b[i].reshape(1, D_FF)
        fin_g = final_norm_g.reshape(1, d)
        pu = _proj(xl, mul2, sh2, w_up, zero_b(2 * D_FF), seq=seq, tm=tm_l, name="ffn_up")
        xl = _ffn_down(pu, conv_w, conv_b, w_down, xl, g2, fin_g, seq=seq, width=GRID_W, tm=tm_f,
                       final_norm=(i == depth - 1), name="ffn_down")
        if need_ctx:
            pu = _proj(xc, cmul2, csh2, w_up, zero_b(2 * D_FF), seq=nctx, tm=tm_c, name="ffn_up_ctx")
            xc = _ffn_down(pu, conv_w, conv_b, w_down, xc, cg2, fin_g, seq=nctx, width=nctx, tm=tm_c,
                           final_norm=False, name="ffn_down_ctx")
    return xl.reshape(bsz, seq, d)


def kernel(x, c, ctx, c_ctx, ada_w, ada_b, norm1_g, norm2_g, ffn_w_up, ffn_conv_w, ffn_conv_b, ffn_w_down, hg_w_in, hg_lb_logits, hg_norm_g, hg_w_out, at_w_qkv, at_b_qkv, at_sink, at_w_out, rt_w_in, rt_decay_exp, rt_w_out, rw_mix, rw_w_rkv, rw_w0, rw_w1, rw_w2, rw_a0, rw_a1, rw_a2, rw_g1, rw_g2, rw_k_k, rw_k_a, rw_r_k, rw_ln_w, rw_ln_b, rw_w_out, final_norm_g):
    return _forward(x, c, ctx, c_ctx, ada_w, ada_b, norm1_g, norm2_g, ffn_w_up, ffn_conv_w, ffn_conv_b, ffn_w_down,
                    hg_w_in, hg_lb_logits, hg_norm_g, hg_w_out, at_w_qkv, at_b_qkv, at_sink, at_w_out,
                    rt_w_in, rt_decay_exp, rt_w_out, rw_mix, rw_w_rkv, rw_w0, rw_w1, rw_w2, rw_a0, rw_a1, rw_a2,
                    rw_g1, rw_g2, rw_k_k, rw_k_a, rw_r_k, rw_ln_w, rw_ln_b, rw_w_out, final_norm_g, DEPTH)
```

```python
import functools
import math

import numpy as np
import jax
import jax.numpy as jnp
from jax import lax
from jax.experimental import pallas as pl
from jax.experimental.pallas import tpu as pltpu

F32 = jnp.float32
BF16 = jnp.bfloat16

D_MODEL = 1024
DEPTH = 4
GRID_W = 64
NORM_EPS = 1e-6

HG_DK = 128
HG_HEADS = D_MODEL // HG_DK
HG_CHUNK = 128

AT_HD = 64
AT_HEADS = D_MODEL // AT_HD
AT_KV_HEADS = AT_HEADS // 4
AT_GROUP = 4
AT_KV_DIM = AT_KV_HEADS * AT_HD
AT_BLOCK = 128
ROPE_BASE = 10000.0
AXIS_DIM = AT_HD // 2

RT_DK = 256
RT_HEADS = D_MODEL // RT_DK
RT_DV = 2 * RT_DK
RT_V_DIM = RT_HEADS * RT_DV
RT_CHUNK = 256

RW_HEAD = 64
RW_HEADS = D_MODEL // RW_HEAD
RW_GN_EPS = 64e-5
RW_CHUNK = 64
RW_GROUP_HEADS = 4
RW_GROUP = RW_GROUP_HEADS * RW_HEAD
RW_NGROUPS = D_MODEL // RW_GROUP

D_FF = 2816
FF_CHUNK = 256

VMEM_LIMIT = 56 * 1024 * 1024


def _cp(*sem):
    return pltpu.CompilerParams(dimension_semantics=sem, vmem_limit_bytes=VMEM_LIMIT)


def _bdot(a, b):
    return jnp.dot(a.astype(BF16), b.astype(BF16), preferred_element_type=F32)


def _bdot_nt(a, b):
    return lax.dot_general(a.astype(BF16), b.astype(BF16), (((1,), (1,)), ((), ())), preferred_element_type=F32)


def _bdot_tn(a, b):
    return lax.dot_general(a.astype(BF16), b.astype(BF16), (((0,), (0,)), ((), ())), preferred_element_type=F32)


def _split3(x):
    hi = x.astype(BF16)
    r1 = x - hi.astype(F32)
    mid = r1.astype(BF16)
    lo = (r1 - mid.astype(F32)).astype(BF16)
    return hi, mid, lo


def _exact_dot(m_bf16, x):
    hi, mid, lo = _split3(x)
    d = lambda v: jnp.dot(m_bf16, v, preferred_element_type=F32)
    return d(hi) + d(mid) + d(lo)


def _exact_dot2(m_bf16, x):
    hi = x.astype(BF16)
    lo = (x - hi.astype(F32)).astype(BF16)
    return jnp.dot(m_bf16, hi, preferred_element_type=F32) + jnp.dot(m_bf16, lo, preferred_element_type=F32)


def _exact_dot_r(x, m_bf16):
    hi, mid, lo = _split3(x)
    d = lambda v: jnp.dot(v, m_bf16, preferred_element_type=F32)
    return d(hi) + d(mid) + d(lo)


def _sigmoid(x):
    return 1.0 / (1.0 + jnp.exp(-x))


def _silu(x):
    return x * _sigmoid(x)


def _rms(x):
    return x * lax.rsqrt(jnp.mean(x * x, axis=-1, keepdims=True) + NORM_EPS)


def _adaln_kernel(c_ref, w_ref, b_ref, o_ref):
    o_ref[...] = _bdot(_silu(c_ref[...]), w_ref[...]) + b_ref[...]


def _adaln(cvecs, ada_w, ada_b):
    depth, d, n = ada_w.shape
    tn = 1536
    return pl.pallas_call(
        _adaln_kernel,
        out_shape=jax.ShapeDtypeStruct((depth, 8, n), F32),
        grid=(depth, n // tn),
        in_specs=[pl.BlockSpec((8, d), lambda l, j: (0, 0)),
                  pl.BlockSpec((None, d, tn), lambda l, j: (l, 0, j)),
                  pl.BlockSpec((None, 1, tn), lambda l, j: (l, 0, j))],
        out_specs=pl.BlockSpec((None, 8, tn), lambda l, j: (l, 0, j)),
        compiler_params=_cp("parallel", "parallel"),
        name="adaln",
    )(cvecs, ada_w.astype(BF16), ada_b.reshape(depth, 1, n))


def _proj_kernel(x_ref, mul_ref, add_ref, w_ref, b_ref, *out_refs, n_lo, tn):
    h = (_rms(x_ref[...]) * mul_ref[...] + add_ref[...]).astype(BF16)
    for n0 in range(0, w_ref.shape[1], tn):
        y = jnp.dot(h, w_ref[:, n0:n0 + tn], preferred_element_type=F32) + b_ref[:, n0:n0 + tn]
        if n0 < n_lo:
            out_refs[0][:, n0:n0 + tn] = y.astype(BF16)
        else:
            out_refs[1][:, n0 - n_lo:n0 - n_lo + tn] = y


def _proj(x2d, mul, add, w, bias, *, seq, tm, name, n_f32_tail=0):
    m, d = x2d.shape
    n = w.shape[1]
    n_lo = n - n_f32_tail
    tn = 512 if (n % 512 == 0 and n_lo % 512 == 0) else 256
    tpb = seq // tm
    out_shape = [jax.ShapeDtypeStruct((m, n_lo), BF16)]
    out_specs = [pl.BlockSpec((tm, n_lo), lambda i: (i, 0))]
    if n_f32_tail:
        out_shape.append(jax.ShapeDtypeStruct((m, n_f32_tail), F32))
        out_specs.append(pl.BlockSpec((tm, n_f32_tail), lambda i: (i, 0)))
    res = pl.pallas_call(
        functools.partial(_proj_kernel, n_lo=n_lo, tn=tn),
        out_shape=tuple(out_shape),
        grid=(m // tm,),
        in_specs=[pl.BlockSpec((tm, d), lambda i: (i, 0)),
                  pl.BlockSpec((None, 1, d), lambda i: (i // tpb, 0, 0)),
                  pl.BlockSpec((None, 1, d), lambda i: (i // tpb, 0, 0)),
                  pl.BlockSpec((d, n), lambda i: (0, 0)),
                  pl.BlockSpec((1, n), lambda i: (0, 0))],
        out_specs=tuple(out_specs),
        compiler_params=_cp("parallel"),
        name=name,
    )(x2d, mul, add, w, bias)
    return res if n_f32_tail else res[0]


def _mm_res_kernel(a_ref, w_ref, x_ref, g_ref, o_ref):
    o_ref[...] = x_ref[...] + g_ref[...] * _bdot(a_ref[...], w_ref[...])


def _mm_res(act, w, x2d, gate, *, seq, tm, name):
    m, k = act.shape
    d = w.shape[1]
    tpb = seq // tm
    return pl.pallas_call(
        _mm_res_kernel,
        out_shape=jax.ShapeDtypeStruct((m, d), F32),
        grid=(m // tm,),
        in_specs=[pl.BlockSpec((tm, k), lambda i: (i, 0)),
                  pl.BlockSpec((k, d), lambda i: (0, 0)),
                  pl.BlockSpec((tm, d), lambda i: (i, 0)),
                  pl.BlockSpec((None, 1, d), lambda i: (i // tpb, 0, 0))],
        out_specs=pl.BlockSpec((tm, d), lambda i: (i, 0)),
        compiler_params=_cp("parallel"),
        name=name,
    )(act, w, x2d, gate)


def _ffn_down_kernel(u_ref, up_ref, un_ref, v_ref, cw_ref, cb_ref, wd_ref, x_ref, g_ref, fg_ref, o_ref, act_scr,
                     *, tm, width, tpi, final_norm):
    i = pl.program_id(0)
    first = (i % tpi) == 0
    last = (i % tpi) == tpi - 1
    col = lax.broadcasted_iota(jnp.int32, (tm, 1), 0) & (width - 1)
    not_left = col != 0
    not_right = col != width - 1
    inv_sqrt2 = 1.0 / math.sqrt(2.0)
    for c0 in range(0, D_FF, FF_CHUNK):
        sl = slice(c0, c0 + FF_CHUNK)
        main = u_ref[:, sl].astype(F32)
        prev = jnp.where(first, 0.0, up_ref[:, sl].astype(F32))
        nxt = jnp.where(last, 0.0, un_ref[:, sl].astype(F32))
        if tm > width:
            above = jnp.concatenate([prev, main[: tm - width]], axis=0)
            below = jnp.concatenate([main[width:], nxt], axis=0)
        else:
            above, below = prev, nxt
        colsum = [cw_ref[b:b + 1, sl] * above + cw_ref[3 + b:4 + b, sl] * main + cw_ref[6 + b:7 + b, sl] * below
                  for b in range(3)]
        acc = (colsum[1] + cb_ref[:, sl]
               + jnp.where(not_left, pltpu.roll(colsum[0], 1, 0), 0.0)
               + jnp.where(not_right, pltpu.roll(colsum[2], tm - 1, 0), 0.0))
        gelu = 0.5 * acc * (1.0 + lax.erf(acc * inv_sqrt2))
        act_scr[:, sl] = (gelu * v_ref[:, sl].astype(F32)).astype(BF16)
    y = x_ref[...] + g_ref[...] * jnp.dot(act_scr[...], wd_ref[...], preferred_element_type=F32)
    if final_norm:
        y = _rms(y) * fg_ref[...]
    o_ref[...] = y


def _ffn_down(p_up, conv_w, conv_b, w_down, x2d, gate, final_g, *, seq, width, tm, final_norm, name):
    m = x2d.shape[0]
    tpi = seq // tm
    rpt = tm // width
    nrow = m // width
    kern = functools.partial(_ffn_down_kernel, tm=tm, width=width, tpi=tpi, final_norm=final_norm)
    return pl.pallas_call(
        kern,
        out_shape=jax.ShapeDtypeStruct((m, D_MODEL), F32),
        grid=(m // tm,),
        in_specs=[pl.BlockSpec((tm, D_FF), lambda i: (i, 0)),
                  pl.BlockSpec((width, D_FF), lambda i: (jnp.maximum(i * rpt - 1, 0), 0)),
                  pl.BlockSpec((width, D_FF), lambda i: (jnp.minimum((i + 1) * rpt, nrow - 1), 0)),
                  pl.BlockSpec((tm, D_FF), lambda i: (i, 1)),
                  pl.BlockSpec((9, D_FF), lambda i: (0, 0)),
                  pl.BlockSpec((1, D_FF), lambda i: (0, 0)),
                  pl.BlockSpec((D_FF, D_MODEL), lambda i: (0, 0)),
                  pl.BlockSpec((tm, D_MODEL), lambda i: (i, 0)),
                  pl.BlockSpec((None, 1, D_MODEL), lambda i: (i // tpi, 0, 0)),
                  pl.BlockSpec((1, D_MODEL), lambda i: (0, 0))],
        out_specs=pl.BlockSpec((tm, D_MODEL), lambda i: (i, 0)),
        scratch_shapes=[pltpu.VMEM((tm, D_FF), BF16)],
        compiler_params=_cp("parallel"),
        name=name,
    )(p_up, p_up, p_up, p_up, conv_w, conv_b, w_down, x2d, gate, final_g)


def _hgrn_tables(c):
    levels = []
    m = c // 2
    while m >= 1:
        levels.append(m)
        m //= 2
    nl = len(levels)
    e = np.zeros((nl + 2, c, c), np.float32)
    msk = np.zeros((nl + 1, c, c), np.float32)
    t = np.arange(c)
    for l, m in enumerate(levels):
        blk = t // (2 * m)
        second = (t % (2 * m)) >= m
        ref = 2 * m * blk + m - 1
        for tt in range(c):
            if second[tt]:
                e[l, tt, ref[tt] + 1:tt + 1] = 1.0
            else:
                e[l, tt, tt + 1:ref[tt] + 1] = 1.0
        msk[l] = (blk[:, None] == blk[None, :]) & second[:, None] & (~second)[None, :]
    msk[nl] = np.eye(c)
    e[nl] = np.tril(np.ones((c, c)))
    e[nl + 1] = (t[None, :] > t[:, None])
    e1 = e[:, ::-1, ::-1]
    m1 = msk[:, ::-1, ::-1]
    e_all = np.stack([e.reshape(-1, c), e1.reshape(-1, c)])
    m_all = np.stack([msk, m1])
    return jnp.asarray(e_all, BF16), jnp.asarray(m_all, F32), nl


def _hgrn_scan_kernel(q_ref, v_ref, z_ref, lb_ref, e_ref, m_ref, s0_ref, o_ref, sf_ref, st_scr, *, c, nl):
    j = pl.program_id(2)

    @pl.when(j == 0)
    def _():
        st_scr[...] = s0_ref[...]

    lb = lb_ref[...]
    z = z_ref[...]
    logf = jnp.log(lb + (1.0 - lb) * _sigmoid(z))
    kin = (1.0 - lb) * _sigmoid(-z)
    ex = jnp.exp(_exact_dot2(e_ref[...], logf))
    dec_tot = jnp.exp(jnp.sum(logf, axis=0, keepdims=True))
    q = q_ref[...].astype(F32)
    v = v_ref[...]
    hs = range(HG_HEADS)
    sls = [slice(h * HG_DK, (h + 1) * HG_DK) for h in hs]
    a = [m_ref[nl] * _bdot_nt(q[:, s], kin[:, s]) for s in sls]
    for l in range(nl):
        f = ex[l * c:(l + 1) * c]
        qf = (q * f).astype(BF16)
        kf = (kin * f).astype(BF16)
        a = [a[h] + m_ref[l] * _bdot_nt(qf[:, sls[h]], kf[:, sls[h]]) for h in hs]
    qtop = (q * ex[nl * c:(nl + 1) * c]).astype(BF16)
    kend = (kin * ex[(nl + 1) * c:(nl + 2) * c]).astype(BF16)
    for h in hs:
        sl = sls[h]
        st = st_scr[h]
        o_ref[:, sl] = (_bdot_nt(qtop[:, sl], st) + _bdot(a[h], v[:, sl])).astype(o_ref.dtype)
        st_scr[h] = st * dec_tot[:, sl] + _bdot_tn(v[:, sl], kend[:, sl])

    @pl.when(j == pl.num_programs(2) - 1)
    def _():
        sf_ref[...] = st_scr[...]


def _hgrn_scan(p, zf, lb_row, s0, *, bsz, seq, name):
    c = min(HG_CHUNK, seq)
    nc = seq // c
    e_all, m_all, nl = _hgrn_tables(c)
    row = lambda b, d, j: b * nc + j + d * (nc - 1 - 2 * j)
    kern = functools.partial(_hgrn_scan_kernel, c=c, nl=nl)
    st_shape = (HG_HEADS, HG_DK, HG_DK)
    return pl.pallas_call(
        kern,
        out_shape=(jax.ShapeDtypeStruct((2, bsz * seq, D_MODEL), BF16),
                   jax.ShapeDtypeStruct((bsz, 2) + st_shape, F32)),
        grid=(bsz, 2, nc),
        in_specs=[pl.BlockSpec((c, D_MODEL), lambda b, d, j: (row(b, d, j), 0)),
                  pl.BlockSpec((c, D_MODEL), lambda b, d, j: (row(b, d, j), 1)),
                  pl.BlockSpec((c, D_MODEL), lambda b, d, j: (row(b, d, j), d)),
                  pl.BlockSpec((1, D_MODEL), lambda b, d, j: (0, 0)),
                  pl.BlockSpec((None,) + e_all.shape[1:], lambda b, d, j: (d, 0, 0)),
                  pl.BlockSpec((None,) + m_all.shape[1:], lambda b, d, j: (d, 0, 0, 0)),
                  pl.BlockSpec((None, None) + st_shape, lambda b, d, j: (b, d, 0, 0, 0))],
        out_specs=(pl.BlockSpec((None, c, D_MODEL), lambda b, d, j: (d, row(b, d, j), 0)),
                   pl.BlockSpec((None, None) + st_shape, lambda b, d, j: (b, d, 0, 0, 0))),
        scratch_shapes=[pltpu.VMEM(st_shape, F32)],
        compiler_params=_cp("parallel", "parallel", "arbitrary"),
        name=name,
    )(p, p, zf, lb_row, e_all, m_all, s0)


def _hgrn_out_kernel(o0_ref, o1_ref, gate_ref, ng_ref, w_ref, x_ref, g_ref, out_ref, act_scr):
    o = o0_ref[...].astype(F32) + o1_ref[...].astype(F32)
    gate = gate_ref[...].astype(F32)
    ng = ng_ref[...]
    for h in range(HG_HEADS):
        sl = slice(h * HG_DK, (h + 1) * HG_DK)
        act_scr[:, sl] = (_rms(o[:, sl]) * ng[:, sl] * _silu(gate[:, sl])).astype(BF16)
    out_ref[...] = x_ref[...] + g_ref[...] * jnp.dot(act_scr[...], w_ref[...], preferred_element_type=F32)


def _hgrn_out(o, p, norm_row, w_out, x2d, gate, *, seq, tm, name):
    m = x2d.shape[0]
    tpb = seq // tm
    return pl.pallas_call(
        _hgrn_out_kernel,
        out_shape=jax.ShapeDtypeStruct((m, D_MODEL), F32),
        grid=(m // tm,),
        in_specs=[pl.BlockSpec((None, tm, D_MODEL), lambda i: (0, i, 0)),
                  pl.BlockSpec((None, tm, D_MODEL), lambda i: (1, i, 0)),
                  pl.BlockSpec((tm, D_MODEL), lambda i: (i, 2)),
                  pl.BlockSpec((1, D_MODEL), lambda i: (0, 0)),
                  pl.BlockSpec((D_MODEL, D_MODEL), lambda i: (0, 0)),
                  pl.BlockSpec((tm, D_MODEL), lambda i: (i, 0)),
                  pl.BlockSpec((None, 1, D_MODEL), lambda i: (i // tpb, 0, 0))],
        out_specs=pl.BlockSpec((tm, D_MODEL), lambda i: (i, 0)),
        scratch_shapes=[pltpu.VMEM((tm, D_MODEL), BF16)],
        compiler_params=_cp("parallel"),
        name=name,
    )(o, o, p, norm_row, w_out, x2d, gate)


def _rope_kernel(q_ref, qr_ref, k_ref, kr_ref, cos_ref, sin_ref, qo_ref, ko_ref):
    cos = cos_ref[...]
    sin = sin_ref[...]
    f32 = lambda ref: ref[...].astype(F32)
    qo_ref[...] = ((f32(q_ref) * cos + f32(qr_ref) * sin) * (AT_HD ** -0.5)).astype(BF16)
    ko_ref[...] = (f32(k_ref) * cos[:, :AT_KV_DIM] + f32(kr_ref) * sin[:, :AT_KV_DIM]).astype(BF16)


def _rope(p, cos_t, sin_t, *, seq, tm):
    m = p.shape[0]
    tpb = seq // tm
    return pl.pallas_call(
        _rope_kernel,
        out_shape=(jax.ShapeDtypeStruct((m, D_MODEL), BF16), jax.ShapeDtypeStruct((m, AT_KV_DIM), BF16)),
        grid=(m // tm,),
        in_specs=[pl.BlockSpec((tm, D_MODEL), lambda i: (i, 0)),
                  pl.BlockSpec((tm, D_MODEL), lambda i: (i, 1)),
                  pl.BlockSpec((tm, AT_KV_DIM), lambda i: (i, 8)),
                  pl.BlockSpec((tm, AT_KV_DIM), lambda i: (i, 10)),
                  pl.BlockSpec((tm, D_MODEL), lambda i: (i % tpb, 0)),
                  pl.BlockSpec((tm, D_MODEL), lambda i: (i % tpb, 0))],
        out_specs=(pl.BlockSpec((tm, D_MODEL), lambda i: (i, 0)),
                   pl.BlockSpec((tm, AT_KV_DIM), lambda i: (i, 0))),
        compiler_params=_cp("parallel"),
        name="at_rope",
    )(p, p, p, p, cos_t, sin_t)


def _sink_attend(q, keys, vals, mask, sink_ref, o_ref):
    t = q.shape[0]
    gs = range(AT_KV_HEADS)
    hsl = lambda hd: slice(hd * AT_HD, (hd + 1) * AT_HD)
    q4 = [jnp.concatenate([q[:, hsl(g * AT_GROUP + r)] for r in range(AT_GROUP)], axis=0) for g in gs]
    logits = [_bdot_nt(q4[g], keys[:, hsl(g)]) for g in gs]
    if mask is not None:
        pen = jnp.where(mask, 0.0, -jnp.inf)
        pen4 = jnp.concatenate([pen] * AT_GROUP, axis=0)
        logits = [lg + pen4 for lg in logits]
    s = [jnp.concatenate([jnp.full((t, 1), sink_ref[g * AT_GROUP + r], F32) for r in range(AT_GROUP)], axis=0)
         for g in gs]
    mx = [jnp.maximum(jnp.max(logits[g], axis=-1, keepdims=True), s[g]) for g in gs]
    p = [jnp.exp(logits[g] - mx[g]) for g in gs]
    denom = [jnp.sum(p[g], axis=-1, keepdims=True) + jnp.exp(s[g] - mx[g]) for g in gs]
    o4 = [_bdot(p[g], vals[:, hsl(g)]) / denom[g] for g in gs]
    for g in gs:
        for r in range(AT_GROUP):
            o_ref[:, hsl(g * AT_GROUP + r)] = o4[g][r * t:(r + 1) * t].astype(o_ref.dtype)


def _attn_kernel(sink_ref, q_ref, kp_ref, kc_ref, kn_ref, vp_ref, vc_ref, vn_ref, ck_ref, cv_ref, o_ref, *, nctx):
    i = pl.program_id(1)
    nb = pl.num_programs(1)
    keys = jnp.concatenate([ck_ref[...], kp_ref[...], kc_ref[...], kn_ref[...]], axis=0)
    vals = jnp.concatenate([cv_ref[...], vp_ref[...], vc_ref[...], vn_ref[...]], axis=0)
    nk = nctx + 3 * AT_BLOCK
    ti = lax.broadcasted_iota(jnp.int32, (AT_BLOCK, nk), 0)
    kj = lax.broadcasted_iota(jnp.int32, (AT_BLOCK, nk), 1) - nctx
    lo = jnp.where(i > 0, 0, AT_BLOCK)
    hi = jnp.where(i < nb - 1, 3 * AT_BLOCK, 2 * AT_BLOCK)
    rel = kj - ti
    mask = (kj < 0) | ((rel >= 0) & (rel <= 2 * AT_BLOCK) & (kj >= lo) & (kj < hi))
    _sink_attend(q_ref[...], keys, vals, mask, sink_ref, o_ref)


def _attn(sink, q_r, k_r, p, p_ctx, *, bsz, seq, nctx):
    nb = seq // AT_BLOCK
    blk = lambda b, i: b * nb + i
    prv = lambda b, i: b * nb + jnp.maximum(i - 1, 0)
    nxt = lambda b, i: b * nb + jnp.minimum(i + 1, nb - 1)
    kspec = lambda f: pl.BlockSpec((AT_BLOCK, AT_KV_DIM), lambda b, i: (f(b, i), 0))
    vspec = lambda f: pl.BlockSpec((AT_BLOCK, AT_KV_DIM), lambda b, i: (f(b, i), 9))
    return pl.pallas_call(
        functools.partial(_attn_kernel, nctx=nctx),
        out_shape=jax.ShapeDtypeStruct((bsz * seq, D_MODEL), BF16),
        grid=(bsz, nb),
        in_specs=[pl.BlockSpec(memory_space=pltpu.SMEM),
                  pl.BlockSpec((AT_BLOCK, D_MODEL), lambda b, i: (blk(b, i), 0)),
                  kspec(prv), kspec(blk), kspec(nxt), vspec(prv), vspec(blk), vspec(nxt),
                  pl.BlockSpec((nctx, AT_KV_DIM), lambda b, i: (b, 8)),
                  pl.BlockSpec((nctx, AT_KV_DIM), lambda b, i: (b, 9))],
        out_specs=pl.BlockSpec((AT_BLOCK, D_MODEL), lambda b, i: (blk(b, i), 0)),
        compiler_params=_cp("parallel", "parallel"),
        name="at_window",
    )(sink, q_r, k_r, k_r, k_r, p, p, p, p_ctx, p_ctx)


def _attn_ctx_kernel(sink_ref, q_ref, k_ref, v_ref, o_ref):
    _sink_attend(q_ref[...] * (AT_HD ** -0.5), k_ref[...], v_ref[...], None, sink_ref, o_ref)


def _attn_ctx(sink, p_ctx, *, bsz, nctx):
    return pl.pallas_call(
        _attn_ctx_kernel,
        out_shape=jax.ShapeDtypeStruct((bsz * nctx, D_MODEL), BF16),
        grid=(bsz,),
        in_specs=[pl.BlockSpec(memory_space=pltpu.SMEM),
                  pl.BlockSpec((nctx, D_MODEL), lambda b: (b, 0)),
                  pl.BlockSpec((nctx, AT_KV_DIM), lambda b: (b, 8)),
                  pl.BlockSpec((nctx, AT_KV_DIM), lambda b: (b, 9))],
        out_specs=pl.BlockSpec((nctx, D_MODEL), lambda b: (b, 0)),
        compiler_params=_cp("parallel"),
        name="at_ctx",
    )(sink, p_ctx, p_ctx, p_ctx)


def _ret_scan_kernel(q_ref, k_ref, v_ref, cos_ref, sin_ref, dm_ref, qin_ref, kout_ref, car_ref, s0_ref,
                     o_ref, sf_ref, st_scr):
    j = pl.program_id(2)

    @pl.when(j == 0)
    def _():
        st_scr[...] = s0_ref[...]

    cos = cos_ref[...]
    sin = sin_ref[...]
    half = RT_DK // 2

    def rope(x):
        rot = jnp.concatenate([-x[:, half:], x[:, :half]], axis=1)
        return x * cos + rot * sin

    for h in range(RT_HEADS):
        sk = slice(h * RT_DK, (h + 1) * RT_DK)
        sv = slice(h * RT_DV, (h + 1) * RT_DV)
        qh = rope(q_ref[:, sk].astype(F32))
        kh = rope(k_ref[:, sk].astype(F32) * (RT_DK ** -0.5))
        vh = v_ref[:, sv]
        s = _bdot_nt(qh, kh) * dm_ref[h]
        st = st_scr[h]
        o_ref[:, sv] = (_bdot(qh * qin_ref[:, sk], st) + _bdot(s, vh)).astype(o_ref.dtype)
        st_scr[h] = car_ref[:, sv] * st + _bdot_tn(kh * kout_ref[:, sk], vh)

    @pl.when(j == pl.num_programs(2) - 1)
    def _():
        sf_ref[...] = st_scr[...]


def _ret_tables(decay_exp, c):
    lg = jnp.log1p(-jnp.exp2(decay_exp.astype(F32)))
    idx = jnp.arange(c, dtype=F32)
    pos = jnp.stack([idx, c - 1.0 - idx])
    rel = pos[:, :, None] - pos[:, None, :]
    lgh = lg[:, :, None, None]
    dm = jnp.where(rel[:, None] >= 0, jnp.exp(lgh * jnp.maximum(rel[:, None], 0.0)), 0.0)
    qin = jnp.exp(lg[:, None, :] * (pos[:, :, None] + 1.0))
    kout = jnp.exp(lg[:, None, :] * (c - 1.0 - pos[:, :, None]))
    car = jnp.exp(lg * c)
    qin = jnp.repeat(qin, RT_DK, axis=-1)
    kout = jnp.repeat(kout, RT_DK, axis=-1)
    car = jnp.repeat(car, RT_DV, axis=-1)[:, None, :]
    return dm, qin, kout, car


def _ret_scan(p, cos_t, sin_t, decay_exp, s0, *, bsz, seq, name):
    c = min(RT_CHUNK, seq)
    nc = seq // c
    dm, qin, kout, car = _ret_tables(decay_exp, c)
    crow = lambda b, d, j: j + d * (nc - 1 - 2 * j)
    row = lambda b, d, j: b * nc + crow(b, d, j)
    st_shape = (RT_HEADS, RT_DK, RT_DV)
    return pl.pallas_call(
        _ret_scan_kernel,
        out_shape=(jax.ShapeDtypeStruct((2, bsz * seq, RT_V_DIM), BF16),
                   jax.ShapeDtypeStruct((bsz, 2) + st_shape, F32)),
        grid=(bsz, 2, nc),
        in_specs=[pl.BlockSpec((c, D_MODEL), lambda b, d, j: (row(b, d, j), 0)),
                  pl.BlockSpec((c, D_MODEL), lambda b, d, j: (row(b, d, j), 1)),
                  pl.BlockSpec((c, RT_V_DIM), lambda b, d, j: (row(b, d, j), 1)),
                  pl.BlockSpec((c, RT_DK), lambda b, d, j: (crow(b, d, j), 0)),
                  pl.BlockSpec((c, RT_DK), lambda b, d, j: (crow(b, d, j), 0)),
                  pl.BlockSpec((None, RT_HEADS, c, c), lambda b, d, j: (d, 0, 0, 0)),
                  pl.BlockSpec((None, c, D_MODEL), lambda b, d, j: (d, 0, 0)),
                  pl.BlockSpec((None, c, D_MODEL), lambda b, d, j: (d, 0, 0)),
                  pl.BlockSpec((None, 1, RT_V_DIM), lambda b, d, j: (d, 0, 0)),
                  pl.BlockSpec((None, None) + st_shape, lambda b, d, j: (b, d, 0, 0, 0))],
        out_specs=(pl.BlockSpec((None, c, RT_V_DIM), lambda b, d, j: (d, row(b, d, j), 0)),
                   pl.BlockSpec((None, None) + st_shape, lambda b, d, j: (b, d, 0, 0, 0))),
        scratch_shapes=[pltpu.VMEM(st_shape, F32)],
        compiler_params=_cp("parallel", "parallel", "arbitrary"),
        name=name,
    )(p, p, p, cos_t, sin_t, dm, qin, kout, car, s0)


def _ret_out_kernel(o0_ref, o1_ref, gate_ref, w_ref, x_ref, g_ref, out_ref, act_scr):
    for h in range(RT_HEADS):
        sv = slice(h * RT_DV, (h + 1) * RT_DV)
        o = o0_ref[:, sv].astype(F32) + o1_ref[:, sv].astype(F32)
        act_scr[:, sv] = (_silu(gate_ref[:, sv].astype(F32)) * _rms(o)).astype(BF16)
    out_ref[...] = x_ref[...] + g_ref[...] * jnp.dot(act_scr[...], w_ref[...], preferred_element_type=F32)


def _ret_out(o, p, w_out, x2d, gate, *, seq, tm, name):
    m = x2d.shape[0]
    tpb = seq // tm
    return pl.pallas_call(
        _ret_out_kernel,
        out_shape=jax.ShapeDtypeStruct((m, D_MODEL), F32),
        grid=(m // tm,),
        in_specs=[pl.BlockSpec((None, tm, RT_V_DIM), lambda i: (0, i, 0)),
                  pl.BlockSpec((None, tm, RT_V_DIM), lambda i: (1, i, 0)),
                  pl.BlockSpec((tm, RT_V_DIM), lambda i: (i, 2)),
                  pl.BlockSpec((RT_V_DIM, D_MODEL), lambda i: (0, 0)),
                  pl.BlockSpec((tm, D_MODEL), lambda i: (i, 0)),
                  pl.BlockSpec((None, 1, D_MODEL), lambda i: (i // tpb, 0, 0))],
        out_specs=pl.BlockSpec((tm, D_MODEL), lambda i: (i, 0)),
        scratch_shapes=[pltpu.VMEM((tm, RT_V_DIM), BF16)],
        compiler_params=_cp("parallel"),
        name=name,
    )(o, o, p, w_out, x2d, gate)


def _seg_sum(x, e_down, e_up):
    hi = x.astype(BF16)
    lo = (x - hi.astype(F32)).astype(BF16)
    s = jnp.dot(hi, e_down, preferred_element_type=F32) + jnp.dot(lo, e_down, preferred_element_type=F32)
    return _exact_dot_r(s, e_up)


def _rw_proj_kernel(x_ref, xp_ref, xn_ref, mul_ref, add_ref, mix_ref, wrkv_ref, w0_ref, w1_ref, w2_ref,
                    a0_ref, a1_ref, a2_ref, g1_ref, g2_ref, kk_ref, ka_ref, rk_ref, ed_ref, eu_ref,
                    r_out, v_out, kn_out, gate_out, bonus_out, lw_out, kd_out, bb_out,
                    *, tm, width, tpi, quarters):
    i = pl.program_id(0)
    first = (i % tpi) == 0
    last = (i % tpi) == tpi - 1
    mul = mul_ref[...]
    add = add_ref[...]
    hmod = lambda x: _rms(x) * mul + add
    hm = hmod(x_ref[...])
    col = lax.broadcasted_iota(jnp.int32, (tm, 1), 0) & (width - 1)
    srcs = {}
    if 'l' in quarters:
        srcs['l'] = jnp.where(col != 0, pltpu.roll(hm, 1, 0), 0.0)
    if 'r' in quarters:
        srcs['r'] = jnp.where(col != width - 1, pltpu.roll(hm, tm - 1, 0), 0.0)
    if 'u' in quarters:
        hp = jnp.where(first, 0.0, hmod(xp_ref[...]))
        srcs['u'] = jnp.concatenate([hp, hm[: tm - width]], axis=0) if tm > width else hp
    if 'd' in quarters:
        hn = jnp.where(last, 0.0, hmod(xn_ref[...]))
        srcs['d'] = jnp.concatenate([hm[width:], hn], axis=0) if tm > width else hn
    qd = D_MODEL // 4
    shifted = jnp.concatenate([srcs[q][:, n * qd:(n + 1) * qd] for n, q in enumerate(quarters)], axis=1)
    xx = shifted - hm
    mixed = lambda n: hm + xx * mix_ref[n:n + 1, :]
    r = _bdot(mixed(0), wrkv_ref[0])
    k = _bdot(mixed(2), wrkv_ref[1])
    v = _bdot(mixed(3), wrkv_ref[2])
    xw = mixed(1)
    xa = mixed(4)
    ed = ed_ref[...]
    eu = eu_ref[...]
    kkh = k * kk_ref[...]
    nrm = jnp.sqrt(_seg_sum(kkh * kkh, ed, eu))
    kn = kkh / jnp.maximum(nrm, 1e-12)
    ksum = jnp.zeros_like(k)
    for z in range(2):
        w_raw = w0_ref[z] + _bdot(jnp.tanh(_bdot(xw, w1_ref[z])), w2_ref[z])
        t = -w_raw
        softplus = jnp.maximum(t, 0.0) + jnp.log1p(jnp.exp(-jnp.abs(t)))
        lw_out[z] = -jnp.exp(-softplus - 0.5)
        a = _sigmoid(a0_ref[z] + _bdot(_bdot(xa, a1_ref[z]), a2_ref[z]))
        kd = k * (1.0 + (a - 1.0) * ka_ref[...])
        kd_out[z] = kd.astype(BF16)
        bb_out[z] = (kn * a).astype(BF16)
        ksum = ksum + kd
    r_out[...] = r.astype(BF16)
    v_out[...] = v.astype(BF16)
    kn_out[...] = kn.astype(BF16)
    gate_out[...] = _bdot(_sigmoid(_bdot(mixed(5), g1_ref[...])), g2_ref[...]).astype(BF16)
    bonus_out[...] = (_seg_sum(r * ksum * rk_ref[...], ed, eu) * v).astype(BF16)


def _seg_mats():
    hid = np.arange(D_MODEL) // RW_HEAD
    e_down = (hid[:, None] == np.arange(128)[None, :]).astype(np.float32)
    return jnp.asarray(e_down, BF16), jnp.asarray(e_down.T, BF16)


def _rw_proj(x2d, mul, add, wts, *, seq, width, tm, quarters, name):
    m = x2d.shape[0]
    tpi = seq // tm
    rpt = tm // width
    nrow = m // width
    e_down, e_up = _seg_mats()
    full = lambda a: pl.BlockSpec(a.shape, lambda i: (0,) * a.ndim)
    consts = [wts['mix'], wts['w_rkv'], wts['w0'], wts['w1'], wts['w2'], wts['a0'], wts['a1'], wts['a2'],
              wts['g1'], wts['g2'], wts['k_k'], wts['k_a'], wts['r_k'], e_down, e_up]
    kern = functools.partial(_rw_proj_kernel, tm=tm, width=width, tpi=tpi, quarters=quarters)
    one = jax.ShapeDtypeStruct((m, D_MODEL), BF16)
    two = jax.ShapeDtypeStruct((2, m, D_MODEL), BF16)
    two_f32 = jax.ShapeDtypeStruct((2, m, D_MODEL), F32)
    ospec1 = pl.BlockSpec((tm, D_MODEL), lambda i: (i, 0))
    ospec2 = pl.BlockSpec((2, tm, D_MODEL), lambda i: (0, i, 0))
    return pl.pallas_call(
        kern,
        out_shape=(one, one, one, one, one, two_f32, two, two),
        grid=(m // tm,),
        in_specs=[pl.BlockSpec((tm, D_MODEL), lambda i: (i, 0)),
                  pl.BlockSpec((width, D_MODEL), lambda i: (jnp.maximum(i * rpt - 1, 0), 0)),
                  pl.BlockSpec((width, D_MODEL), lambda i: (jnp.minimum((i + 1) * rpt, nrow - 1), 0)),
                  pl.BlockSpec((None, 1, D_MODEL), lambda i: (i // tpi, 0, 0)),
                  pl.BlockSpec((None, 1, D_MODEL), lambda i: (i // tpi, 0, 0))] + [full(a) for a in consts],
        out_specs=(ospec1, ospec1, ospec1, ospec1, ospec1, ospec2, ospec2, ospec2),
        compiler_params=_cp("parallel"),
        name=name,
    )(x2d, x2d, x2d, mul, add, *consts)


def _rw_tables(c):
    r = c * RW_GROUP_HEADS
    t = np.arange(r) % c
    hrow = np.arange(r) // c
    same = hrow[:, None] == hrow[None, :]
    strict0 = same & (t[None, :] < t[:, None])
    incl0 = same & (t[None, :] <= t[:, None])
    strict1 = same & (t[None, :] > t[:, None])
    incl1 = same & (t[None, :] >= t[:, None])
    tri0 = np.tril(np.ones((c, c)))
    tri = np.stack([tri0, tri0.T])
    lane_h = np.arange(RW_GROUP) // RW_HEAD
    hmask = hrow[:, None] == lane_h[None, :]
    bdm = lane_h[:, None] == lane_h[None, :]
    ms = np.stack([strict0, strict1]).astype(np.float32)
    mi = np.stack([incl0, incl1]).astype(np.float32)
    return (jnp.asarray(tri, BF16), jnp.asarray(ms), jnp.asarray(mi), jnp.asarray(hmask.astype(np.float32), BF16),
            jnp.asarray(bdm.astype(np.float32)))


def _rw_scan_kernel(r_ref, kn_ref, v_ref, lw_ref, kd_ref, bb_ref, tri_ref, ms_ref, mi_ref, hm_ref, bdm_ref, s0_ref,
                    y_ref, sf_ref, st_scr, *, c):
    j = pl.program_id(2)

    @pl.when(j == 0)
    def _():
        st_scr[...] = s0_ref[...]

    lw = lw_ref[...]
    g = _exact_dot(tri_ref[...], lw)
    gtot = jnp.sum(lw, axis=0, keepdims=True)
    e_incl = jnp.exp(g)
    e_excl = jnp.exp(g - lw)
    e_inv = jnp.exp(-g)
    e_end = jnp.exp(gtot - g)
    dec_tot = jnp.exp(gtot)
    kt = kn_ref[...].astype(F32) * e_excl
    rt = r_ref[...].astype(F32) * e_incl
    kd = kd_ref[...].astype(F32)
    bb = bb_ref[...].astype(F32)
    kh = kd * e_inv
    bh = bb * e_inv
    ke = kd * e_end
    be = bb * e_end
    v = v_ref[...]
    hm = hm_ref[...]
    ms = ms_ref[...]
    mi = mi_ref[...]
    bdm = bdm_ref[...]
    nrow = c * RW_GROUP_HEADS
    eye = (lax.broadcasted_iota(jnp.int32, (nrow, nrow), 0) == lax.broadcasted_iota(jnp.int32, (nrow, nrow), 1)).astype(F32)
    tile = lambda x: jnp.concatenate([x] * RW_GROUP_HEADS, axis=0)
    unstack = lambda x: functools.reduce(lambda a, b: a + b, [x[n * c:(n + 1) * c] for n in range(RW_GROUP_HEADS)])
    nsq = int(math.log2(c)) - 1
    b16 = lambda x: x.astype(BF16)
    dot = lambda a, b: jnp.dot(a, b, preferred_element_type=F32)
    dot_nt = lambda a, b: lax.dot_general(a, b, (((1,), (1,)), ((), ())), preferred_element_type=F32)
    gs = range(RW_NGROUPS)
    sls = [slice(gi * RW_GROUP, (gi + 1) * RW_GROUP) for gi in gs]
    xk = [tile(b16(kt[:, s])) * hm for s in sls]
    xr = [tile(b16(rt[:, s])) * hm for s in sls]
    yb = [tile(b16(bh[:, s])) for s in sls]
    yk = [tile(b16(kh[:, s])) for s in sls]
    vm = [tile(b16(v[:, s])) * hm for s in sls]
    n = [-(dot_nt(xk[gi], yb[gi]) * ms) for gi in gs]
    l_k = [b16(dot_nt(xk[gi], yk[gi]) * ms) for gi in gs]
    m_k = [b16(dot_nt(xr[gi], yk[gi]) * mi) for gi in gs]
    m_b = [b16(dot_nt(xr[gi], yb[gi]) * mi) for gi in gs]
    p = [eye + n[gi] for gi in gs]
    nb = [b16(n[gi]) for gi in gs]
    for _ in range(nsq):
        n = [dot(nb[gi], nb[gi]) for gi in gs]
        nb = [b16(n[gi]) for gi in gs]
        p = [p[gi] + dot(b16(p[gi]), nb[gi]) for gi in gs]
    st = [st_scr[gi] for gi in gs]
    stb = [b16(st[gi]) for gi in gs]
    rhs = [dot_nt(xk[gi], stb[gi]) + dot(l_k[gi], vm[gi]) for gi in gs]
    u = [dot(b16(p[gi]), b16(rhs[gi])) for gi in gs]
    ym = [dot_nt(xr[gi], stb[gi]) + dot(m_k[gi], vm[gi]) - dot(m_b[gi], b16(u[gi])) for gi in gs]
    for gi in gs:
        sl = sls[gi]
        y_ref[:, sl] = unstack(ym[gi]).astype(y_ref.dtype)
        lhs = jnp.concatenate([v[:, sl], b16(-unstack(u[gi]))], axis=0)
        rhs_s = jnp.concatenate([ke[:, sl], be[:, sl]], axis=0)
        st_scr[gi] = st[gi] * dec_tot[:, sl] + bdm * _bdot_tn(lhs, rhs_s)

    @pl.when(j == pl.num_programs(2) - 1)
    def _():
        sf_ref[...] = st_scr[...]


def _rw_scan(r, kn, v, lw, kd, bb, s0, *, bsz, seq, name):
    c = min(RW_CHUNK, seq)
    nc = seq // c
    tri, ms, mi, hmask, bdm = _rw_tables(c)
    row = lambda b, d, j: b * nc + j + d * (nc - 1 - 2 * j)
    st_shape = (RW_NGROUPS, RW_GROUP, RW_GROUP)
    shared = pl.BlockSpec((c, D_MODEL), lambda b, d, j: (row(b, d, j), 0))
    perdir = pl.BlockSpec((None, c, D_MODEL), lambda b, d, j: (d, row(b, d, j), 0))
    nrow = c * RW_GROUP_HEADS
    return pl.pallas_call(
        functools.partial(_rw_scan_kernel, c=c),
        out_shape=(jax.ShapeDtypeStruct((2, bsz * seq, D_MODEL), BF16),
                   jax.ShapeDtypeStruct((bsz, 2) + st_shape, F32)),
        grid=(bsz, 2, nc),
        in_specs=[shared, shared, shared, perdir, perdir, perdir,
                  pl.BlockSpec((None, c, c), lambda b, d, j: (d, 0, 0)),
                  pl.BlockSpec((None, nrow, nrow), lambda b, d, j: (d, 0, 0)),
                  pl.BlockSpec((None, nrow, nrow), lambda b, d, j: (d, 0, 0)),
                  pl.BlockSpec((nrow, RW_GROUP), lambda b, d, j: (0, 0)),
                  pl.BlockSpec((RW_GROUP, RW_GROUP), lambda b, d, j: (0, 0)),
                  pl.BlockSpec((None, None) + st_shape, lambda b, d, j: (b, d, 0, 0, 0))],
        out_specs=(perdir, pl.BlockSpec((None, None) + st_shape, lambda b, d, j: (b, d, 0, 0, 0))),
        scratch_shapes=[pltpu.VMEM(st_shape, F32)],
        compiler_params=_cp("parallel", "parallel", "arbitrary"),
        name=name,
    )(r, kn, v, lw, kd, bb, tri, ms, mi, hmask, bdm, s0)


def _rw_out_kernel(y0_ref, y1_ref, bonus_ref, gate_ref, lnw_ref, lnb_ref, ed_ref, eu_ref, w_ref, x_ref, g_ref,
                   out_ref):
    y = y0_ref[...].astype(F32) + y1_ref[...].astype(F32)
    ed = ed_ref[...]
    eu = eu_ref[...]
    mu = _seg_sum(y, ed, eu) * (1.0 / RW_HEAD)
    dlt = y - mu
    var = _seg_sum(dlt * dlt, ed, eu) * (1.0 / RW_HEAD)
    yn = dlt * lax.rsqrt(var + RW_GN_EPS) * lnw_ref[...] + lnb_ref[...] + bonus_ref[...].astype(F32)
    out_ref[...] = x_ref[...] + g_ref[...] * _bdot(yn * gate_ref[...].astype(F32), w_ref[...])


def _rw_out(y, bonus, gate_act, ln_w, ln_b, w_out, x2d, gate, *, seq, tm, name):
    m = x2d.shape[0]
    tpb = seq // tm
    e_down, e_up = _seg_mats()
    rowspec = pl.BlockSpec((tm, D_MODEL), lambda i: (i, 0))
    full = lambda a: pl.BlockSpec(a.shape, lambda i: (0,) * a.ndim)
    return pl.pallas_call(
        _rw_out_kernel,
        out_shape=jax.ShapeDtypeStruct((m, D_MODEL), F32),
        grid=(m // tm,),
        in_specs=[pl.BlockSpec((None, tm, D_MODEL), lambda i: (0, i, 0)),
                  pl.BlockSpec((None, tm, D_MODEL), lambda i: (1, i, 0)),
                  rowspec, rowspec, full(ln_w), full(ln_b), full(e_down), full(e_up), full(w_out), rowspec,
                  pl.BlockSpec((None, 1, D_MODEL), lambda i: (i // tpb, 0, 0))],
        out_specs=rowspec,
        compiler_params=_cp("parallel"),
        name=name,
    )(y, y, bonus, gate_act, ln_w, ln_b, e_down, e_up, w_out, x2d, gate)


def _rotate_half_cols(w, n_seg):
    sh = w.shape
    ws = w.reshape(sh[:-1] + (n_seg, 2, sh[-1] // (2 * n_seg)))
    return jnp.concatenate([-ws[..., 1:, :], ws[..., :1, :]], axis=-2).reshape(sh)


def _axial_tables(n_tokens):
    t = jnp.arange(n_tokens)
    row = (t // GRID_W).astype(F32)
    col = (t % GRID_W).astype(F32)
    inv = ROPE_BASE ** (-jnp.arange(0, AXIS_DIM, 2, dtype=F32) / AXIS_DIM)
    ang_r = row[:, None] * inv[None, :]
    ang_c = col[:, None] * inv[None, :]
    ang = jnp.concatenate([ang_r, ang_r, ang_c, ang_c], axis=-1)
    return jnp.tile(jnp.cos(ang), (1, AT_HEADS)), jnp.tile(jnp.sin(ang), (1, AT_HEADS))


def _retention_tables(n_tokens):
    t = jnp.arange(n_tokens, dtype=F32)
    inv = ROPE_BASE ** (-jnp.linspace(0.0, 1.0, RT_DK // 2, dtype=F32))
    ang = t[:, None] * inv[None, :]
    ang = jnp.concatenate([ang, ang], axis=-1)
    return jnp.cos(ang), jnp.sin(ang)


def _forward(x, c, ctx, c_ctx, ada_w, ada_b, norm1_g, norm2_g, ffn_w_up, ffn_conv_w, ffn_conv_b, ffn_w_down,
             hg_w_in, hg_lb_logits, hg_norm_g, hg_w_out, at_w_qkv, at_b_qkv, at_sink, at_w_out,
             rt_w_in, rt_decay_exp, rt_w_out, rw_mix, rw_w_rkv, rw_w0, rw_w1, rw_w2, rw_a0, rw_a1, rw_a2,
             rw_g1, rw_g2, rw_k_k, rw_k_a, rw_r_k, rw_ln_w, rw_ln_b, rw_w_out, final_norm_g, depth):
    bsz, seq, d = x.shape
    nctx = ctx.shape[1]
    tm_l = 512 if seq % 512 == 0 else seq
    tm_f = 256 if seq % 256 == 0 else seq
    tm_c = nctx

    cvecs = jnp.zeros((8, d), F32).at[:bsz].set(c).at[bsz].set(c_ctx)
    mod = _adaln(cvecs, ada_w[:depth], ada_b[:depth])
    lb_all = jnp.cumsum(jax.nn.softmax(hg_lb_logits.astype(F32), axis=0), axis=0)
    zero_b = lambda n: jnp.zeros((1, n), F32)

    xl = x.reshape(bsz * seq, d)
    xc = ctx.reshape(bsz * nctx, d)
    for i in range(depth):
        kind, j = i % 4, i // 4
        need_ctx = i < depth - 1
        ml = mod[i, :bsz].reshape(bsz, 6, 1, d)
        mc = jnp.broadcast_to(mod[i, bsz].reshape(1, 6, 1, d), (bsz, 6, 1, d))
        sh1, sc1, g1, sh2, sc2, g2 = (ml[:, n] for n in range(6))
        csh1, csc1, cg1, csh2, csc2, cg2 = (mc[:, n] for n in range(6))
        mul1, cmul1 = norm1_g[i] * (1.0 + sc1), norm1_g[i] * (1.0 + csc1)
        mul2, cmul2 = norm2_g[i] * (1.0 + sc2), norm2_g[i] * (1.0 + csc2)

        if kind == 0:
            w_in = hg_w_in[j].astype(BF16)
            w_out = hg_w_out[j].astype(BF16)
            lb_row = lb_all[i].reshape(1, d)
            ng_row = jnp.tile(hg_norm_g[j], HG_HEADS).reshape(1, d)
            pc, zc = _proj(xc, cmul1, csh1, w_in, zero_b(5 * d), seq=nctx, tm=tm_c, name="hg_proj_ctx",
                           n_f32_tail=2 * d)
            pL, zl = _proj(xl, mul1, sh1, w_in, zero_b(5 * d), seq=seq, tm=tm_l, name="hg_proj", n_f32_tail=2 * d)
            s0 = jnp.zeros((bsz, 2, HG_HEADS, HG_DK, HG_DK), F32)
            oc, s_ctx = _hgrn_scan(pc, zc, lb_row, s0, bsz=bsz, seq=nctx, name="hg_scan_ctx")
            ol, _ = _hgrn_scan(pL, zl, lb_row, s_ctx, bsz=bsz, seq=seq, name="hg_scan")
            xl = _hgrn_out(ol, pL, ng_row, w_out, xl, g1, seq=seq, tm=tm_l, name="hg_out")
            if need_ctx:
                xc = _hgrn_out(oc, pc, ng_row, w_out, xc, cg1, seq=nctx, tm=tm_c, name="hg_out_ctx")
        elif kind == 1:
            wq = at_w_qkv[j]
            bq = at_b_qkv[j]
            qd = AT_HEADS * AT_HD
            wq_h = wq[:, :qd].reshape(d, AT_HEADS, AT_HD)
            wk_h = wq[:, qd:qd + AT_KV_DIM].reshape(d, AT_KV_HEADS, AT_HD)
            bq_h = bq[:qd].reshape(AT_HEADS, AT_HD)
            bk_h = bq[qd:qd + AT_KV_DIM].reshape(AT_KV_HEADS, AT_HD)
            w_ext = jnp.concatenate([wq[:, :qd], _rotate_half_cols(wq_h, 2).reshape(d, qd), wq[:, qd:],
                                     _rotate_half_cols(wk_h, 2).reshape(d, AT_KV_DIM)], axis=1).astype(BF16)
            b_ext = jnp.concatenate([bq[:qd], _rotate_half_cols(bq_h, 2).reshape(qd), bq[qd:],
                                     _rotate_half_cols(bk_h, 2).reshape(AT_KV_DIM)]).reshape(1, -1)
            w_out = at_w_out[j].astype(BF16)
            sink = at_sink[j].astype(F32)
            cos2, sin2 = _axial_tables(seq)
            pc = _proj(xc, cmul1, csh1, w_ext, b_ext, seq=nctx, tm=tm_c, name="at_proj_ctx")
            pL = _proj(xl, mul1, sh1, w_ext, b_ext, seq=seq, tm=tm_l, name="at_proj")
            q_r, k_r = _rope(pL, cos2, sin2, seq=seq, tm=tm_l)
            ol = _attn(sink, q_r, k_r, pL, pc, bsz=bsz, seq=seq, nctx=nctx)
            xl = _mm_res(ol, w_out, xl, g1, seq=seq, tm=tm_l, name="at_out")
            if need_ctx:
                oc = _attn_ctx(sink, pc, bsz=bsz, nctx=nctx)
                xc = _mm_res(oc, w_out, xc, cg1, seq=nctx, tm=tm_c, name="at_out_ctx")
        elif kind == 2:
            w_in = rt_w_in[j].astype(BF16)
            w_out = rt_w_out[j].astype(BF16)
            cos1, sin1 = _retention_tables(seq)
            ones_t, zeros_t = jnp.ones((nctx, RT_DK), F32), jnp.zeros((nctx, RT_DK), F32)
            pc = _proj(xc, cmul1, csh1, w_in, zero_b(6 * d), seq=nctx, tm=tm_c, name="rt_proj_ctx")
            pL = _proj(xl, mul1, sh1, w_in, zero_b(6 * d), seq=seq, tm=tm_l, name="rt_proj")
            s0 = jnp.zeros((bsz, 2, RT_HEADS, RT_DK, RT_DV), F32)
            oc, s_ctx = _ret_scan(pc, ones_t, zeros_t, rt_decay_exp[j], s0, bsz=bsz, seq=nctx, name="rt_scan_ctx")
            ol, _ = _ret_scan(pL, cos1, sin1, rt_decay_exp[j], s_ctx, bsz=bsz, seq=seq, name="rt_scan")
            xl = _ret_out(ol, pL, w_out, xl, g1, seq=seq, tm=tm_l, name="rt_out")
            if need_ctx:
                xc = _ret_out(oc, pc, w_out, xc, cg1, seq=nctx, tm=tm_c, name="rt_out_ctx")
        else:
            wts = dict(mix=rw_mix[j], w_rkv=rw_w_rkv[j].astype(BF16), w0=rw_w0[j].reshape(2, 1, d),
                       w1=rw_w1[j].astype(BF16), w2=rw_w2[j].astype(BF16), a0=rw_a0[j].reshape(2, 1, d),
                       a1=rw_a1[j].astype(BF16), a2=rw_a2[j].astype(BF16), g1=rw_g1[j].astype(BF16),
                       g2=rw_g2[j].astype(BF16), k_k=rw_k_k[j].reshape(1, d), k_a=rw_k_a[j].reshape(1, d),
                       r_k=rw_r_k[j].reshape(1, d))
            w_out = rw_w_out[j].astype(BF16)
            ln_w, ln_b = rw_ln_w[j].reshape(1, d), rw_ln_b[j].reshape(1, d)
            rc = _rw_proj(xc, cmul1, csh1, wts, seq=nctx, width=nctx, tm=tm_c, quarters=('l', 'l', 'r', 'r'),
                          name="rw_proj_ctx")
            rl = _rw_proj(xl, mul1, sh1, wts, seq=seq, width=GRID_W, tm=tm_f, quarters=('l', 'r', 'u', 'd'),
                          name="rw_proj")
            s0 = jnp.zeros((bsz, 2, RW_NGROUPS, RW_GROUP, RW_GROUP), F32)
            scan_in = lambda t: (t[0], t[2], t[1], t[5], t[6], t[7])
            yc, s_ctx = _rw_scan(*scan_in(rc), s0, bsz=bsz, seq=nctx, name="rw_scan_ctx")
            yl, _ = _rw_scan(*scan_in(rl), s_ctx, bsz=bsz, seq=seq, name="rw_scan")
            xl = _rw_out(yl, rl[4], rl[3], ln_w, ln_b, w_out, xl, g1, seq=seq, tm=tm_l, name="rw_out")
            if need_ctx:
                xc = _rw_out(yc, rc[4], rc[3], ln_w, ln_b, w_out, xc, cg1, seq=nctx, tm=tm_c, name="rw_out_ctx")

        w_up = ffn_w_up[i].astype(BF16)
        w_down = ffn_w_down[i].astype(BF16)
        conv_w = ffn_conv_w[i].reshape(9, D_FF)
        conv_b = ffn_conv_b[i].reshape(1, D_FF)
        fin_g = final_norm_g.reshape(1, d)
        pu = _proj(xl, mul2, sh2, w_up, zero_b(2 * D_FF), seq=seq, tm=tm_l, name="ffn_up")
        xl = _ffn_down(pu, conv_w, conv_b, w_down, xl, g2, fin_g, seq=seq, width=GRID_W, tm=tm_f,
                       final_norm=(i == depth - 1), name="ffn_down")
        if need_ctx:
            pu = _proj(xc, cmul2, csh2, w_up, zero_b(2 * D_FF), seq=nctx, tm=tm_c, name="ffn_up_ctx")
            xc = _ffn_down(pu, conv_w, conv_b, w_down, xc, cg2, fin_g, seq=nctx, width=nctx, tm=tm_c,
                           final_norm=False, name="ffn_down_ctx")
    return xl.reshape(bsz, seq, d)


def kernel(x, c, ctx, c_ctx, ada_w, ada_b, norm1_g, norm2_g, ffn_w_up, ffn_conv_w, ffn_conv_b, ffn_w_down, hg_w_in, hg_lb_logits, hg_norm_g, hg_w_out, at_w_qkv, at_b_qkv, at_sink, at_w_out, rt_w_in, rt_decay_exp, rt_w_out, rw_mix, rw_w_rkv, rw_w0, rw_w1, rw_w2, rw_a0, rw_a1, rw_a2, rw_g1, rw_g2, rw_k_k, rw_k_a, rw_r_k, rw_ln_w, rw_ln_b, rw_w_out, final_norm_g):
    return _forward(x, c, ctx, c_ctx, ada_w, ada_b, norm1_g, norm2_g, ffn_w_up, ffn_conv_w, ffn_conv_b, ffn_w_down,
                    hg_w_in, hg_lb_logits, hg_norm_g, hg_w_out, at_w_qkv, at_b_qkv, at_sink, at_w_out,
                    rt_w_in, rt_decay_exp, rt_w_out, rw_mix, rw_w_rkv, rw_w0, rw_w1, rw_w2, rw_a0, rw_a1, rw_a2,
                    rw_g1, rw_g2, rw_k_k, rw_k_a, rw_r_k, rw_ln_w, rw_ln_b, rw_w_out, final_norm_g, DEPTH)
```

```python
import functools
import math

import numpy as np
import jax
import jax.numpy as jnp
from jax import lax
from jax.experimental import pallas as pl
from jax.experimental.pallas import tpu as pltpu

F32 = jnp.float32
BF16 = jnp.bfloat16

D_MODEL = 1024
DEPTH = 4
GRID_W = 64
NORM_EPS = 1e-6

HG_DK = 128
HG_HEADS = D_MODEL // HG_DK
HG_CHUNK = 128

AT_HD = 64
AT_HEADS = D_MODEL // AT_HD
AT_KV_HEADS = AT_HEADS // 4
AT_GROUP = 4
AT_KV_DIM = AT_KV_HEADS * AT_HD
AT_BLOCK = 128
ROPE_BASE = 10000.0
AXIS_DIM = AT_HD // 2

RT_DK = 256
RT_HEADS = D_MODEL // RT_DK
RT_DV = 2 * RT_DK
RT_V_DIM = RT_HEADS * RT_DV
RT_CHUNK = 256

RW_HEAD = 64
RW_HEADS = D_MODEL // RW_HEAD
RW_GN_EPS = 64e-5
RW_CHUNK = 64
RW_GROUP_HEADS = 4
RW_GROUP = RW_GROUP_HEADS * RW_HEAD
RW_NGROUPS = D_MODEL // RW_GROUP

D_FF = 2816
FF_CHUNK = 256

VMEM_LIMIT = 56 * 1024 * 1024


def _cp(*sem):
    return pltpu.CompilerParams(dimension_semantics=sem, vmem_limit_bytes=VMEM_LIMIT)


def _bdot(a, b):
    return jnp.dot(a.astype(BF16), b.astype(BF16), preferred_element_type=F32)


def _bdot_nt(a, b):
    return lax.dot_general(a.astype(BF16), b.astype(BF16), (((1,), (1,)), ((), ())), preferred_element_type=F32)


def _bdot_tn(a, b):
    return lax.dot_general(a.astype(BF16), b.astype(BF16), (((0,), (0,)), ((), ())), preferred_element_type=F32)


def _split3(x):
    hi = x.astype(BF16)
    r1 = x - hi.astype(F32)
    mid = r1.astype(BF16)
    lo = (r1 - mid.astype(F32)).astype(BF16)
    return hi, mid, lo


def _exact_dot(m_bf16, x):
    hi, mid, lo = _split3(x)
    d = lambda v: jnp.dot(m_bf16, v, preferred_element_type=F32)
    return d(hi) + d(mid) + d(lo)


def _exact_dot2(m_bf16, x):
    hi = x.astype(BF16)
    lo = (x - hi.astype(F32)).astype(BF16)
    return jnp.dot(m_bf16, hi, preferred_element_type=F32) + jnp.dot(m_bf16, lo, preferred_element_type=F32)


def _sigmoid(x):
    return 1.0 / (1.0 + jnp.exp(-x))


def _silu(x):
    return x * _sigmoid(x)


def _rms(x):
    return x * lax.rsqrt(jnp.mean(x * x, axis=-1, keepdims=True) + NORM_EPS)


def _adaln_kernel(c_ref, w_ref, b_ref, o_ref):
    o_ref[...] = _bdot(_silu(c_ref[...]), w_ref[...]) + b_ref[...]


def _adaln(cvecs, ada_w, ada_b):
    depth, d, n = ada_w.shape
    tn = 1536
    return pl.pallas_call(
        _adaln_kernel,
        out_shape=jax.ShapeDtypeStruct((depth, 8, n), F32),
        grid=(depth, n // tn),
        in_specs=[pl.BlockSpec((8, d), lambda l, j: (0, 0)),
                  pl.BlockSpec((None, d, tn), lambda l, j: (l, 0, j)),
                  pl.BlockSpec((None, 1, tn), lambda l, j: (l, 0, j))],
        out_specs=pl.BlockSpec((None, 8, tn), lambda l, j: (l, 0, j)),
        compiler_params=_cp("parallel", "parallel"),
        name="adaln",
    )(cvecs, ada_w.astype(BF16), ada_b.reshape(depth, 1, n))


def _proj_kernel(x_ref, mul_ref, add_ref, w_ref, b_ref, *out_refs, n_lo, tn):
    h = (_rms(x_ref[...]) * mul_ref[...] + add_ref[...]).astype(BF16)
    for n0 in range(0, w_ref.shape[1], tn):
        y = jnp.dot(h, w_ref[:, n0:n0 + tn], preferred_element_type=F32) + b_ref[:, n0:n0 + tn]
        if n0 < n_lo:
            out_refs[0][:, n0:n0 + tn] = y.astype(BF16)
        else:
            out_refs[1][:, n0 - n_lo:n0 - n_lo + tn] = y


def _proj(x2d, mul, add, w, bias, *, seq, tm, name, n_f32_tail=0):
    m, d = x2d.shape
    n = w.shape[1]
    n_lo = n - n_f32_tail
    tn = 512 if (n % 512 == 0 and n_lo % 512 == 0) else 256
    tpb = seq // tm
    out_shape = [jax.ShapeDtypeStruct((m, n_lo), BF16)]
    out_specs = [pl.BlockSpec((tm, n_lo), lambda i: (i, 0))]
    if n_f32_tail:
        out_shape.append(jax.ShapeDtypeStruct((m, n_f32_tail), F32))
        out_specs.append(pl.BlockSpec((tm, n_f32_tail), lambda i: (i, 0)))
    res = pl.pallas_call(
        functools.partial(_proj_kernel, n_lo=n_lo, tn=tn),
        out_shape=tuple(out_shape),
        grid=(m // tm,),
        in_specs=[pl.BlockSpec((tm, d), lambda i: (i, 0)),
                  pl.BlockSpec((None, 1, d), lambda i: (i // tpb, 0, 0)),
                  pl.BlockSpec((None, 1, d), lambda i: (i // tpb, 0, 0)),
                  pl.BlockSpec((d, n), lambda i: (0, 0)),
                  pl.BlockSpec((1, n), lambda i: (0, 0))],
        out_specs=tuple(out_specs),
        compiler_params=_cp("parallel"),
        name=name,
    )(x2d, mul, add, w, bias)
    return res if n_f32_tail else res[0]


def _mm_res_kernel(a_ref, w_ref, x_ref, g_ref, o_ref):
    o_ref[...] = x_ref[...] + g_ref[...] * _bdot(a_ref[...], w_ref[...])


def _mm_res(act, w, x2d, gate, *, seq, tm, name):
    m, k = act.shape
    d = w.shape[1]
    tpb = seq // tm
    return pl.pallas_call(
        _mm_res_kernel,
        out_shape=jax.ShapeDtypeStruct((m, d), F32),
        grid=(m // tm,),
        in_specs=[pl.BlockSpec((tm, k), lambda i: (i, 0)),
                  pl.BlockSpec((k, d), lambda i: (0, 0)),
                  pl.BlockSpec((tm, d), lambda i: (i, 0)),
                  pl.BlockSpec((None, 1, d), lambda i: (i // tpb, 0, 0))],
        out_specs=pl.BlockSpec((tm, d), lambda i: (i, 0)),
        compiler_params=_cp("parallel"),
        name=name,
    )(act, w, x2d, gate)


def _ffn_kernel(x_ref, xp_ref, xn_ref, mul_ref, add_ref, wu_ref, cw_ref, cb_ref, wd_ref, g_ref, fg_ref, o_ref,
                act_scr, *, tm, width, tpi, has_rows, final_norm):
    i = pl.program_id(0)
    mul = mul_ref[...]
    add = add_ref[...]
    hmod = lambda x: (_rms(x) * mul + add).astype(BF16)
    x = x_ref[...]
    hm = hmod(x)
    if has_rows:
        first = (i % tpi) == 0
        last = (i % tpi) == tpi - 1
        hp = jnp.where(first, jnp.zeros((), BF16), hmod(xp_ref[...]))
        hn = jnp.where(last, jnp.zeros((), BF16), hmod(xn_ref[...]))
        h_ext = jnp.concatenate([hp, hm, hn], axis=0)
    else:
        h_ext = hm
    col = lax.broadcasted_iota(jnp.int32, (tm, 1), 0) & (width - 1)
    not_left = col != 0
    not_right = col != width - 1
    inv_sqrt2 = 1.0 / math.sqrt(2.0)
    def up(c0):
        return (jnp.dot(h_ext, wu_ref[:, c0:c0 + FF_CHUNK], preferred_element_type=F32),
                jnp.dot(hm, wu_ref[:, D_FF + c0:D_FF + c0 + FF_CHUNK], preferred_element_type=F32))

    starts = list(range(0, D_FF, FF_CHUNK))
    half = starts[len(starts) // 2]
    nxt = up(starts[0])
    y_lo = None
    for n_c, c0 in enumerate(starts):
        sl = slice(c0, c0 + FF_CHUNK)
        u_ext, vv = nxt
        if n_c + 1 < len(starts):
            nxt = up(starts[n_c + 1])
        if c0 == half:
            y_lo = jnp.dot(act_scr[:, :half], wd_ref[:half, :], preferred_element_type=F32)
        if has_rows:
            rows = (u_ext[:tm], u_ext[width:width + tm], u_ext[2 * width:2 * width + tm])
            colsum = [cw_ref[b:b + 1, sl] * rows[0] + cw_ref[3 + b:4 + b, sl] * rows[1]
                      + cw_ref[6 + b:7 + b, sl] * rows[2] for b in range(3)]
        else:
            colsum = [cw_ref[3 + b:4 + b, sl] * u_ext for b in range(3)]
        acc = (colsum[1] + cb_ref[:, sl]
               + jnp.where(not_left, pltpu.roll(colsum[0], 1, 0), 0.0)
               + jnp.where(not_right, pltpu.roll(colsum[2], tm - 1, 0), 0.0))
        gelu = 0.5 * acc * (1.0 + lax.erf(acc * inv_sqrt2))
        act_scr[:, sl] = (gelu * vv).astype(BF16)
    y_hi = jnp.dot(act_scr[:, half:], wd_ref[half:, :], preferred_element_type=F32)
    y = x + g_ref[...] * (y_lo + y_hi)
    if final_norm:
        y = _rms(y) * fg_ref[...]
    o_ref[...] = y


def _ffn(x2d, mul, add, w_up, conv_w, conv_b, w_down, gate, final_g, *, seq, width, tm, final_norm, name):
    m = x2d.shape[0]
    tpi = seq // tm
    rpt = tm // width
    nrow = m // width
    has_rows = seq > width
    kern = functools.partial(_ffn_kernel, tm=tm, width=width, tpi=tpi, has_rows=has_rows, final_norm=final_norm)
    resident = lambda shape: pl.BlockSpec(shape, lambda i: (0,) * len(shape), pipeline_mode=pl.Buffered(1))
    return pl.pallas_call(
        kern,
        out_shape=jax.ShapeDtypeStruct((m, D_MODEL), F32),
        grid=(m // tm,),
        in_specs=[pl.BlockSpec((tm, D_MODEL), lambda i: (i, 0)),
                  pl.BlockSpec((width, D_MODEL), lambda i: (jnp.maximum(i * rpt - 1, 0), 0)),
                  pl.BlockSpec((width, D_MODEL), lambda i: (jnp.minimum((i + 1) * rpt, nrow - 1), 0)),
                  pl.BlockSpec((None, 1, D_MODEL), lambda i: (i // tpi, 0, 0)),
                  pl.BlockSpec((None, 1, D_MODEL), lambda i: (i // tpi, 0, 0)),
                  resident((D_MODEL, 2 * D_FF)),
                  pl.BlockSpec((9, D_FF), lambda i: (0, 0)),
                  pl.BlockSpec((1, D_FF), lambda i: (0, 0)),
                  resident((D_FF, D_MODEL)),
                  pl.BlockSpec((None, 1, D_MODEL), lambda i: (i // tpi, 0, 0)),
                  pl.BlockSpec((1, D_MODEL), lambda i: (0, 0))],
        out_specs=pl.BlockSpec((tm, D_MODEL), lambda i: (i, 0)),
        scratch_shapes=[pltpu.VMEM((tm, D_FF), BF16)],
        compiler_params=_cp("parallel"),
        name=name,
    )(x2d, x2d, x2d, mul, add, w_up, conv_w, conv_b, w_down, gate, final_g)


def _hgrn_levels(c):
    levels = []
    m = c // 2
    while m >= 1:
        levels.append(m)
        m //= 2
    return levels


def _hgrn_tables(c, rev):
    levels = _hgrn_levels(c)
    msk = np.zeros((len(levels) + 1, c, c), np.float32)
    t = np.arange(c)
    for l, m in enumerate(levels):
        blk = t // (2 * m)
        second = (t % (2 * m)) >= m
        msk[l] = (blk[:, None] == blk[None, :]) & second[:, None] & (~second)[None, :]
    msk[len(levels)] = np.eye(c)
    tri = np.tril(np.ones((c, c), np.float32))
    if rev:
        msk = msk[:, ::-1, ::-1]
        tri = tri.T
    return jnp.asarray(tri, BF16), jnp.asarray(np.ascontiguousarray(msk), F32)


def _level_ref(bcum, m, rev, row):
    c = bcum.shape[0]
    r = m if rev else m - 1
    if 2 * m >= 8:
        return jnp.concatenate([jnp.broadcast_to(bcum[s + r:s + r + 1, :], (2 * m, bcum.shape[1]))
                                for s in range(0, c, 2 * m)], axis=0)
    phase = row & (2 * m - 1)
    out = bcum
    for ph in range(2 * m):
        if ph != r:
            out = jnp.where(phase == ph, pltpu.roll(bcum, (ph - r) % c, 0), out)
    return out


def _hgrn_scan_kernel(q_ref, v_ref, z_ref, lb_ref, tri_ref, m_ref, s0_ref, o_ref, sf_ref, st_scr, *, c, rev):
    j = pl.program_id(0)

    @pl.when(j == 0)
    def _():
        st_scr[...] = s0_ref[...]

    levels = _hgrn_levels(c)
    nl = len(levels)
    lb = lb_ref[...]
    tri = tri_ref[...]
    row = lax.broadcasted_iota(jnp.int32, (c, 1), 0)
    sls = [slice(h * HG_DK, (h + 1) * HG_DK) for h in range(HG_HEADS)]
    nbat = q_ref.shape[0]
    items = [(b, h) for b in range(nbat) for h in range(HG_HEADS)]
    q, kin, v, bcum, btot = [], [], [], [], []
    for b in range(nbat):
        z = z_ref[b]
        logf = jnp.log(lb + (1.0 - lb) * _sigmoid(z))
        kin.append((1.0 - lb) * _sigmoid(-z))
        bcum.append(_exact_dot2(tri, logf))
        btot.append(jnp.sum(logf, axis=0, keepdims=True))
        q.append(q_ref[b].astype(F32))
        v.append(v_ref[b])
    a = [m_ref[nl] * _bdot_nt(q[b][:, sls[h]], kin[b][:, sls[h]]) for b, h in items]
    for l, m in enumerate(levels):
        upper = (row & (2 * m - 1)) >= m
        qside = jnp.logical_not(upper) if rev else upper
        qf, kf = [], []
        for b in range(nbat):
            d = bcum[b] - _level_ref(bcum[b], m, rev, row)
            f = jnp.exp(jnp.where(qside, d, -d))
            qf.append((q[b] * f).astype(BF16))
            kf.append((kin[b] * f).astype(BF16))
        a = [a[i] + m_ref[l] * _bdot_nt(qf[b][:, sls[h]], kf[b][:, sls[h]]) for i, (b, h) in enumerate(items)]
    qtop = [(q[b] * jnp.exp(bcum[b])).astype(BF16) for b in range(nbat)]
    kend = [(kin[b] * jnp.exp(btot[b] - bcum[b])).astype(BF16) for b in range(nbat)]
    dec_tot = [jnp.exp(btot[b]) for b in range(nbat)]
    for i, (b, h) in enumerate(items):
        sl = sls[h]
        st = st_scr[b, h]
        o_ref[b, :, sl] = (_bdot_nt(qtop[b][:, sl], st) + _bdot(a[i], v[b][:, sl])).astype(o_ref.dtype)
        st_scr[b, h] = st * dec_tot[b][:, sl] + _bdot_tn(v[b][:, sl], kend[b][:, sl])

    @pl.when(j == pl.num_programs(0) - 1)
    def _():
        sf_ref[...] = st_scr[...]


def _hgrn_scan(p, zf, lb_row, s0, *, bsz, seq, name):
    c = min(HG_CHUNK, seq)
    nc = seq // c
    st_shape = (HG_HEADS, HG_DK, HG_DK)
    p3 = p.reshape(bsz, seq, -1)
    z3 = zf.reshape(bsz, seq, -1)
    outs, finals = [], []
    for d in range(2):
        tri, masks = _hgrn_tables(c, rev=bool(d))
        crow = (lambda j: nc - 1 - j) if d else (lambda j: j)
        blk = lambda col, crow=crow: pl.BlockSpec((bsz, c, D_MODEL), lambda j: (0, crow(j), col))
        o, sf = pl.pallas_call(
            functools.partial(_hgrn_scan_kernel, c=c, rev=bool(d)),
            out_shape=(jax.ShapeDtypeStruct((bsz, seq, D_MODEL), BF16),
                       jax.ShapeDtypeStruct((bsz,) + st_shape, F32)),
            grid=(nc,),
            in_specs=[blk(0), blk(1), blk(d),
                      pl.BlockSpec((1, D_MODEL), lambda j: (0, 0)),
                      pl.BlockSpec(tri.shape, lambda j: (0, 0)),
                      pl.BlockSpec(masks.shape, lambda j: (0, 0, 0)),
                      pl.BlockSpec((bsz, None) + st_shape, lambda j, d=d: (0, d, 0, 0, 0))],
            out_specs=(blk(0), pl.BlockSpec((bsz,) + st_shape, lambda j: (0, 0, 0, 0))),
            scratch_shapes=[pltpu.VMEM((bsz,) + st_shape, F32)],
            compiler_params=_cp("arbitrary"),
            name=f"{name}_d{d}",
        )(p3, p3, z3, lb_row, tri, masks, s0)
        outs.append(o.reshape(bsz * seq, D_MODEL))
        finals.append(sf)
    return outs, jnp.stack(finals, axis=1)


def _hgrn_out_kernel(o0_ref, o1_ref, gate_ref, ng_ref, w_ref, x_ref, g_ref, out_ref, act_scr):
    o = o0_ref[...].astype(F32) + o1_ref[...].astype(F32)
    gate = gate_ref[...].astype(F32)
    ng = ng_ref[...]
    for h in range(HG_HEADS):
        sl = slice(h * HG_DK, (h + 1) * HG_DK)
        act_scr[:, sl] = (_rms(o[:, sl]) * ng[:, sl] * _silu(gate[:, sl])).astype(BF16)
    out_ref[...] = x_ref[...] + g_ref[...] * jnp.dot(act_scr[...], w_ref[...], preferred_element_type=F32)


def _hgrn_out(o, p, norm_row, w_out, x2d, gate, *, seq, tm, name):
    m = x2d.shape[0]
    tpb = seq // tm
    return pl.pallas_call(
        _hgrn_out_kernel,
        out_shape=jax.ShapeDtypeStruct((m, D_MODEL), F32),
        grid=(m // tm,),
        in_specs=[pl.BlockSpec((tm, D_MODEL), lambda i: (i, 0)),
                  pl.BlockSpec((tm, D_MODEL), lambda i: (i, 0)),
                  pl.BlockSpec((tm, D_MODEL), lambda i: (i, 2)),
                  pl.BlockSpec((1, D_MODEL), lambda i: (0, 0)),
                  pl.BlockSpec((D_MODEL, D_MODEL), lambda i: (0, 0)),
                  pl.BlockSpec((tm, D_MODEL), lambda i: (i, 0)),
                  pl.BlockSpec((None, 1, D_MODEL), lambda i: (i // tpb, 0, 0))],
        out_specs=pl.BlockSpec((tm, D_MODEL), lambda i: (i, 0)),
        scratch_shapes=[pltpu.VMEM((tm, D_MODEL), BF16)],
        compiler_params=_cp("parallel"),
        name=name,
    )(o[0], o[1], p, norm_row, w_out, x2d, gate)


def _rope_kernel(q_ref, qr_ref, k_ref, kr_ref, cos_ref, sin_ref, qo_ref, ko_ref):
    cos = jnp.concatenate([cos_ref[...]] * (D_MODEL // 128), axis=1)
    sin = jnp.concatenate([sin_ref[...]] * (D_MODEL // 128), axis=1)
    f32 = lambda ref: ref[...].astype(F32)
    qo_ref[...] = ((f32(q_ref) * cos + f32(qr_ref) * sin) * (AT_HD ** -0.5)).astype(BF16)
    ko_ref[...] = (f32(k_ref) * cos[:, :AT_KV_DIM] + f32(kr_ref) * sin[:, :AT_KV_DIM]).astype(BF16)


def _rope(p, cos_t, sin_t, *, seq, tm):
    m = p.shape[0]
    tpb = seq // tm
    return pl.pallas_call(
        _rope_kernel,
        out_shape=(jax.ShapeDtypeStruct((m, D_MODEL), BF16), jax.ShapeDtypeStruct((m, AT_KV_DIM), BF16)),
        grid=(m // tm,),
        in_specs=[pl.BlockSpec((tm, D_MODEL), lambda i: (i, 0)),
                  pl.BlockSpec((tm, D_MODEL), lambda i: (i, 1)),
                  pl.BlockSpec((tm, AT_KV_DIM), lambda i: (i, 8)),
                  pl.BlockSpec((tm, AT_KV_DIM), lambda i: (i, 10)),
                  pl.BlockSpec((tm, 128), lambda i: (i % tpb, 0)),
                  pl.BlockSpec((tm, 128), lambda i: (i % tpb, 0))],
        out_specs=(pl.BlockSpec((tm, D_MODEL), lambda i: (i, 0)),
                   pl.BlockSpec((tm, AT_KV_DIM), lambda i: (i, 0))),
        compiler_params=_cp("parallel"),
        name="at_rope",
    )(p, p, p, p, cos_t, sin_t)


def _sink_attend(q, keys, vals, mask, sink_ref, o_ref):
    t = q.shape[0]
    gs = range(AT_KV_HEADS)
    hsl = lambda hd: slice(hd * AT_HD, (hd + 1) * AT_HD)
    q4 = [jnp.concatenate([q[:, hsl(g * AT_GROUP + r)] for r in range(AT_GROUP)], axis=0) for g in gs]
    logits = [_bdot_nt(q4[g], keys[:, hsl(g)]) for g in gs]
    if mask is not None:
        pen = jnp.where(mask, 0.0, -jnp.inf)
        pen4 = jnp.concatenate([pen] * AT_GROUP, axis=0)
        logits = [lg + pen4 for lg in logits]
    s = [jnp.concatenate([jnp.full((t, 1), sink_ref[g * AT_GROUP + r], F32) for r in range(AT_GROUP)], axis=0)
         for g in gs]
    mx = [jnp.maximum(jnp.max(logits[g], axis=-1, keepdims=True), s[g]) for g in gs]
    p = [jnp.exp(logits[g] - mx[g]) for g in gs]
    denom = [jnp.sum(p[g], axis=-1, keepdims=True) + jnp.exp(s[g] - mx[g]) for g in gs]
    o4 = [_bdot(p[g], vals[:, hsl(g)]) / denom[g] for g in gs]
    for g in gs:
        for r in range(AT_GROUP):
            o_ref[:, hsl(g * AT_GROUP + r)] = o4[g][r * t:(r + 1) * t].astype(o_ref.dtype)


def _attn_kernel(sink_ref, q_ref, kp_ref, kc_ref, kn_ref, vp_ref, vc_ref, vn_ref, ck_ref, cv_ref, o_ref, *, nctx):
    i = pl.program_id(1)
    nb = pl.num_programs(1)
    keys = jnp.concatenate([ck_ref[...], kp_ref[...], kc_ref[...], kn_ref[...]], axis=0)
    vals = jnp.concatenate([cv_ref[...], vp_ref[...], vc_ref[...], vn_ref[...]], axis=0)
    nk = nctx + 3 * AT_BLOCK
    ti = lax.broadcasted_iota(jnp.int32, (AT_BLOCK, nk), 0)
    kj = lax.broadcasted_iota(jnp.int32, (AT_BLOCK, nk), 1) - nctx
    lo = jnp.where(i > 0, 0, AT_BLOCK)
    hi = jnp.where(i < nb - 1, 3 * AT_BLOCK, 2 * AT_BLOCK)
    rel = kj - ti
    mask = (kj < 0) | ((rel >= 0) & (rel <= 2 * AT_BLOCK) & (kj >= lo) & (kj < hi))
    _sink_attend(q_ref[...], keys, vals, mask, sink_ref, o_ref)


def _attn(sink, q_r, k_r, p, p_ctx, *, bsz, seq, nctx):
    nb = seq // AT_BLOCK
    blk = lambda b, i: b * nb + i
    prv = lambda b, i: b * nb + jnp.maximum(i - 1, 0)
    nxt = lambda b, i: b * nb + jnp.minimum(i + 1, nb - 1)
    kspec = lambda f: pl.BlockSpec((AT_BLOCK, AT_KV_DIM), lambda b, i: (f(b, i), 0))
    vspec = lambda f: pl.BlockSpec((AT_BLOCK, AT_KV_DIM), lambda b, i: (f(b, i), 9))
    return pl.pallas_call(
        functools.partial(_attn_kernel, nctx=nctx),
        out_shape=jax.ShapeDtypeStruct((bsz * seq, D_MODEL), BF16),
        grid=(bsz, nb),
        in_specs=[pl.BlockSpec(memory_space=pltpu.SMEM),
                  pl.BlockSpec((AT_BLOCK, D_MODEL), lambda b, i: (blk(b, i), 0)),
                  kspec(prv), kspec(blk), kspec(nxt), vspec(prv), vspec(blk), vspec(nxt),
                  pl.BlockSpec((nctx, AT_KV_DIM), lambda b, i: (b, 8)),
                  pl.BlockSpec((nctx, AT_KV_DIM), lambda b, i: (b, 9))],
        out_specs=pl.BlockSpec((AT_BLOCK, D_MODEL), lambda b, i: (blk(b, i), 0)),
        compiler_params=_cp("parallel", "parallel"),
        name="at_window",
    )(sink, q_r, k_r, k_r, k_r, p, p, p, p_ctx, p_ctx)


def _attn_ctx_kernel(sink_ref, q_ref, k_ref, v_ref, o_ref):
    _sink_attend(q_ref[...] * (AT_HD ** -0.5), k_ref[...], v_ref[...], None, sink_ref, o_ref)


def _attn_ctx(sink, p_ctx, *, bsz, nctx):
    return pl.pallas_call(
        _attn_ctx_kernel,
        out_shape=jax.ShapeDtypeStruct((bsz * nctx, D_MODEL), BF16),
        grid=(bsz,),
        in_specs=[pl.BlockSpec(memory_space=pltpu.SMEM),
                  pl.BlockSpec((nctx, D_MODEL), lambda b: (b, 0)),
                  pl.BlockSpec((nctx, AT_KV_DIM), lambda b: (b, 8)),
                  pl.BlockSpec((nctx, AT_KV_DIM), lambda b: (b, 9))],
        out_specs=pl.BlockSpec((nctx, D_MODEL), lambda b: (b, 0)),
        compiler_params=_cp("parallel"),
        name="at_ctx",
    )(sink, p_ctx, p_ctx, p_ctx)


def _ret_scan_kernel(q_ref, k_ref, v_ref, cos_ref, sin_ref, dm_ref, qin_ref, kout_ref, car_ref, s0_ref,
                     o_ref, sf_ref, st_scr):
    j = pl.program_id(2)

    @pl.when(j == 0)
    def _():
        st_scr[...] = s0_ref[...]

    cos = cos_ref[...]
    sin = sin_ref[...]
    half = RT_DK // 2

    def rope(x):
        rot = jnp.concatenate([-x[:, half:], x[:, :half]], axis=1)
        return x * cos + rot * sin

    for h in range(RT_HEADS):
        sk = slice(h * RT_DK, (h + 1) * RT_DK)
        sv = slice(h * RT_DV, (h + 1) * RT_DV)
        qh = rope(q_ref[:, sk].astype(F32))
        kh = rope(k_ref[:, sk].astype(F32) * (RT_DK ** -0.5))
        vh = v_ref[:, sv]
        s = _bdot_nt(qh, kh) * dm_ref[h]
        st = st_scr[h]
        o_ref[:, sv] = (_bdot(qh * qin_ref[:, sk], st) + _bdot(s, vh)).astype(o_ref.dtype)
        st_scr[h] = car_ref[:, sv] * st + _bdot_tn(kh * kout_ref[:, sk], vh)

    @pl.when(j == pl.num_programs(2) - 1)
    def _():
        sf_ref[...] = st_scr[...]


def _ret_tables(decay_exp, c):
    lg = jnp.log1p(-jnp.exp2(decay_exp.astype(F32)))
    idx = jnp.arange(c, dtype=F32)
    pos = jnp.stack([idx, c - 1.0 - idx])
    rel = pos[:, :, None] - pos[:, None, :]
    lgh = lg[:, :, None, None]
    dm = jnp.where(rel[:, None] >= 0, jnp.exp(lgh * jnp.maximum(rel[:, None], 0.0)), 0.0)
    qin = jnp.exp(lg[:, None, :] * (pos[:, :, None] + 1.0))
    kout = jnp.exp(lg[:, None, :] * (c - 1.0 - pos[:, :, None]))
    car = jnp.exp(lg * c)
    qin = jnp.repeat(qin, RT_DK, axis=-1)
    kout = jnp.repeat(kout, RT_DK, axis=-1)
    car = jnp.repeat(car, RT_DV, axis=-1)[:, None, :]
    return dm, qin, kout, car


def _ret_scan(p, cos_t, sin_t, decay_exp, s0, *, bsz, seq, name):
    c = min(RT_CHUNK, seq)
    nc = seq // c
    dm, qin, kout, car = _ret_tables(decay_exp, c)
    crow = lambda b, d, j: j + d * (nc - 1 - 2 * j)
    row = lambda b, d, j: b * nc + crow(b, d, j)
    st_shape = (RT_HEADS, RT_DK, RT_DV)
    return pl.pallas_call(
        _ret_scan_kernel,
        out_shape=(jax.ShapeDtypeStruct((2, bsz * seq, RT_V_DIM), BF16),
                   jax.ShapeDtypeStruct((bsz, 2) + st_shape, F32)),
        grid=(bsz, 2, nc),
        in_specs=[pl.BlockSpec((c, D_MODEL), lambda b, d, j: (row(b, d, j), 0)),
                  pl.BlockSpec((c, D_MODEL), lambda b, d, j: (row(b, d, j), 1)),
                  pl.BlockSpec((c, RT_V_DIM), lambda b, d, j: (row(b, d, j), 1)),
                  pl.BlockSpec((c, RT_DK), lambda b, d, j: (crow(b, d, j), 0)),
                  pl.BlockSpec((c, RT_DK), lambda b, d, j: (crow(b, d, j), 0)),
                  pl.BlockSpec((None, RT_HEADS, c, c), lambda b, d, j: (d, 0, 0, 0)),
                  pl.BlockSpec((None, c, D_MODEL), lambda b, d, j: (d, 0, 0)),
                  pl.BlockSpec((None, c, D_MODEL), lambda b, d, j: (d, 0, 0)),
                  pl.BlockSpec((None, 1, RT_V_DIM), lambda b, d, j: (d, 0, 0)),
                  pl.BlockSpec((None, None) + st_shape, lambda b, d, j: (b, d, 0, 0, 0))],
        out_specs=(pl.BlockSpec((None, c, RT_V_DIM), lambda b, d, j: (d, row(b, d, j), 0)),
                   pl.BlockSpec((None, None) + st_shape, lambda b, d, j: (b, d, 0, 0, 0))),
        scratch_shapes=[pltpu.VMEM(st_shape, F32)],
        compiler_params=_cp("parallel", "parallel", "arbitrary"),
        name=name,
    )(p, p, p, cos_t, sin_t, dm, qin, kout, car, s0)


def _ret_out_kernel(o0_ref, o1_ref, gate_ref, w_ref, x_ref, g_ref, out_ref, act_scr):
    for h in range(RT_HEADS):
        sv = slice(h * RT_DV, (h + 1) * RT_DV)
        o = o0_ref[:, sv].astype(F32) + o1_ref[:, sv].astype(F32)
        act_scr[:, sv] = (_silu(gate_ref[:, sv].astype(F32)) * _rms(o)).astype(BF16)
    out_ref[...] = x_ref[...] + g_ref[...] * jnp.dot(act_scr[...], w_ref[...], preferred_element_type=F32)


def _ret_out(o, p, w_out, x2d, gate, *, seq, tm, name):
    m = x2d.shape[0]
    tpb = seq // tm
    return pl.pallas_call(
        _ret_out_kernel,
        out_shape=jax.ShapeDtypeStruct((m, D_MODEL), F32),
        grid=(m // tm,),
        in_specs=[pl.BlockSpec((None, tm, RT_V_DIM), lambda i: (0, i, 0)),
                  pl.BlockSpec((None, tm, RT_V_DIM), lambda i: (1, i, 0)),
                  pl.BlockSpec((tm, RT_V_DIM), lambda i: (i, 2)),
                  pl.BlockSpec((RT_V_DIM, D_MODEL), lambda i: (0, 0)),
                  pl.BlockSpec((tm, D_MODEL), lambda i: (i, 0)),
                  pl.BlockSpec((None, 1, D_MODEL), lambda i: (i // tpb, 0, 0))],
        out_specs=pl.BlockSpec((tm, D_MODEL), lambda i: (i, 0)),
        scratch_shapes=[pltpu.VMEM((tm, RT_V_DIM), BF16)],
        compiler_params=_cp("parallel"),
        name=name,
    )(o, o, p, w_out, x2d, gate)


def _seg_sum(x, e_down, e_up):
    hi = x.astype(BF16)
    lo = (x - hi.astype(F32)).astype(BF16)
    s = jnp.dot(hi, e_down, preferred_element_type=F32) + jnp.dot(lo, e_down, preferred_element_type=F32)
    s_hi = s.astype(BF16)
    s_lo = (s - s_hi.astype(F32)).astype(BF16)
    return jnp.dot(s_hi, e_up, preferred_element_type=F32) + jnp.dot(s_lo, e_up, preferred_element_type=F32)


def _rw_proj_kernel(x_ref, xp_ref, xn_ref, mul_ref, add_ref, mix_ref, wrkv_ref, w0_ref, w1_ref, w2_ref,
                    a0_ref, a1_ref, a2_ref, g1_ref, g2_ref, kk_ref, ka_ref, rk_ref, ed_ref, eu_ref,
                    r_out, v_out, kn_out, gate_out, bonus_out, lw_out, kd_out, bb_out,
                    *, tm, width, tpi, quarters):
    i = pl.program_id(0)
    first = (i % tpi) == 0
    last = (i % tpi) == tpi - 1
    mul = mul_ref[...]
    add = add_ref[...]
    hmod = lambda x: _rms(x) * mul + add
    hm = hmod(x_ref[...])
    col = lax.broadcasted_iota(jnp.int32, (tm, 1), 0) & (width - 1)
    srcs = {}
    if 'l' in quarters:
        srcs['l'] = jnp.where(col != 0, pltpu.roll(hm, 1, 0), 0.0)
    if 'r' in quarters:
        srcs['r'] = jnp.where(col != width - 1, pltpu.roll(hm, tm - 1, 0), 0.0)
    if 'u' in quarters:
        hp = jnp.where(first, 0.0, hmod(xp_ref[...]))
        srcs['u'] = jnp.concatenate([hp, hm[: tm - width]], axis=0) if tm > width else hp
    if 'd' in quarters:
        hn = jnp.where(last, 0.0, hmod(xn_ref[...]))
        srcs['d'] = jnp.concatenate([hm[width:], hn], axis=0) if tm > width else hn
    qd = D_MODEL // 4
    shifted = jnp.concatenate([srcs[q][:, n * qd:(n + 1) * qd] for n, q in enumerate(quarters)], axis=1)
    xx = shifted - hm
    mixed = lambda n: hm + xx * mix_ref[n:n + 1, :]
    r = _bdot(mixed(0), wrkv_ref[0])
    k = _bdot(mixed(2), wrkv_ref[1])
    v = _bdot(mixed(3), wrkv_ref[2])
    xw = mixed(1)
    xa = mixed(4)
    ed = ed_ref[...]
    eu = eu_ref[...]
    kkh = k * kk_ref[...]
    nrm = jnp.sqrt(_seg_sum(kkh * kkh, ed, eu))
    kn = kkh / jnp.maximum(nrm, 1e-12)
    ksum = jnp.zeros_like(k)
    for z in range(2):
        w_raw = w0_ref[z] + _bdot(jnp.tanh(_bdot(xw, w1_ref[z])), w2_ref[z])
        lw_out[z] = -math.exp(-0.5) * _sigmoid(w_raw)
        a = _sigmoid(a0_ref[z] + _bdot(_bdot(xa, a1_ref[z]), a2_ref[z]))
        kd = k * (1.0 + (a - 1.0) * ka_ref[...])
        kd_out[z] = kd.astype(BF16)
        bb_out[z] = (kn * a).astype(BF16)
        ksum = ksum + kd
    r_out[...] = r.astype(BF16)
    v_out[...] = v.astype(BF16)
    kn_out[...] = kn.astype(BF16)
    gate_out[...] = _bdot(_sigmoid(_bdot(mixed(5), g1_ref[...])), g2_ref[...]).astype(BF16)
    bonus_out[...] = (_seg_sum(r * ksum * rk_ref[...], ed, eu) * v).astype(BF16)


def _seg_mats():
    hid = np.arange(D_MODEL) // RW_HEAD
    e_down = (hid[:, None] == np.arange(128)[None, :]).astype(np.float32)
    return jnp.asarray(e_down, BF16), jnp.asarray(e_down.T, BF16)


def _rw_proj(x2d, mul, add, wts, *, seq, width, tm, quarters, name):
    m = x2d.shape[0]
    tpi = seq // tm
    rpt = tm // width
    nrow = m // width
    e_down, e_up = _seg_mats()
    full = lambda a: pl.BlockSpec(a.shape, lambda i: (0,) * a.ndim)
    consts = [wts['mix'], wts['w_rkv'], wts['w0'], wts['w1'], wts['w2'], wts['a0'], wts['a1'], wts['a2'],
              wts['g1'], wts['g2'], wts['k_k'], wts['k_a'], wts['r_k'], e_down, e_up]
    kern = functools.partial(_rw_proj_kernel, tm=tm, width=width, tpi=tpi, quarters=quarters)
    one = jax.ShapeDtypeStruct((m, D_MODEL), BF16)
    two = jax.ShapeDtypeStruct((2, m, D_MODEL), BF16)
    two_f32 = jax.ShapeDtypeStruct((2, m, D_MODEL), F32)
    ospec1 = pl.BlockSpec((tm, D_MODEL), lambda i: (i, 0))
    ospec2 = pl.BlockSpec((2, tm, D_MODEL), lambda i: (0, i, 0))
    return pl.pallas_call(
        kern,
        out_shape=(one, one, one, one, one, two_f32, two, two),
        grid=(m // tm,),
        in_specs=[pl.BlockSpec((tm, D_MODEL), lambda i: (i, 0)),
                  pl.BlockSpec((width, D_MODEL), lambda i: (jnp.maximum(i * rpt - 1, 0), 0)),
                  pl.BlockSpec((width, D_MODEL), lambda i: (jnp.minimum((i + 1) * rpt, nrow - 1), 0)),
                  pl.BlockSpec((None, 1, D_MODEL), lambda i: (i // tpi, 0, 0)),
                  pl.BlockSpec((None, 1, D_MODEL), lambda i: (i // tpi, 0, 0))] + [full(a) for a in consts],
        out_specs=(ospec1, ospec1, ospec1, ospec1, ospec1, ospec2, ospec2, ospec2),
        compiler_params=_cp("parallel"),
        name=name,
    )(x2d, x2d, x2d, mul, add, *consts)


def _rw_tables(c):
    r = c * RW_GROUP_HEADS
    t = np.arange(r) % c
    hrow = np.arange(r) // c
    same = hrow[:, None] == hrow[None, :]
    strict0 = same & (t[None, :] < t[:, None])
    incl0 = same & (t[None, :] <= t[:, None])
    strict1 = same & (t[None, :] > t[:, None])
    incl1 = same & (t[None, :] >= t[:, None])
    tri0 = np.tril(np.ones((c, c)))
    tri = np.stack([tri0, tri0.T])
    lane_h = np.arange(RW_GROUP) // RW_HEAD
    hmask = hrow[:, None] == lane_h[None, :]
    bdm = lane_h[:, None] == lane_h[None, :]
    ms = np.stack([strict0, strict1]).astype(np.float32)
    mi = np.stack([incl0, incl1]).astype(np.float32)
    return (jnp.asarray(tri, BF16), jnp.asarray(ms), jnp.asarray(mi), jnp.asarray(hmask.astype(np.float32), BF16),
            jnp.asarray(bdm.astype(np.float32)))


def _rw_scan_kernel(r_ref, kn_ref, v_ref, lw_ref, kd_ref, bb_ref, tri_ref, ms_ref, mi_ref, hm_ref, bdm_ref, s0_ref,
                    y_ref, sf_ref, st_scr, *, c):
    j = pl.program_id(1)

    @pl.when(j == 0)
    def _():
        st_scr[...] = s0_ref[...]

    hm = hm_ref[...]
    ms = ms_ref[...]
    mi = mi_ref[...]
    bdm = bdm_ref[...]
    tri = tri_ref[...]
    nrow = c * RW_GROUP_HEADS
    eye = (lax.broadcasted_iota(jnp.int32, (nrow, nrow), 0) == lax.broadcasted_iota(jnp.int32, (nrow, nrow), 1)).astype(F32)
    tile = lambda x: jnp.concatenate([x] * RW_GROUP_HEADS, axis=0)
    unstack = lambda x: functools.reduce(lambda a, b: a + b, [x[n * c:(n + 1) * c] for n in range(RW_GROUP_HEADS)])
    nsq = int(math.log2(c)) - 1
    b16 = lambda x: x.astype(BF16)
    dot = lambda a, b: jnp.dot(a, b, preferred_element_type=F32)
    dot_nt = lambda a, b: lax.dot_general(a, b, (((1,), (1,)), ((), ())), preferred_element_type=F32)
    sls = [slice(gi * RW_GROUP, (gi + 1) * RW_GROUP) for gi in range(RW_NGROUPS)]

    pre = []
    for b in range(r_ref.shape[0]):
        lw = lw_ref[b]
        g = _exact_dot(tri, lw)
        gtot = jnp.sum(lw, axis=0, keepdims=True)
        e_inv = jnp.exp(-g)
        e_end = jnp.exp(gtot - g)
        kd = kd_ref[b].astype(F32)
        bb = bb_ref[b].astype(F32)
        pre.append(dict(kt=b16(kn_ref[b].astype(F32) * jnp.exp(g - lw)), rt=b16(r_ref[b].astype(F32) * jnp.exp(g)),
                        kh=b16(kd * e_inv), bh=b16(bb * e_inv), ke=b16(kd * e_end), be=b16(bb * e_end),
                        v=v_ref[b], dec=jnp.exp(gtot)))

    items = [(b, gi) for b in range(len(pre)) for gi in range(RW_NGROUPS)]
    ids = range(len(items))
    part = lambda name: [pre[b][name][:, sls[gi]] for b, gi in items]
    xk = [tile(x) * hm for x in part('kt')]
    xr = [tile(x) * hm for x in part('rt')]
    yb = [tile(x) for x in part('bh')]
    yk = [tile(x) for x in part('kh')]
    vm = [tile(x) * hm for x in part('v')]
    n = [-(dot_nt(xk[i], yb[i]) * ms) for i in ids]
    l_k = [b16(dot_nt(xk[i], yk[i]) * ms) for i in ids]
    m_k = [b16(dot_nt(xr[i], yk[i]) * mi) for i in ids]
    m_b = [b16(dot_nt(xr[i], yb[i]) * mi) for i in ids]
    p = [eye + n[i] for i in ids]
    nb = [b16(n[i]) for i in ids]
    for _ in range(nsq):
        n = [dot(nb[i], nb[i]) for i in ids]
        nb = [b16(n[i]) for i in ids]
        p = [p[i] + dot(b16(p[i]), nb[i]) for i in ids]
    st = [st_scr[b, gi] for b, gi in items]
    stb = [b16(x) for x in st]
    rhs = [dot_nt(xk[i], stb[i]) + dot(l_k[i], vm[i]) for i in ids]
    u = [dot(b16(p[i]), b16(rhs[i])) for i in ids]
    ym = [dot_nt(xr[i], stb[i]) + dot(m_k[i], vm[i]) - dot(m_b[i], b16(u[i])) for i in ids]
    for i, (b, gi) in enumerate(items):
        sl = sls[gi]
        y_ref[b, :, sl] = unstack(ym[i]).astype(y_ref.dtype)
        lhs = jnp.concatenate([pre[b]['v'][:, sl], b16(-unstack(u[i]))], axis=0)
        rhs_s = jnp.concatenate([pre[b]['ke'][:, sl], pre[b]['be'][:, sl]], axis=0)
        st_scr[b, gi] = st[i] * pre[b]['dec'][:, sl] + bdm * _bdot_tn(lhs, rhs_s)

    @pl.when(j == pl.num_programs(1) - 1)
    def _():
        sf_ref[...] = st_scr[...]


def _rw_scan(r, kn, v, lw, kd, bb, s0, *, bsz, seq, name):
    c = min(RW_CHUNK, seq)
    nc = seq // c
    tri, ms, mi, hmask, bdm = _rw_tables(c)
    crow = lambda d, j: j + d * (nc - 1 - 2 * j)
    st_shape = (RW_NGROUPS, RW_GROUP, RW_GROUP)
    shared = pl.BlockSpec((bsz, c, D_MODEL), lambda d, j: (0, crow(d, j), 0))
    perdir = pl.BlockSpec((None, bsz, c, D_MODEL), lambda d, j: (d, 0, crow(d, j), 0))
    stspec = pl.BlockSpec((bsz, None) + st_shape, lambda d, j: (0, d, 0, 0, 0))
    nrow = c * RW_GROUP_HEADS
    b3 = lambda a: a.reshape(bsz, seq, D_MODEL)
    b4 = lambda a: a.reshape(2, bsz, seq, D_MODEL)
    y, sf = pl.pallas_call(
        functools.partial(_rw_scan_kernel, c=c),
        out_shape=(jax.ShapeDtypeStruct((2, bsz, seq, D_MODEL), BF16),
                   jax.ShapeDtypeStruct((bsz, 2) + st_shape, F32)),
        grid=(2, nc),
        in_specs=[shared, shared, shared, perdir, perdir, perdir,
                  pl.BlockSpec((None, c, c), lambda d, j: (d, 0, 0)),
                  pl.BlockSpec((None, nrow, nrow), lambda d, j: (d, 0, 0)),
                  pl.BlockSpec((None, nrow, nrow), lambda d, j: (d, 0, 0)),
                  pl.BlockSpec((nrow, RW_GROUP), lambda d, j: (0, 0)),
                  pl.BlockSpec((RW_GROUP, RW_GROUP), lambda d, j: (0, 0)),
                  stspec],
        out_specs=(perdir, stspec),
        scratch_shapes=[pltpu.VMEM((bsz,) + st_shape, F32)],
        compiler_params=_cp("parallel", "arbitrary"),
        name=name,
    )(b3(r), b3(kn), b3(v), b4(lw), b4(kd), b4(bb), tri, ms, mi, hmask, bdm, s0)
    return y.reshape(2, bsz * seq, D_MODEL), sf


def _rw_out_kernel(y0_ref, y1_ref, bonus_ref, gate_ref, lnw_ref, lnb_ref, ed_ref, eu_ref, w_ref, x_ref, g_ref,
                   out_ref):
    y = y0_ref[...].astype(F32) + y1_ref[...].astype(F32)
    ed = ed_ref[...]
    eu = eu_ref[...]
    mu = _seg_sum(y, ed, eu) * (1.0 / RW_HEAD)
    dlt = y - mu
    var = _seg_sum(dlt * dlt, ed, eu) * (1.0 / RW_HEAD)
    yn = dlt * lax.rsqrt(var + RW_GN_EPS) * lnw_ref[...] + lnb_ref[...] + bonus_ref[...].astype(F32)
    out_ref[...] = x_ref[...] + g_ref[...] * _bdot(yn * gate_ref[...].astype(F32), w_ref[...])


def _rw_out(y, bonus, gate_act, ln_w, ln_b, w_out, x2d, gate, *, seq, tm, name):
    m = x2d.shape[0]
    tpb = seq // tm
    e_down, e_up = _seg_mats()
    rowspec = pl.BlockSpec((tm, D_MODEL), lambda i: (i, 0))
    full = lambda a: pl.BlockSpec(a.shape, lambda i: (0,) * a.ndim)
    return pl.pallas_call(
        _rw_out_kernel,
        out_shape=jax.ShapeDtypeStruct((m, D_MODEL), F32),
        grid=(m // tm,),
        in_specs=[pl.BlockSpec((None, tm, D_MODEL), lambda i: (0, i, 0)),
                  pl.BlockSpec((None, tm, D_MODEL), lambda i: (1, i, 0)),
                  rowspec, rowspec, full(ln_w), full(ln_b), full(e_down), full(e_up), full(w_out), rowspec,
                  pl.BlockSpec((None, 1, D_MODEL), lambda i: (i // tpb, 0, 0))],
        out_specs=rowspec,
        compiler_params=_cp("parallel"),
        name=name,
    )(y, y, bonus, gate_act, ln_w, ln_b, e_down, e_up, w_out, x2d, gate)


def _rotate_half_cols(w, n_seg):
    sh = w.shape
    ws = w.reshape(sh[:-1] + (n_seg, 2, sh[-1] // (2 * n_seg)))
    return jnp.concatenate([-ws[..., 1:, :], ws[..., :1, :]], axis=-2).reshape(sh)


def _axial_tables(n_tokens):
    t = jnp.arange(n_tokens)
    row = (t // GRID_W).astype(F32)
    col = (t % GRID_W).astype(F32)
    inv = ROPE_BASE ** (-jnp.arange(0, AXIS_DIM, 2, dtype=F32) / AXIS_DIM)
    ang_r = row[:, None] * inv[None, :]
    ang_c = col[:, None] * inv[None, :]
    ang = jnp.concatenate([ang_r, ang_r, ang_c, ang_c], axis=-1)
    return jnp.tile(jnp.cos(ang), (1, 2)), jnp.tile(jnp.sin(ang), (1, 2))


def _retention_tables(n_tokens):
    t = jnp.arange(n_tokens, dtype=F32)
    inv = ROPE_BASE ** (-jnp.linspace(0.0, 1.0, RT_DK // 2, dtype=F32))
    ang = t[:, None] * inv[None, :]
    ang = jnp.concatenate([ang, ang], axis=-1)
    return jnp.cos(ang), jnp.sin(ang)


def _forward(x, c, ctx, c_ctx, ada_w, ada_b, norm1_g, norm2_g, ffn_w_up, ffn_conv_w, ffn_conv_b, ffn_w_down,
             hg_w_in, hg_lb_logits, hg_norm_g, hg_w_out, at_w_qkv, at_b_qkv, at_sink, at_w_out,
             rt_w_in, rt_decay_exp, rt_w_out, rw_mix, rw_w_rkv, rw_w0, rw_w1, rw_w2, rw_a0, rw_a1, rw_a2,
             rw_g1, rw_g2, rw_k_k, rw_k_a, rw_r_k, rw_ln_w, rw_ln_b, rw_w_out, final_norm_g, depth):
    bsz, seq, d = x.shape
    nctx = ctx.shape[1]
    tm_l = 512 if seq % 512 == 0 else seq
    tm_f = 256 if seq % 256 == 0 else seq
    tm_ffn = 1024 if seq % 1024 == 0 else tm_l
    tm_c = nctx

    cvecs = jnp.zeros((8, d), F32).at[:bsz].set(c).at[bsz].set(c_ctx)
    mod = _adaln(cvecs, ada_w[:depth], ada_b[:depth])
    lb_all = jnp.cumsum(jax.nn.softmax(hg_lb_logits.astype(F32), axis=0), axis=0)
    zero_b = lambda n: jnp.zeros((1, n), F32)

    xl = x.reshape(bsz * seq, d)
    xc = ctx.reshape(bsz * nctx, d)
    for i in range(depth):
        kind, j = i % 4, i // 4
        need_ctx = i < depth - 1
        ml = mod[i, :bsz].reshape(bsz, 6, 1, d)
        mc = jnp.broadcast_to(mod[i, bsz].reshape(1, 6, 1, d), (bsz, 6, 1, d))
        sh1, sc1, g1, sh2, sc2, g2 = (ml[:, n] for n in range(6))
        csh1, csc1, cg1, csh2, csc2, cg2 = (mc[:, n] for n in range(6))
        mul1, cmul1 = norm1_g[i] * (1.0 + sc1), norm1_g[i] * (1.0 + csc1)
        mul2, cmul2 = norm2_g[i] * (1.0 + sc2), norm2_g[i] * (1.0 + csc2)

        if kind == 0:
            w_in = hg_w_in[j].astype(BF16)
            w_out = hg_w_out[j].astype(BF16)
            lb_row = lb_all[i].reshape(1, d)
            ng_row = jnp.tile(hg_norm_g[j], HG_HEADS).reshape(1, d)
            pc, zc = _proj(xc, cmul1, csh1, w_in, zero_b(5 * d), seq=nctx, tm=tm_c, name="hg_proj_ctx",
                           n_f32_tail=2 * d)
            pL, zl = _proj(xl, mul1, sh1, w_in, zero_b(5 * d), seq=seq, tm=tm_l, name="hg_proj", n_f32_tail=2 * d)
            s0 = jnp.zeros((bsz, 2, HG_HEADS, HG_DK, HG_DK), F32)
            oc, s_ctx = _hgrn_scan(pc, zc, lb_row, s0, bsz=bsz, seq=nctx, name="hg_scan_ctx")
            ol, _ = _hgrn_scan(pL, zl, lb_row, s_ctx, bsz=bsz, seq=seq, name="hg_scan")
            xl = _hgrn_out(ol, pL, ng_row, w_out, xl, g1, seq=seq, tm=tm_l, name="hg_out")
            if need_ctx:
                xc = _hgrn_out(oc, pc, ng_row, w_out, xc, cg1, seq=nctx, tm=tm_c, name="hg_out_ctx")
        elif kind == 1:
            wq = at_w_qkv[j]
            bq = at_b_qkv[j]
            qd = AT_HEADS * AT_HD
            wq_h = wq[:, :qd].reshape(d, AT_HEADS, AT_HD)
            wk_h = wq[:, qd:qd + AT_KV_DIM].reshape(d, AT_KV_HEADS, AT_HD)
            bq_h = bq[:qd].reshape(AT_HEADS, AT_HD)
            bk_h = bq[qd:qd + AT_KV_DIM].reshape(AT_KV_HEADS, AT_HD)
            w_ext = jnp.concatenate([wq[:, :qd], _rotate_half_cols(wq_h, 2).reshape(d, qd), wq[:, qd:],
                                     _rotate_half_cols(wk_h, 2).reshape(d, AT_KV_DIM)], axis=1).astype(BF16)
            b_ext = jnp.concatenate([bq[:qd], _rotate_half_cols(bq_h, 2).reshape(qd), bq[qd:],
                                     _rotate_half_cols(bk_h, 2).reshape(AT_KV_DIM)]).reshape(1, -1)
            w_out = at_w_out[j].astype(BF16)
            sink = at_sink[j].astype(F32)
            cos2, sin2 = _axial_tables(seq)
            pc = _proj(xc, cmul1, csh1, w_ext, b_ext, seq=nctx, tm=tm_c, name="at_proj_ctx")
            pL = _proj(xl, mul1, sh1, w_ext, b_ext, seq=seq, tm=tm_l, name="at_proj")
            q_r, k_r = _rope(pL, cos2, sin2, seq=seq, tm=tm_l)
            ol = _attn(sink, q_r, k_r, pL, pc, bsz=bsz, seq=seq, nctx=nctx)
            xl = _mm_res(ol, w_out, xl, g1, seq=seq, tm=tm_l, name="at_out")
            if need_ctx:
                oc = _attn_ctx(sink, pc, bsz=bsz, nctx=nctx)
                xc = _mm_res(oc, w_out, xc, cg1, seq=nctx, tm=tm_c, name="at_out_ctx")
        elif kind == 2:
            w_in = rt_w_in[j].astype(BF16)
            w_out = rt_w_out[j].astype(BF16)
            cos1, sin1 = _retention_tables(seq)
            ones_t, zeros_t = jnp.ones((nctx, RT_DK), F32), jnp.zeros((nctx, RT_DK), F32)
            pc = _proj(xc, cmul1, csh1, w_in, zero_b(6 * d), seq=nctx, tm=tm_c, name="rt_proj_ctx")
            pL = _proj(xl, mul1, sh1, w_in, zero_b(6 * d), seq=seq, tm=tm_l, name="rt_proj")
            s0 = jnp.zeros((bsz, 2, RT_HEADS, RT_DK, RT_DV), F32)
            oc, s_ctx = _ret_scan(pc, ones_t, zeros_t, rt_decay_exp[j], s0, bsz=bsz, seq=nctx, name="rt_scan_ctx")
            ol, _ = _ret_scan(pL, cos1, sin1, rt_decay_exp[j], s_ctx, bsz=bsz, seq=seq, name="rt_scan")
            xl = _ret_out(ol, pL, w_out, xl, g1, seq=seq, tm=tm_l, name="rt_out")
            if need_ctx:
                xc = _ret_out(oc, pc, w_out, xc, cg1, seq=nctx, tm=tm_c, name="rt_out_ctx")
        else:
            wts = dict(mix=rw_mix[j], w_rkv=rw_w_rkv[j].astype(BF16), w0=rw_w0[j].reshape(2, 1, d),
                       w1=rw_w1[j].astype(BF16), w2=rw_w2[j].astype(BF16), a0=rw_a0[j].reshape(2, 1, d),
                       a1=rw_a1[j].astype(BF16), a2=rw_a2[j].astype(BF16), g1=rw_g1[j].astype(BF16),
                       g2=rw_g2[j].astype(BF16), k_k=rw_k_k[j].reshape(1, d), k_a=rw_k_a[j].reshape(1, d),
                       r_k=rw_r_k[j].reshape(1, d))
            w_out = rw_w_out[j].astype(BF16)
            ln_w, ln_b = rw_ln_w[j].reshape(1, d), rw_ln_b[j].reshape(1, d)
            rc = _rw_proj(xc, cmul1, csh1, wts, seq=nctx, width=nctx, tm=tm_c, quarters=('l', 'l', 'r', 'r'),
                          name="rw_proj_ctx")
            rl = _rw_proj(xl, mul1, sh1, wts, seq=seq, width=GRID_W, tm=tm_f, quarters=('l', 'r', 'u', 'd'),
                          name="rw_proj")
            s0 = jnp.zeros((bsz, 2, RW_NGROUPS, RW_GROUP, RW_GROUP), F32)
            scan_in = lambda t: (t[0], t[2], t[1], t[5], t[6], t[7])
            yc, s_ctx = _rw_scan(*scan_in(rc), s0, bsz=bsz, seq=nctx, name="rw_scan_ctx")
            yl, _ = _rw_scan(*scan_in(rl), s_ctx, bsz=bsz, seq=seq, name="rw_scan")
            xl = _rw_out(yl, rl[4], rl[3], ln_w, ln_b, w_out, xl, g1, seq=seq, tm=tm_l, name="rw_out")
            if need_ctx:
                xc = _rw_out(yc, rc[4], rc[3], ln_w, ln_b, w_out, xc, cg1, seq=nctx, tm=tm_c, name="rw_out_ctx")

        w_up = ffn_w_up[i].astype(BF16)
        w_down = ffn_w_down[i].astype(BF16)
        conv_w = ffn_conv_w[i].reshape(9, D_FF)
        conv_b = ffn_conv_b[i].reshape(1, D_FF)
        fin_g = final_norm_g.reshape(1, d)
        xl = _ffn(xl, mul2, sh2, w_up, conv_w, conv_b, w_down, g2, fin_g, seq=seq, width=GRID_W, tm=tm_ffn,
                  final_norm=(i == depth - 1), name="ffn")
        if need_ctx:
            xc = _ffn(xc, cmul2, csh2, w_up, conv_w, conv_b, w_down, cg2, fin_g, seq=nctx, width=nctx, tm=tm_c,
                      final_norm=False, name="ffn_ctx")
    return xl.reshape(bsz, seq, d)


def kernel(x, c, ctx, c_ctx, ada_w, ada_b, norm1_g, norm2_g, ffn_w_up, ffn_conv_w, ffn_conv_b, ffn_w_down, hg_w_in, hg_lb_logits, hg_norm_g, hg_w_out, at_w_qkv, at_b_qkv, at_sink, at_w_out, rt_w_in, rt_decay_exp, rt_w_out, rw_mix, rw_w_rkv, rw_w0, rw_w1, rw_w2, rw_a0, rw_a1, rw_a2, rw_g1, rw_g2, rw_k_k, rw_k_a, rw_r_k, rw_ln_w, rw_ln_b, rw_w_out, final_norm_g):
    return _forward(x, c, ctx, c_ctx, ada_w, ada_b, norm1_g, norm2_g, ffn_w_up, ffn_conv_w, ffn_conv_b, ffn_w_down,
                    hg_w_in, hg_lb_logits, hg_norm_g, hg_w_out, at_w_qkv, at_b_qkv, at_sink, at_w_out,
                    rt_w_in, rt_decay_exp, rt_w_out, rw_mix, rw_w_rkv, rw_w0, rw_w1, rw_w2, rw_a0, rw_a1, rw_a2,
                    rw_g1, rw_g2, rw_k_k, rw_k_a, rw_r_k, rw_ln_w, rw_ln_b, rw_w_out, final_norm_g, DEPTH)
```

```python
import functools
import math

import numpy as np
import jax
import jax.numpy as jnp
from jax import lax
from jax.experimental import pallas as pl
from jax.experimental.pallas import tpu as pltpu

F32 = jnp.float32
BF16 = jnp.bfloat16

D_MODEL = 1024
DEPTH = 4
GRID_W = 64
NORM_EPS = 1e-6

HG_DK = 128
HG_HEADS = D_MODEL // HG_DK
HG_CHUNK = 128

AT_HD = 64
AT_HEADS = D_MODEL // AT_HD
AT_KV_HEADS = AT_HEADS // 4
AT_GROUP = 4
AT_KV_DIM = AT_KV_HEADS * AT_HD
AT_BLOCK = 128
ROPE_BASE = 10000.0
AXIS_DIM = AT_HD // 2

RT_DK = 256
RT_HEADS = D_MODEL // RT_DK
RT_DV = 2 * RT_DK
RT_V_DIM = RT_HEADS * RT_DV
RT_CHUNK = 256

RW_HEAD = 64
RW_HEADS = D_MODEL // RW_HEAD
RW_GN_EPS = 64e-5
RW_CHUNK = 64
RW_GROUP_HEADS = 4
RW_GROUP = RW_GROUP_HEADS * RW_HEAD
RW_NGROUPS = D_MODEL // RW_GROUP

D_FF = 2816
FF_CHUNK = 256
FF_DOWN_PARTS = 2

VMEM_LIMIT = 56 * 1024 * 1024


def _cp(*sem):
    return pltpu.CompilerParams(dimension_semantics=sem, vmem_limit_bytes=VMEM_LIMIT)


def _bdot(a, b):
    return jnp.dot(a.astype(BF16), b.astype(BF16), preferred_element_type=F32)


def _bdot_nt(a, b):
    return lax.dot_general(a.astype(BF16), b.astype(BF16), (((1,), (1,)), ((), ())), preferred_element_type=F32)


def _bdot_tn(a, b):
    return lax.dot_general(a.astype(BF16), b.astype(BF16), (((0,), (0,)), ((), ())), preferred_element_type=F32)


def _split3(x):
    hi = x.astype(BF16)
    r1 = x - hi.astype(F32)
    mid = r1.astype(BF16)
    lo = (r1 - mid.astype(F32)).astype(BF16)
    return hi, mid, lo


def _exact_dot(m_bf16, x):
    hi, mid, lo = _split3(x)
    d = lambda v: jnp.dot(m_bf16, v, preferred_element_type=F32)
    return d(hi) + d(mid) + d(lo)


def _exact_dot2(m_bf16, x):
    hi = x.astype(BF16)
    lo = (x - hi.astype(F32)).astype(BF16)
    return jnp.dot(m_bf16, hi, preferred_element_type=F32) + jnp.dot(m_bf16, lo, preferred_element_type=F32)


def _sigmoid(x):
    return 1.0 / (1.0 + jnp.exp(-x))


def _silu(x):
    return x * _sigmoid(x)


def _rms(x):
    return x * lax.rsqrt(jnp.mean(x * x, axis=-1, keepdims=True) + NORM_EPS)


def _adaln_kernel(c_ref, w_ref, b_ref, o_ref):
    o_ref[...] = _bdot(_silu(c_ref[...]), w_ref[...]) + b_ref[...]


def _adaln(cvecs, ada_w, ada_b):
    depth, d, n = ada_w.shape
    tn = 1536
    return pl.pallas_call(
        _adaln_kernel,
        out_shape=jax.ShapeDtypeStruct((depth, 8, n), F32),
        grid=(depth, n // tn),
        in_specs=[pl.BlockSpec((8, d), lambda l, j: (0, 0)),
                  pl.BlockSpec((None, d, tn), lambda l, j: (l, 0, j)),
                  pl.BlockSpec((None, 1, tn), lambda l, j: (l, 0, j))],
        out_specs=pl.BlockSpec((None, 8, tn), lambda l, j: (l, 0, j)),
        compiler_params=_cp("parallel", "parallel"),
        name="adaln",
    )(cvecs, ada_w.astype(BF16), ada_b.reshape(depth, 1, n))


def _proj_kernel(x_ref, mul_ref, add_ref, w_ref, b_ref, *out_refs, n_lo, tn):
    h = (_rms(x_ref[...]) * mul_ref[...] + add_ref[...]).astype(BF16)
    for n0 in range(0, w_ref.shape[1], tn):
        y = jnp.dot(h, w_ref[:, n0:n0 + tn], preferred_element_type=F32) + b_ref[:, n0:n0 + tn]
        if n0 < n_lo:
            out_refs[0][:, n0:n0 + tn] = y.astype(BF16)
        else:
            out_refs[1][:, n0 - n_lo:n0 - n_lo + tn] = y


def _proj(x2d, mul, add, w, bias, *, seq, tm, name, n_f32_tail=0):
    m, d = x2d.shape
    n = w.shape[1]
    n_lo = n - n_f32_tail
    tn = 512 if (n % 512 == 0 and n_lo % 512 == 0) else 256
    tpb = seq // tm
    out_shape = [jax.ShapeDtypeStruct((m, n_lo), BF16)]
    out_specs = [pl.BlockSpec((tm, n_lo), lambda i: (i, 0))]
    if n_f32_tail:
        out_shape.append(jax.ShapeDtypeStruct((m, n_f32_tail), F32))
        out_specs.append(pl.BlockSpec((tm, n_f32_tail), lambda i: (i, 0)))
    res = pl.pallas_call(
        functools.partial(_proj_kernel, n_lo=n_lo, tn=tn),
        out_shape=tuple(out_shape),
        grid=(m // tm,),
        in_specs=[pl.BlockSpec((tm, d), lambda i: (i, 0)),
                  pl.BlockSpec((None, 1, d), lambda i: (i // tpb, 0, 0)),
                  pl.BlockSpec((None, 1, d), lambda i: (i // tpb, 0, 0)),
                  pl.BlockSpec((d, n), lambda i: (0, 0)),
                  pl.BlockSpec((1, n), lambda i: (0, 0))],
        out_specs=tuple(out_specs),
        compiler_params=_cp("parallel"),
        name=name,
    )(x2d, mul, add, w, bias)
    return res if n_f32_tail else res[0]


def _mm_res_kernel(a_ref, w_ref, x_ref, g_ref, o_ref):
    o_ref[...] = x_ref[...] + g_ref[...] * _bdot(a_ref[...], w_ref[...])


def _mm_res(act, w, x2d, gate, *, seq, tm, name):
    m, k = act.shape
    d = w.shape[1]
    tpb = seq // tm
    return pl.pallas_call(
        _mm_res_kernel,
        out_shape=jax.ShapeDtypeStruct((m, d), F32),
        grid=(m // tm,),
        in_specs=[pl.BlockSpec((tm, k), lambda i: (i, 0)),
                  pl.BlockSpec((k, d), lambda i: (0, 0)),
                  pl.BlockSpec((tm, d), lambda i: (i, 0)),
                  pl.BlockSpec((None, 1, d), lambda i: (i // tpb, 0, 0))],
        out_specs=pl.BlockSpec((tm, d), lambda i: (i, 0)),
        compiler_params=_cp("parallel"),
        name=name,
    )(act, w, x2d, gate)


def _ffn_kernel(x_ref, xp_ref, xn_ref, mul_ref, add_ref, wu_ref, cw_ref, cb_ref, wd_ref, g_ref, fg_ref, o_ref,
                act_scr, *, tm, width, tpi, has_rows, final_norm):
    i = pl.program_id(0)
    mul = mul_ref[...]
    add = add_ref[...]
    hmod = lambda x: (_rms(x) * mul + add).astype(BF16)
    x = x_ref[...]
    hm = hmod(x)
    if has_rows:
        first = (i % tpi) == 0
        last = (i % tpi) == tpi - 1
        hp = jnp.where(first, jnp.zeros((), BF16), hmod(xp_ref[...]))
        hn = jnp.where(last, jnp.zeros((), BF16), hmod(xn_ref[...]))
        h_ext = jnp.concatenate([hp, hm, hn], axis=0)
    else:
        h_ext = hm
    col = lax.broadcasted_iota(jnp.int32, (tm, 1), 0) & (width - 1)
    not_left = col != 0
    not_right = col != width - 1
    inv_sqrt2 = 1.0 / math.sqrt(2.0)
    def up(c0):
        return (jnp.dot(h_ext, wu_ref[:, c0:c0 + FF_CHUNK], preferred_element_type=F32),
                jnp.dot(hm, wu_ref[:, D_FF + c0:D_FF + c0 + FF_CHUNK], preferred_element_type=F32))

    starts = list(range(0, D_FF, FF_CHUNK))
    cuts = [starts[(len(starts) * n) // FF_DOWN_PARTS] for n in range(1, FF_DOWN_PARTS)]
    down = lambda lo, hi: jnp.dot(act_scr[:, lo:hi], wd_ref[lo:hi, :], preferred_element_type=F32)
    nxt = up(starts[0])
    y_mm = None
    done = 0
    for n_c, c0 in enumerate(starts):
        sl = slice(c0, c0 + FF_CHUNK)
        u_ext, vv = nxt
        if n_c + 1 < len(starts):
            nxt = up(starts[n_c + 1])
        if c0 in cuts:
            part = down(done, c0)
            y_mm = part if y_mm is None else y_mm + part
            done = c0
        if has_rows:
            rows = (u_ext[:tm], u_ext[width:width + tm], u_ext[2 * width:2 * width + tm])
            colsum = [cw_ref[b:b + 1, sl] * rows[0] + cw_ref[3 + b:4 + b, sl] * rows[1]
                      + cw_ref[6 + b:7 + b, sl] * rows[2] for b in range(3)]
        else:
            colsum = [cw_ref[3 + b:4 + b, sl] * u_ext for b in range(3)]
        acc = (colsum[1] + cb_ref[:, sl]
               + jnp.where(not_left, pltpu.roll(colsum[0], 1, 0), 0.0)
               + jnp.where(not_right, pltpu.roll(colsum[2], tm - 1, 0), 0.0))
        gelu = 0.5 * acc * (1.0 + lax.erf(acc * inv_sqrt2))
        act_scr[:, sl] = (gelu * vv).astype(BF16)
    y = x + g_ref[...] * (y_mm + down(done, D_FF))
    if final_norm:
        y = _rms(y) * fg_ref[...]
    o_ref[...] = y


def _ffn(x2d, mul, add, w_up, conv_w, conv_b, w_down, gate, final_g, *, seq, width, tm, final_norm, name):
    m = x2d.shape[0]
    tpi = seq // tm
    rpt = tm // width
    nrow = m // width
    has_rows = seq > width
    kern = functools.partial(_ffn_kernel, tm=tm, width=width, tpi=tpi, has_rows=has_rows, final_norm=final_norm)
    resident = lambda shape: pl.BlockSpec(shape, lambda i: (0,) * len(shape), pipeline_mode=pl.Buffered(1))
    return pl.pallas_call(
        kern,
        out_shape=jax.ShapeDtypeStruct((m, D_MODEL), F32),
        grid=(m // tm,),
        in_specs=[pl.BlockSpec((tm, D_MODEL), lambda i: (i, 0)),
                  pl.BlockSpec((width, D_MODEL), lambda i: (jnp.maximum(i * rpt - 1, 0), 0)),
                  pl.BlockSpec((width, D_MODEL), lambda i: (jnp.minimum((i + 1) * rpt, nrow - 1), 0)),
                  pl.BlockSpec((None, 1, D_MODEL), lambda i: (i // tpi, 0, 0)),
                  pl.BlockSpec((None, 1, D_MODEL), lambda i: (i // tpi, 0, 0)),
                  resident((D_MODEL, 2 * D_FF)),
                  pl.BlockSpec((9, D_FF), lambda i: (0, 0)),
                  pl.BlockSpec((1, D_FF), lambda i: (0, 0)),
                  resident((D_FF, D_MODEL)),
                  pl.BlockSpec((None, 1, D_MODEL), lambda i: (i // tpi, 0, 0)),
                  pl.BlockSpec((1, D_MODEL), lambda i: (0, 0))],
        out_specs=pl.BlockSpec((tm, D_MODEL), lambda i: (i, 0)),
        scratch_shapes=[pltpu.VMEM((tm, D_FF), BF16)],
        compiler_params=_cp("parallel"),
        name=name,
    )(x2d, x2d, x2d, mul, add, w_up, conv_w, conv_b, w_down, gate, final_g)


def _hgrn_levels(c):
    levels = []
    m = c // 2
    while m >= 1:
        levels.append(m)
        m //= 2
    return levels


def _hgrn_tables(c, rev):
    levels = _hgrn_levels(c)
    msk = np.zeros((len(levels) + 1, c, c), np.float32)
    t = np.arange(c)
    for l, m in enumerate(levels):
        blk = t // (2 * m)
        second = (t % (2 * m)) >= m
        msk[l] = (blk[:, None] == blk[None, :]) & second[:, None] & (~second)[None, :]
    msk[len(levels)] = np.eye(c)
    tri = np.tril(np.ones((c, c), np.float32))
    if rev:
        msk = msk[:, ::-1, ::-1]
        tri = tri.T
    return jnp.asarray(tri, BF16), jnp.asarray(np.ascontiguousarray(msk), F32)


def _level_ref(bcum, m, rev, row):
    c = bcum.shape[0]
    r = m if rev else m - 1
    if 2 * m >= 8:
        return jnp.concatenate([jnp.broadcast_to(bcum[s + r:s + r + 1, :], (2 * m, bcum.shape[1]))
                                for s in range(0, c, 2 * m)], axis=0)
    phase = row & (2 * m - 1)
    out = bcum
    for ph in range(2 * m):
        if ph != r:
            out = jnp.where(phase == ph, pltpu.roll(bcum, (ph - r) % c, 0), out)
    return out


def _hgrn_scan_kernel(q_ref, v_ref, z_ref, lb_ref, tri_ref, m_ref, s0_ref, o_ref, sf_ref, st_scr, *, c, rev):
    j = pl.program_id(0)

    @pl.when(j == 0)
    def _():
        st_scr[...] = s0_ref[...]

    levels = _hgrn_levels(c)
    nl = len(levels)
    lb = lb_ref[...]
    tri = tri_ref[...]
    row = lax.broadcasted_iota(jnp.int32, (c, 1), 0)
    sls = [slice(h * HG_DK, (h + 1) * HG_DK) for h in range(HG_HEADS)]
    nbat = q_ref.shape[0]
    items = [(b, h) for b in range(nbat) for h in range(HG_HEADS)]
    q, kin, v, bcum, btot = [], [], [], [], []
    for b in range(nbat):
        z = z_ref[b]
        logf = jnp.log(lb + (1.0 - lb) * _sigmoid(z))
        kin.append((1.0 - lb) * _sigmoid(-z))
        bcum.append(_exact_dot2(tri, logf))
        btot.append(jnp.sum(logf, axis=0, keepdims=True))
        q.append(q_ref[b].astype(F32))
        v.append(v_ref[b])
    a = [m_ref[nl] * _bdot_nt(q[b][:, sls[h]], kin[b][:, sls[h]]) for b, h in items]
    for l, m in enumerate(levels):
        qf, kf = [], []
        for b in range(nbat):
            f = jnp.exp2(jnp.abs(bcum[b] - _level_ref(bcum[b], m, rev, row)) * (-1.0 / math.log(2.0)))
            qf.append((q[b] * f).astype(BF16))
            kf.append((kin[b] * f).astype(BF16))
        a = [a[i] + m_ref[l] * _bdot_nt(qf[b][:, sls[h]], kf[b][:, sls[h]]) for i, (b, h) in enumerate(items)]
    qtop = [(q[b] * jnp.exp(bcum[b])).astype(BF16) for b in range(nbat)]
    kend = [(kin[b] * jnp.exp(btot[b] - bcum[b])).astype(BF16) for b in range(nbat)]
    dec_tot = [jnp.exp(btot[b]) for b in range(nbat)]
    for i, (b, h) in enumerate(items):
        sl = sls[h]
        st = st_scr[b, h]
        o_ref[b, :, sl] = (_bdot_nt(qtop[b][:, sl], st) + _bdot(a[i], v[b][:, sl])).astype(o_ref.dtype)
        st_scr[b, h] = st * dec_tot[b][:, sl] + _bdot_tn(v[b][:, sl], kend[b][:, sl])

    @pl.when(j == pl.num_programs(0) - 1)
    def _():
        sf_ref[...] = st_scr[...]


def _hgrn_scan(p, zf, lb_row, s0, *, bsz, seq, name):
    c = min(HG_CHUNK, seq)
    nc = seq // c
    st_shape = (HG_HEADS, HG_DK, HG_DK)
    p3 = p.reshape(bsz, seq, -1)
    z3 = zf.reshape(bsz, seq, -1)
    outs, finals = [], []
    for d in range(2):
        tri, masks = _hgrn_tables(c, rev=bool(d))
        crow = (lambda j: nc - 1 - j) if d else (lambda j: j)
        blk = lambda col, crow=crow: pl.BlockSpec((bsz, c, D_MODEL), lambda j: (0, crow(j), col))
        o, sf = pl.pallas_call(
            functools.partial(_hgrn_scan_kernel, c=c, rev=bool(d)),
            out_shape=(jax.ShapeDtypeStruct((bsz, seq, D_MODEL), BF16),
                       jax.ShapeDtypeStruct((bsz,) + st_shape, F32)),
            grid=(nc,),
            in_specs=[blk(0), blk(1), blk(d),
                      pl.BlockSpec((1, D_MODEL), lambda j: (0, 0)),
                      pl.BlockSpec(tri.shape, lambda j: (0, 0)),
                      pl.BlockSpec(masks.shape, lambda j: (0, 0, 0)),
                      pl.BlockSpec((bsz, None) + st_shape, lambda j, d=d: (0, d, 0, 0, 0))],
            out_specs=(blk(0), pl.BlockSpec((bsz,) + st_shape, lambda j: (0, 0, 0, 0))),
            scratch_shapes=[pltpu.VMEM((bsz,) + st_shape, F32)],
            compiler_params=_cp("arbitrary"),
            name=f"{name}_d{d}",
        )(p3, p3, z3, lb_row, tri, masks, s0)
        outs.append(o.reshape(bsz * seq, D_MODEL))
        finals.append(sf)
    return outs, jnp.stack(finals, axis=1)


def _hgrn_out_kernel(o0_ref, o1_ref, gate_ref, ng_ref, w_ref, x_ref, g_ref, out_ref, act_scr):
    o = o0_ref[...].astype(F32) + o1_ref[...].astype(F32)
    gate = gate_ref[...].astype(F32)
    ng = ng_ref[...]
    for h in range(HG_HEADS):
        sl = slice(h * HG_DK, (h + 1) * HG_DK)
        act_scr[:, sl] = (_rms(o[:, sl]) * ng[:, sl] * _silu(gate[:, sl])).astype(BF16)
    out_ref[...] = x_ref[...] + g_ref[...] * jnp.dot(act_scr[...], w_ref[...], preferred_element_type=F32)


def _hgrn_out(o, p, norm_row, w_out, x2d, gate, *, seq, tm, name):
    m = x2d.shape[0]
    tpb = seq // tm
    return pl.pallas_call(
        _hgrn_out_kernel,
        out_shape=jax.ShapeDtypeStruct((m, D_MODEL), F32),
        grid=(m // tm,),
        in_specs=[pl.BlockSpec((tm, D_MODEL), lambda i: (i, 0)),
                  pl.BlockSpec((tm, D_MODEL), lambda i: (i, 0)),
                  pl.BlockSpec((tm, D_MODEL), lambda i: (i, 2)),
                  pl.BlockSpec((1, D_MODEL), lambda i: (0, 0)),
                  pl.BlockSpec((D_MODEL, D_MODEL), lambda i: (0, 0)),
                  pl.BlockSpec((tm, D_MODEL), lambda i: (i, 0)),
                  pl.BlockSpec((None, 1, D_MODEL), lambda i: (i // tpb, 0, 0))],
        out_specs=pl.BlockSpec((tm, D_MODEL), lambda i: (i, 0)),
        scratch_shapes=[pltpu.VMEM((tm, D_MODEL), BF16)],
        compiler_params=_cp("parallel"),
        name=name,
    )(o[0], o[1], p, norm_row, w_out, x2d, gate)


def _rope_kernel(q_ref, qr_ref, k_ref, kr_ref, cos_ref, sin_ref, qo_ref, ko_ref):
    cos = jnp.concatenate([cos_ref[...]] * (D_MODEL // 128), axis=1)
    sin = jnp.concatenate([sin_ref[...]] * (D_MODEL // 128), axis=1)
    f32 = lambda ref: ref[...].astype(F32)
    qo_ref[...] = ((f32(q_ref) * cos + f32(qr_ref) * sin) * (AT_HD ** -0.5)).astype(BF16)
    ko_ref[...] = (f32(k_ref) * cos[:, :AT_KV_DIM] + f32(kr_ref) * sin[:, :AT_KV_DIM]).astype(BF16)


def _rope(p, cos_t, sin_t, *, seq, tm):
    m = p.shape[0]
    tpb = seq // tm
    return pl.pallas_call(
        _rope_kernel,
        out_shape=(jax.ShapeDtypeStruct((m, D_MODEL), BF16), jax.ShapeDtypeStruct((m, AT_KV_DIM), BF16)),
        grid=(m // tm,),
        in_specs=[pl.BlockSpec((tm, D_MODEL), lambda i: (i, 0)),
                  pl.BlockSpec((tm, D_MODEL), lambda i: (i, 1)),
                  pl.BlockSpec((tm, AT_KV_DIM), lambda i: (i, 8)),
                  pl.BlockSpec((tm, AT_KV_DIM), lambda i: (i, 10)),
                  pl.BlockSpec((tm, 128), lambda i: (i % tpb, 0)),
                  pl.BlockSpec((tm, 128), lambda i: (i % tpb, 0))],
        out_specs=(pl.BlockSpec((tm, D_MODEL), lambda i: (i, 0)),
                   pl.BlockSpec((tm, AT_KV_DIM), lambda i: (i, 0))),
        compiler_params=_cp("parallel"),
        name="at_rope",
    )(p, p, p, p, cos_t, sin_t)


def _sink_attend(qs, keys, vals, mask, sink_ref, o_ref):
    t = qs[0].shape[0]
    hsl = lambda hd: slice(hd * AT_HD, (hd + 1) * AT_HD)
    items = [(n, g) for n in range(len(qs)) for g in range(AT_KV_HEADS)]
    q4 = [jnp.concatenate([qs[n][:, hsl(g * AT_GROUP + r)] for r in range(AT_GROUP)], axis=0) for n, g in items]
    logits = [_bdot_nt(q4[i], keys[n][:, hsl(g)]) for i, (n, g) in enumerate(items)]
    if mask is not None:
        pen = jnp.where(mask, 0.0, -jnp.inf)
        pen4 = jnp.concatenate([pen] * AT_GROUP, axis=0)
        logits = [lg + pen4 for lg in logits]
    s = [jnp.concatenate([jnp.full((t, 1), sink_ref[g * AT_GROUP + r], F32) for r in range(AT_GROUP)], axis=0)
         for n, g in items]
    mx = [jnp.maximum(jnp.max(lg, axis=-1, keepdims=True), s[i]) for i, lg in enumerate(logits)]
    p = [jnp.exp(lg - mx[i]) for i, lg in enumerate(logits)]
    denom = [jnp.sum(p[i], axis=-1, keepdims=True) + jnp.exp(s[i] - mx[i]) for i in range(len(items))]
    o4 = [_bdot(p[i], vals[n][:, hsl(g)]) / denom[i] for i, (n, g) in enumerate(items)]
    for i, (n, g) in enumerate(items):
        for r in range(AT_GROUP):
            o_ref[n, :, hsl(g * AT_GROUP + r)] = o4[i][r * t:(r + 1) * t].astype(o_ref.dtype)


def _attn_kernel(sink_ref, q_ref, kp_ref, kc_ref, kn_ref, vp_ref, vc_ref, vn_ref, ck_ref, cv_ref, o_ref, *, nctx):
    i = pl.program_id(0)
    nb = pl.num_programs(0)
    bs = range(q_ref.shape[0])
    keys = [jnp.concatenate([ck_ref[n], kp_ref[n], kc_ref[n], kn_ref[n]], axis=0) for n in bs]
    vals = [jnp.concatenate([cv_ref[n], vp_ref[n], vc_ref[n], vn_ref[n]], axis=0) for n in bs]
    nk = nctx + 3 * AT_BLOCK
    ti = lax.broadcasted_iota(jnp.int32, (AT_BLOCK, nk), 0)
    kj = lax.broadcasted_iota(jnp.int32, (AT_BLOCK, nk), 1) - nctx
    lo = jnp.where(i > 0, 0, AT_BLOCK)
    hi = jnp.where(i < nb - 1, 3 * AT_BLOCK, 2 * AT_BLOCK)
    rel = kj - ti
    mask = (kj < 0) | ((rel >= 0) & (rel <= 2 * AT_BLOCK) & (kj >= lo) & (kj < hi))
    _sink_attend([q_ref[n] for n in bs], keys, vals, mask, sink_ref, o_ref)


def _attn(sink, q_r, k_r, p, p_ctx, *, bsz, seq, nctx):
    nb = seq // AT_BLOCK
    cur = lambda i: i
    prv = lambda i: jnp.maximum(i - 1, 0)
    nxt = lambda i: jnp.minimum(i + 1, nb - 1)
    kspec = lambda f: pl.BlockSpec((bsz, AT_BLOCK, AT_KV_DIM), lambda i: (0, f(i), 0))
    vspec = lambda f: pl.BlockSpec((bsz, AT_BLOCK, AT_KV_DIM), lambda i: (0, f(i), 9))
    q3 = q_r.reshape(bsz, seq, D_MODEL)
    k3 = k_r.reshape(bsz, seq, AT_KV_DIM)
    p3 = p.reshape(bsz, seq, -1)
    c3 = p_ctx.reshape(bsz, nctx, -1)
    out = pl.pallas_call(
        functools.partial(_attn_kernel, nctx=nctx),
        out_shape=jax.ShapeDtypeStruct((bsz, seq, D_MODEL), BF16),
        grid=(nb,),
        in_specs=[pl.BlockSpec(memory_space=pltpu.SMEM),
                  pl.BlockSpec((bsz, AT_BLOCK, D_MODEL), lambda i: (0, i, 0)),
                  kspec(prv), kspec(cur), kspec(nxt), vspec(prv), vspec(cur), vspec(nxt),
                  pl.BlockSpec((bsz, nctx, AT_KV_DIM), lambda i: (0, 0, 8)),
                  pl.BlockSpec((bsz, nctx, AT_KV_DIM), lambda i: (0, 0, 9))],
        out_specs=pl.BlockSpec((bsz, AT_BLOCK, D_MODEL), lambda i: (0, i, 0)),
        compiler_params=_cp("parallel"),
        name="at_window",
    )(sink, q3, k3, k3, k3, p3, p3, p3, c3, c3)
    return out.reshape(bsz * seq, D_MODEL)


def _attn_ctx_kernel(sink_ref, q_ref, k_ref, v_ref, o_ref):
    _sink_attend([q_ref[0] * (AT_HD ** -0.5)], [k_ref[0]], [v_ref[0]], None, sink_ref, o_ref)


def _attn_ctx(sink, p_ctx, *, bsz, nctx):
    c3 = p_ctx.reshape(bsz, nctx, -1)
    out = pl.pallas_call(
        _attn_ctx_kernel,
        out_shape=jax.ShapeDtypeStruct((bsz, nctx, D_MODEL), BF16),
        grid=(bsz,),
        in_specs=[pl.BlockSpec(memory_space=pltpu.SMEM),
                  pl.BlockSpec((1, nctx, D_MODEL), lambda b: (b, 0, 0)),
                  pl.BlockSpec((1, nctx, AT_KV_DIM), lambda b: (b, 0, 8)),
                  pl.BlockSpec((1, nctx, AT_KV_DIM), lambda b: (b, 0, 9))],
        out_specs=pl.BlockSpec((1, nctx, D_MODEL), lambda b: (b, 0, 0)),
        compiler_params=_cp("parallel"),
        name="at_ctx",
    )(sink, c3, c3, c3)
    return out.reshape(bsz * nctx, D_MODEL)


def _ret_scan_kernel(q_ref, k_ref, v_ref, cos_ref, sin_ref, dm_ref, qin_ref, kout_ref, car_ref, s0_ref,
                     o_ref, sf_ref, st_scr):
    j = pl.program_id(2)

    @pl.when(j == 0)
    def _():
        st_scr[...] = s0_ref[...]

    cos = cos_ref[...]
    sin = sin_ref[...]
    half = RT_DK // 2

    def rope(x):
        rot = jnp.concatenate([-x[:, half:], x[:, :half]], axis=1)
        return x * cos + rot * sin

    for h in range(RT_HEADS):
        sk = slice(h * RT_DK, (h + 1) * RT_DK)
        sv = slice(h * RT_DV, (h + 1) * RT_DV)
        qh = rope(q_ref[:, sk].astype(F32))
        kh = rope(k_ref[:, sk].astype(F32) * (RT_DK ** -0.5))
        vh = v_ref[:, sv]
        s = _bdot_nt(qh, kh) * dm_ref[h]
        st = st_scr[h]
        o_ref[:, sv] = (_bdot(qh * qin_ref[:, sk], st) + _bdot(s, vh)).astype(o_ref.dtype)
        st_scr[h] = car_ref[:, sv] * st + _bdot_tn(kh * kout_ref[:, sk], vh)

    @pl.when(j == pl.num_programs(2) - 1)
    def _():
        sf_ref[...] = st_scr[...]


def _ret_tables(decay_exp, c):
    lg = jnp.log1p(-jnp.exp2(decay_exp.astype(F32)))
    idx = jnp.arange(c, dtype=F32)
    pos = jnp.stack([idx, c - 1.0 - idx])
    rel = pos[:, :, None] - pos[:, None, :]
    lgh = lg[:, :, None, None]
    dm = jnp.where(rel[:, None] >= 0, jnp.exp(lgh * jnp.maximum(rel[:, None], 0.0)), 0.0)
    qin = jnp.exp(lg[:, None, :] * (pos[:, :, None] + 1.0))
    kout = jnp.exp(lg[:, None, :] * (c - 1.0 - pos[:, :, None]))
    car = jnp.exp(lg * c)
    qin = jnp.repeat(qin, RT_DK, axis=-1)
    kout = jnp.repeat(kout, RT_DK, axis=-1)
    car = jnp.repeat(car, RT_DV, axis=-1)[:, None, :]
    return dm, qin, kout, car


def _ret_scan(p, cos_t, sin_t, decay_exp, s0, *, bsz, seq, name):
    c = min(RT_CHUNK, seq)
    nc = seq // c
    dm, qin, kout, car = _ret_tables(decay_exp, c)
    crow = lambda b, d, j: j + d * (nc - 1 - 2 * j)
    row = lambda b, d, j: b * nc + crow(b, d, j)
    st_shape = (RT_HEADS, RT_DK, RT_DV)
    return pl.pallas_call(
        _ret_scan_kernel,
        out_shape=(jax.ShapeDtypeStruct((2, bsz * seq, RT_V_DIM), BF16),
                   jax.ShapeDtypeStruct((bsz, 2) + st_shape, F32)),
        grid=(bsz, 2, nc),
        in_specs=[pl.BlockSpec((c, D_MODEL), lambda b, d, j: (row(b, d, j), 0)),
                  pl.BlockSpec((c, D_MODEL), lambda b, d, j: (row(b, d, j), 1)),
                  pl.BlockSpec((c, RT_V_DIM), lambda b, d, j: (row(b, d, j), 1)),
                  pl.BlockSpec((c, RT_DK), lambda b, d, j: (crow(b, d, j), 0)),
                  pl.BlockSpec((c, RT_DK), lambda b, d, j: (crow(b, d, j), 0)),
                  pl.BlockSpec((None, RT_HEADS, c, c), lambda b, d, j: (d, 0, 0, 0)),
                  pl.BlockSpec((None, c, D_MODEL), lambda b, d, j: (d, 0, 0)),
                  pl.BlockSpec((None, c, D_MODEL), lambda b, d, j: (d, 0, 0)),
                  pl.BlockSpec((None, 1, RT_V_DIM), lambda b, d, j: (d, 0, 0)),
                  pl.BlockSpec((None, None) + st_shape, lambda b, d, j: (b, d, 0, 0, 0))],
        out_specs=(pl.BlockSpec((None, c, RT_V_DIM), lambda b, d, j: (d, row(b, d, j), 0)),
                   pl.BlockSpec((None, None) + st_shape, lambda b, d, j: (b, d, 0, 0, 0))),
        scratch_shapes=[pltpu.VMEM(st_shape, F32)],
        compiler_params=_cp("parallel", "parallel", "arbitrary"),
        name=name,
    )(p, p, p, cos_t, sin_t, dm, qin, kout, car, s0)


def _ret_out_kernel(o0_ref, o1_ref, gate_ref, w_ref, x_ref, g_ref, out_ref, act_scr):
    for h in range(RT_HEADS):
        sv = slice(h * RT_DV, (h + 1) * RT_DV)
        o = o0_ref[:, sv].astype(F32) + o1_ref[:, sv].astype(F32)
        act_scr[:, sv] = (_silu(gate_ref[:, sv].astype(F32)) * _rms(o)).astype(BF16)
    out_ref[...] = x_ref[...] + g_ref[...] * jnp.dot(act_scr[...], w_ref[...], preferred_element_type=F32)


def _ret_out(o, p, w_out, x2d, gate, *, seq, tm, name):
    m = x2d.shape[0]
    tpb = seq // tm
    return pl.pallas_call(
        _ret_out_kernel,
        out_shape=jax.ShapeDtypeStruct((m, D_MODEL), F32),
        grid=(m // tm,),
        in_specs=[pl.BlockSpec((None, tm, RT_V_DIM), lambda i: (0, i, 0)),
                  pl.BlockSpec((None, tm, RT_V_DIM), lambda i: (1, i, 0)),
                  pl.BlockSpec((tm, RT_V_DIM), lambda i: (i, 2)),
                  pl.BlockSpec((RT_V_DIM, D_MODEL), lambda i: (0, 0)),
                  pl.BlockSpec((tm, D_MODEL), lambda i: (i, 0)),
                  pl.BlockSpec((None, 1, D_MODEL), lambda i: (i // tpb, 0, 0))],
        out_specs=pl.BlockSpec((tm, D_MODEL), lambda i: (i, 0)),
        scratch_shapes=[pltpu.VMEM((tm, RT_V_DIM), BF16)],
        compiler_params=_cp("parallel"),
        name=name,
    )(o, o, p, w_out, x2d, gate)


def _seg_sum(x, e_down, e_up):
    hi = x.astype(BF16)
    lo = (x - hi.astype(F32)).astype(BF16)
    s = jnp.dot(hi, e_down, preferred_element_type=F32) + jnp.dot(lo, e_down, preferred_element_type=F32)
    s_hi = s.astype(BF16)
    s_lo = (s - s_hi.astype(F32)).astype(BF16)
    return jnp.dot(s_hi, e_up, preferred_element_type=F32) + jnp.dot(s_lo, e_up, preferred_element_type=F32)


def _rw_proj_kernel(x_ref, xp_ref, xn_ref, mul_ref, add_ref, mix_ref, wrkv_ref, w0_ref, w1_ref, w2_ref,
                    a0_ref, a1_ref, a2_ref, g1_ref, g2_ref, kk_ref, ka_ref, rk_ref, ed_ref, eu_ref,
                    r_out, v_out, kn_out, gate_out, bonus_out, lw_out, kd_out, bb_out,
                    *, tm, width, tpi, quarters):
    i = pl.program_id(0)
    first = (i % tpi) == 0
    last = (i % tpi) == tpi - 1
    mul = mul_ref[...]
    add = add_ref[...]
    hmod = lambda x: _rms(x) * mul + add
    hm = hmod(x_ref[...])
    col = lax.broadcasted_iota(jnp.int32, (tm, 1), 0) & (width - 1)
    srcs = {}
    if 'l' in quarters:
        srcs['l'] = jnp.where(col != 0, pltpu.roll(hm, 1, 0), 0.0)
    if 'r' in quarters:
        srcs['r'] = jnp.where(col != width - 1, pltpu.roll(hm, tm - 1, 0), 0.0)
    if 'u' in quarters:
        hp = jnp.where(first, 0.0, hmod(xp_ref[...]))
        srcs['u'] = jnp.concatenate([hp, hm[: tm - width]], axis=0) if tm > width else hp
    if 'd' in quarters:
        hn = jnp.where(last, 0.0, hmod(xn_ref[...]))
        srcs['d'] = jnp.concatenate([hm[width:], hn], axis=0) if tm > width else hn
    qd = D_MODEL // 4
    shifted = jnp.concatenate([srcs[q][:, n * qd:(n + 1) * qd] for n, q in enumerate(quarters)], axis=1)
    xx = shifted - hm
    mixed = lambda n: hm + xx * mix_ref[n:n + 1, :]
    r = _bdot(mixed(0), wrkv_ref[0])
    k = _bdot(mixed(2), wrkv_ref[1])
    v = _bdot(mixed(3), wrkv_ref[2])
    xw = mixed(1)
    xa = mixed(4)
    ed = ed_ref[...]
    eu = eu_ref[...]
    kkh = k * kk_ref[...]
    nrm = jnp.sqrt(_seg_sum(kkh * kkh, ed, eu))
    kn = kkh / jnp.maximum(nrm, 1e-12)
    ksum = jnp.zeros_like(k)
    for z in range(2):
        w_raw = w0_ref[z] + _bdot(jnp.tanh(_bdot(xw, w1_ref[z])), w2_ref[z])
        lw_out[z] = -math.exp(-0.5) * _sigmoid(w_raw)
        a = _sigmoid(a0_ref[z] + _bdot(_bdot(xa, a1_ref[z]), a2_ref[z]))
        kd = k * (1.0 + (a - 1.0) * ka_ref[...])
        kd_out[z] = kd.astype(BF16)
        bb_out[z] = (kn * a).astype(BF16)
        ksum = ksum + kd
    r_out[...] = r.astype(BF16)
    v_out[...] = v.astype(BF16)
    kn_out[...] = kn.astype(BF16)
    gate_out[...] = _bdot(_sigmoid(_bdot(mixed(5), g1_ref[...])), g2_ref[...]).astype(BF16)
    bonus_out[...] = (_seg_sum(r * ksum * rk_ref[...], ed, eu) * v).astype(BF16)


def _seg_mats():
    hid = np.arange(D_MODEL) // RW_HEAD
    e_down = (hid[:, None] == np.arange(128)[None, :]).astype(np.float32)
    return jnp.asarray(e_down, BF16), jnp.asarray(e_down.T, BF16)


def _rw_proj(x2d, mul, add, wts, *, seq, width, tm, quarters, name):
    m = x2d.shape[0]
    tpi = seq // tm
    rpt = tm // width
    nrow = m // width
    e_down, e_up = _seg_mats()
    full = lambda a: pl.BlockSpec(a.shape, lambda i: (0,) * a.ndim)
    consts = [wts['mix'], wts['w_rkv'], wts['w0'], wts['w1'], wts['w2'], wts['a0'], wts['a1'], wts['a2'],
              wts['g1'], wts['g2'], wts['k_k'], wts['k_a'], wts['r_k'], e_down, e_up]
    kern = functools.partial(_rw_proj_kernel, tm=tm, width=width, tpi=tpi, quarters=quarters)
    one = jax.ShapeDtypeStruct((m, D_MODEL), BF16)
    two = jax.ShapeDtypeStruct((2, m, D_MODEL), BF16)
    two_f32 = jax.ShapeDtypeStruct((2, m, D_MODEL), F32)
    ospec1 = pl.BlockSpec((tm, D_MODEL), lambda i: (i, 0))
    ospec2 = pl.BlockSpec((2, tm, D_MODEL), lambda i: (0, i, 0))
    return pl.pallas_call(
        kern,
        out_shape=(one, one, one, one, one, two_f32, two, two),
        grid=(m // tm,),
        in_specs=[pl.BlockSpec((tm, D_MODEL), lambda i: (i, 0)),
                  pl.BlockSpec((width, D_MODEL), lambda i: (jnp.maximum(i * rpt - 1, 0), 0)),
                  pl.BlockSpec((width, D_MODEL), lambda i: (jnp.minimum((i + 1) * rpt, nrow - 1), 0)),
                  pl.BlockSpec((None, 1, D_MODEL), lambda i: (i // tpi, 0, 0)),
                  pl.BlockSpec((None, 1, D_MODEL), lambda i: (i // tpi, 0, 0))] + [full(a) for a in consts],
        out_specs=(ospec1, ospec1, ospec1, ospec1, ospec1, ospec2, ospec2, ospec2),
        compiler_params=_cp("parallel"),
        name=name,
    )(x2d, x2d, x2d, mul, add, *consts)


def _rw_tables(c):
    t = np.arange(c)[:, None]
    s = np.arange(c * RW_GROUP_HEADS)[None, :] % c
    ms = np.stack([s < t, s > t]).astype(np.float32)
    mi = np.stack([s <= t, s >= t]).astype(np.float32)
    tri0 = np.tril(np.ones((c, c)))
    tri = np.stack([tri0, tri0.T])
    hrow = np.arange(c * RW_GROUP_HEADS) // c
    lane_h = np.arange(RW_GROUP) // RW_HEAD
    hmask = hrow[:, None] == lane_h[None, :]
    bdc = hrow[:, None] == hrow[None, :]
    bdm = lane_h[:, None] == lane_h[None, :]
    return (jnp.asarray(tri, BF16), jnp.asarray(ms), jnp.asarray(mi), jnp.asarray(hmask.astype(np.float32), BF16),
            jnp.asarray(bdc.astype(np.float32), BF16), jnp.asarray(bdm.astype(np.float32)))


def _rw_scan_kernel(r_ref, kn_ref, v_ref, lw_ref, kd_ref, bb_ref, tri_ref, ms_ref, mi_ref, hm_ref, bdc_ref, bdm_ref,
                    s0_ref, y_ref, sf_ref, st_scr, *, c):
    j = pl.program_id(1)

    @pl.when(j == 0)
    def _():
        st_scr[...] = s0_ref[...]

    hm = hm_ref[...]
    ms = ms_ref[...]
    mi = mi_ref[...]
    bdc = bdc_ref[...]
    bdm = bdm_ref[...]
    tri = tri_ref[...]
    ncat = c * RW_GROUP_HEADS
    eye = (lax.broadcasted_iota(jnp.int32, (c, ncat), 0)
           == (lax.broadcasted_iota(jnp.int32, (c, ncat), 1) & (c - 1))).astype(F32)
    tile = lambda x: jnp.concatenate([x] * RW_GROUP_HEADS, axis=0)
    stacked = lambda x: tile(x) * hm
    blockdiag = lambda x: tile(x) * bdc
    rows = lambda a, b: jnp.concatenate([a, b], axis=0)
    nsq = int(math.log2(c)) - 1
    b16 = lambda x: x.astype(BF16)
    dot = lambda a, b: jnp.dot(a, b, preferred_element_type=F32)
    dot_nt = lambda a, b: lax.dot_general(a, b, (((1,), (1,)), ((), ())), preferred_element_type=F32)
    sls = [slice(gi * RW_GROUP, (gi + 1) * RW_GROUP) for gi in range(RW_NGROUPS)]

    pre = []
    for b in range(r_ref.shape[0]):
        lw = lw_ref[b]
        g = _exact_dot(tri, lw)
        gtot = jnp.sum(lw, axis=0, keepdims=True)
        e_inv = jnp.exp(-g)
        e_end = jnp.exp(gtot - g)
        kd = kd_ref[b].astype(F32)
        bb = bb_ref[b].astype(F32)
        pre.append(dict(kt=b16(kn_ref[b].astype(F32) * jnp.exp(g - lw)), rt=b16(r_ref[b].astype(F32) * jnp.exp(g)),
                        kh=b16(kd * e_inv), bh=b16(bb * e_inv), ke=b16(kd * e_end), be=b16(bb * e_end),
                        v=v_ref[b], dec=jnp.exp(gtot)))

    items = [(b, gi) for b in range(len(pre)) for gi in range(RW_NGROUPS)]
    ids = range(len(items))
    part = lambda name: [pre[b][name][:, sls[gi]] for b, gi in items]
    kr = [rows(a, b) for a, b in zip(part('kt'), part('rt'))]
    yb = [stacked(x) for x in part('bh')]
    yk = [stacked(x) for x in part('kh')]
    vm = [stacked(x) for x in part('v')]
    by_b = [dot_nt(kr[i], yb[i]) for i in ids]
    by_k = [dot_nt(kr[i], yk[i]) for i in ids]
    n = [-(by_b[i][:c] * ms) for i in ids]
    m_b = [b16(by_b[i][c:] * mi) for i in ids]
    lm_k = [rows(b16(by_k[i][:c] * ms), b16(by_k[i][c:] * mi)) for i in ids]
    p = [eye + n[i] for i in ids]
    nb = [b16(n[i]) for i in ids]
    n = [dot(nb[i], blockdiag(nb[i])) for i in ids]
    for lvl in range(nsq):
        nb = [b16(n[i]) for i in ids]
        nbd = [blockdiag(nb[i]) for i in ids]
        if lvl + 1 < nsq:
            both = [dot(rows(nb[i], b16(p[i])), nbd[i]) for i in ids]
            n = [both[i][:c] for i in ids]
            p = [p[i] + both[i][c:] for i in ids]
        else:
            p = [p[i] + dot(b16(p[i]), nbd[i]) for i in ids]
    st = [st_scr[b, gi] for b, gi in items]
    by_s = [dot_nt(kr[i], b16(st[i])) for i in ids]
    by_v = [dot(lm_k[i], vm[i]) for i in ids]
    u = [dot(b16(p[i]), stacked(b16(by_s[i][:c] + by_v[i][:c]))) for i in ids]
    y = [by_s[i][c:] + by_v[i][c:] - dot(m_b[i], stacked(b16(u[i]))) for i in ids]
    for i, (b, gi) in enumerate(items):
        sl = sls[gi]
        y_ref[b, :, sl] = y[i].astype(y_ref.dtype)
        lhs = rows(pre[b]['v'][:, sl], b16(-u[i]))
        rhs_s = rows(pre[b]['ke'][:, sl], pre[b]['be'][:, sl])
        st_scr[b, gi] = st[i] * pre[b]['dec'][:, sl] + bdm * _bdot_tn(lhs, rhs_s)

    @pl.when(j == pl.num_programs(1) - 1)
    def _():
        sf_ref[...] = st_scr[...]


def _rw_scan(r, kn, v, lw, kd, bb, s0, *, bsz, seq, name):
    c = min(RW_CHUNK, seq)
    nc = seq // c
    tri, ms, mi, hmask, bdc, bdm = _rw_tables(c)
    crow = lambda d, j: j + d * (nc - 1 - 2 * j)
    st_shape = (RW_NGROUPS, RW_GROUP, RW_GROUP)
    shared = pl.BlockSpec((bsz, c, D_MODEL), lambda d, j: (0, crow(d, j), 0))
    perdir = pl.BlockSpec((None, bsz, c, D_MODEL), lambda d, j: (d, 0, crow(d, j), 0))
    stspec = pl.BlockSpec((bsz, None) + st_shape, lambda d, j: (0, d, 0, 0, 0))
    nrow = c * RW_GROUP_HEADS
    b3 = lambda a: a.reshape(bsz, seq, D_MODEL)
    b4 = lambda a: a.reshape(2, bsz, seq, D_MODEL)
    y, sf = pl.pallas_call(
        functools.partial(_rw_scan_kernel, c=c),
        out_shape=(jax.ShapeDtypeStruct((2, bsz, seq, D_MODEL), BF16),
                   jax.ShapeDtypeStruct((bsz, 2) + st_shape, F32)),
        grid=(2, nc),
        in_specs=[shared, shared, shared, perdir, perdir, perdir,
                  pl.BlockSpec((None, c, c), lambda d, j: (d, 0, 0)),
                  pl.BlockSpec((None, c, nrow), lambda d, j: (d, 0, 0)),
                  pl.BlockSpec((None, c, nrow), lambda d, j: (d, 0, 0)),
                  pl.BlockSpec((nrow, RW_GROUP), lambda d, j: (0, 0)),
                  pl.BlockSpec((nrow, nrow), lambda d, j: (0, 0)),
                  pl.BlockSpec((RW_GROUP, RW_GROUP), lambda d, j: (0, 0)),
                  stspec],
        out_specs=(perdir, stspec),
        scratch_shapes=[pltpu.VMEM((bsz,) + st_shape, F32)],
        compiler_params=_cp("parallel", "arbitrary"),
        name=name,
    )(b3(r), b3(kn), b3(v), b4(lw), b4(kd), b4(bb), tri, ms, mi, hmask, bdc, bdm, s0)
    return y.reshape(2, bsz * seq, D_MODEL), sf


def _rw_out_kernel(y0_ref, y1_ref, bonus_ref, gate_ref, lnw_ref, lnb_ref, ed_ref, eu_ref, w_ref, x_ref, g_ref,
                   out_ref):
    y = y0_ref[...].astype(F32) + y1_ref[...].astype(F32)
    ed = ed_ref[...]
    eu = eu_ref[...]
    mu = _seg_sum(y, ed, eu) * (1.0 / RW_HEAD)
    dlt = y - mu
    var = _seg_sum(dlt * dlt, ed, eu) * (1.0 / RW_HEAD)
    yn = dlt * lax.rsqrt(var + RW_GN_EPS) * lnw_ref[...] + lnb_ref[...] + bonus_ref[...].astype(F32)
    out_ref[...] = x_ref[...] + g_ref[...] * _bdot(yn * gate_ref[...].astype(F32), w_ref[...])


def _rw_out(y, bonus, gate_act, ln_w, ln_b, w_out, x2d, gate, *, seq, tm, name):
    m = x2d.shape[0]
    tpb = seq // tm
    e_down, e_up = _seg_mats()
    rowspec = pl.BlockSpec((tm, D_MODEL), lambda i: (i, 0))
    full = lambda a: pl.BlockSpec(a.shape, lambda i: (0,) * a.ndim)
    return pl.pallas_call(
        _rw_out_kernel,
        out_shape=jax.ShapeDtypeStruct((m, D_MODEL), F32),
        grid=(m // tm,),
        in_specs=[pl.BlockSpec((None, tm, D_MODEL), lambda i: (0, i, 0)),
                  pl.BlockSpec((None, tm, D_MODEL), lambda i: (1, i, 0)),
                  rowspec, rowspec, full(ln_w), full(ln_b), full(e_down), full(e_up), full(w_out), rowspec,
                  pl.BlockSpec((None, 1, D_MODEL), lambda i: (i // tpb, 0, 0))],
        out_specs=rowspec,
        compiler_params=_cp("parallel"),
        name=name,
    )(y, y, bonus, gate_act, ln_w, ln_b, e_down, e_up, w_out, x2d, gate)


def _rotate_half_cols(w, n_seg):
    sh = w.shape
    ws = w.reshape(sh[:-1] + (n_seg, 2, sh[-1] // (2 * n_seg)))
    return jnp.concatenate([-ws[..., 1:, :], ws[..., :1, :]], axis=-2).reshape(sh)


def _axial_tables(n_tokens):
    t = jnp.arange(n_tokens)
    row = (t // GRID_W).astype(F32)
    col = (t % GRID_W).astype(F32)
    inv = ROPE_BASE ** (-jnp.arange(0, AXIS_DIM, 2, dtype=F32) / AXIS_DIM)
    ang_r = row[:, None] * inv[None, :]
    ang_c = col[:, None] * inv[None, :]
    ang = jnp.concatenate([ang_r, ang_r, ang_c, ang_c], axis=-1)
    return jnp.tile(jnp.cos(ang), (1, 2)), jnp.tile(jnp.sin(ang), (1, 2))


def _retention_tables(n_tokens):
    t = jnp.arange(n_tokens, dtype=F32)
    inv = ROPE_BASE ** (-jnp.linspace(0.0, 1.0, RT_DK // 2, dtype=F32))
    ang = t[:, None] * inv[None, :]
    ang = jnp.concatenate([ang, ang], axis=-1)
    return jnp.cos(ang), jnp.sin(ang)


def _forward(x, c, ctx, c_ctx, ada_w, ada_b, norm1_g, norm2_g, ffn_w_up, ffn_conv_w, ffn_conv_b, ffn_w_down,
             hg_w_in, hg_lb_logits, hg_norm_g, hg_w_out, at_w_qkv, at_b_qkv, at_sink, at_w_out,
             rt_w_in, rt_decay_exp, rt_w_out, rw_mix, rw_w_rkv, rw_w0, rw_w1, rw_w2, rw_a0, rw_a1, rw_a2,
             rw_g1, rw_g2, rw_k_k, rw_k_a, rw_r_k, rw_ln_w, rw_ln_b, rw_w_out, final_norm_g, depth):
    bsz, seq, d = x.shape
    nctx = ctx.shape[1]
    tm_l = 512 if seq % 512 == 0 else seq
    tm_f = 256 if seq % 256 == 0 else seq
    tm_ffn = 1024 if seq % 1024 == 0 else tm_l
    tm_c = nctx

    cvecs = jnp.zeros((8, d), F32).at[:bsz].set(c).at[bsz].set(c_ctx)
    mod = _adaln(cvecs, ada_w[:depth], ada_b[:depth])
    lb_all = jnp.cumsum(jax.nn.softmax(hg_lb_logits.astype(F32), axis=0), axis=0)
    zero_b = lambda n: jnp.zeros((1, n), F32)

    xl = x.reshape(bsz * seq, d)
    xc = ctx.reshape(bsz * nctx, d)
    for i in range(depth):
        kind, j = i % 4, i // 4
        need_ctx = i < depth - 1
        ml = mod[i, :bsz].reshape(bsz, 6, 1, d)
        mc = jnp.broadcast_to(mod[i, bsz].reshape(1, 6, 1, d), (bsz, 6, 1, d))
        sh1, sc1, g1, sh2, sc2, g2 = (ml[:, n] for n in range(6))
        csh1, csc1, cg1, csh2, csc2, cg2 = (mc[:, n] for n in range(6))
        mul1, cmul1 = norm1_g[i] * (1.0 + sc1), norm1_g[i] * (1.0 + csc1)
        mul2, cmul2 = norm2_g[i] * (1.0 + sc2), norm2_g[i] * (1.0 + csc2)

        if kind == 0:
            w_in = hg_w_in[j].astype(BF16)
            w_out = hg_w_out[j].astype(BF16)
            lb_row = lb_all[i].reshape(1, d)
            ng_row = jnp.tile(hg_norm_g[j], HG_HEADS).reshape(1, d)
            pc, zc = _proj(xc, cmul1, csh1, w_in, zero_b(5 * d), seq=nctx, tm=tm_c, name="hg_proj_ctx",
                           n_f32_tail=2 * d)
            pL, zl = _proj(xl, mul1, sh1, w_in, zero_b(5 * d), seq=seq, tm=tm_l, name="hg_proj", n_f32_tail=2 * d)
            s0 = jnp.zeros((bsz, 2, HG_HEADS, HG_DK, HG_DK), F32)
            oc, s_ctx = _hgrn_scan(pc, zc, lb_row, s0, bsz=bsz, seq=nctx, name="hg_scan_ctx")
            ol, _ = _hgrn_scan(pL, zl, lb_row, s_ctx, bsz=bsz, seq=seq, name="hg_scan")
            xl = _hgrn_out(ol, pL, ng_row, w_out, xl, g1, seq=seq, tm=tm_l, name="hg_out")
            if need_ctx:
                xc = _hgrn_out(oc, pc, ng_row, w_out, xc, cg1, seq=nctx, tm=tm_c, name="hg_out_ctx")
        elif kind == 1:
            wq = at_w_qkv[j]
            bq = at_b_qkv[j]
            qd = AT_HEADS * AT_HD
            wq_h = wq[:, :qd].reshape(d, AT_HEADS, AT_HD)
            wk_h = wq[:, qd:qd + AT_KV_DIM].reshape(d, AT_KV_HEADS, AT_HD)
            bq_h = bq[:qd].reshape(AT_HEADS, AT_HD)
            bk_h = bq[qd:qd + AT_KV_DIM].reshape(AT_KV_HEADS, AT_HD)
            w_ext = jnp.concatenate([wq[:, :qd], _rotate_half_cols(wq_h, 2).reshape(d, qd), wq[:, qd:],
                                     _rotate_half_cols(wk_h, 2).reshape(d, AT_KV_DIM)], axis=1).astype(BF16)
            b_ext = jnp.concatenate([bq[:qd], _rotate_half_cols(bq_h, 2).reshape(qd), bq[qd:],
                                     _rotate_half_cols(bk_h, 2).reshape(AT_KV_DIM)]).reshape(1, -1)
            w_out = at_w_out[j].astype(BF16)
            sink = at_sink[j].astype(F32)
            cos2, sin2 = _axial_tables(seq)
            pc = _proj(xc, cmul1, csh1, w_ext, b_ext, seq=nctx, tm=tm_c, name="at_proj_ctx")
            pL = _proj(xl, mul1, sh1, w_ext, b_ext, seq=seq, tm=tm_l, name="at_proj")
            q_r, k_r = _rope(pL, cos2, sin2, seq=seq, tm=tm_l)
            ol = _attn(sink, q_r, k_r, pL, pc, bsz=bsz, seq=seq, nctx=nctx)
            xl = _mm_res(ol, w_out, xl, g1, seq=seq, tm=tm_l, name="at_out")
            if need_ctx:
                oc = _attn_ctx(sink, pc, bsz=bsz, nctx=nctx)
                xc = _mm_res(oc, w_out, xc, cg1, seq=nctx, tm=tm_c, name="at_out_ctx")
        elif kind == 2:
            w_in = rt_w_in[j].astype(BF16)
            w_out = rt_w_out[j].astype(BF16)
            cos1, sin1 = _retention_tables(seq)
            ones_t, zeros_t = jnp.ones((nctx, RT_DK), F32), jnp.zeros((nctx, RT_DK), F32)
            pc = _proj(xc, cmul1, csh1, w_in, zero_b(6 * d), seq=nctx, tm=tm_c, name="rt_proj_ctx")
            pL = _proj(xl, mul1, sh1, w_in, zero_b(6 * d), seq=seq, tm=tm_l, name="rt_proj")
            s0 = jnp.zeros((bsz, 2, RT_HEADS, RT_DK, RT_DV), F32)
            oc, s_ctx = _ret_scan(pc, ones_t, zeros_t, rt_decay_exp[j], s0, bsz=bsz, seq=nctx, name="rt_scan_ctx")
            ol, _ = _ret_scan(pL, cos1, sin1, rt_decay_exp[j], s_ctx, bsz=bsz, seq=seq, name="rt_scan")
            xl = _ret_out(ol, pL, w_out, xl, g1, seq=seq, tm=tm_l, name="rt_out")
            if need_ctx:
                xc = _ret_out(oc, pc, w_out, xc, cg1, seq=nctx, tm=tm_c, name="rt_out_ctx")
        else:
            wts = dict(mix=rw_mix[j], w_rkv=rw_w_rkv[j].astype(BF16), w0=rw_w0[j].reshape(2, 1, d),
                       w1=rw_w1[j].astype(BF16), w2=rw_w2[j].astype(BF16), a0=rw_a0[j].reshape(2, 1, d),
                       a1=rw_a1[j].astype(BF16), a2=rw_a2[j].astype(BF16), g1=rw_g1[j].astype(BF16),
                       g2=rw_g2[j].astype(BF16), k_k=rw_k_k[j].reshape(1, d), k_a=rw_k_a[j].reshape(1, d),
                       r_k=rw_r_k[j].reshape(1, d))
            w_out = rw_w_out[j].astype(BF16)
            ln_w, ln_b = rw_ln_w[j].reshape(1, d), rw_ln_b[j].reshape(1, d)
            rc = _rw_proj(xc, cmul1, csh1, wts, seq=nctx, width=nctx, tm=tm_c, quarters=('l', 'l', 'r', 'r'),
                          name="rw_proj_ctx")
            rl = _rw_proj(xl, mul1, sh1, wts, seq=seq, width=GRID_W, tm=tm_f, quarters=('l', 'r', 'u', 'd'),
                          name="rw_proj")
            s0 = jnp.zeros((bsz, 2, RW_NGROUPS, RW_GROUP, RW_GROUP), F32)
            scan_in = lambda t: (t[0], t[2], t[1], t[5], t[6], t[7])
            yc, s_ctx = _rw_scan(*scan_in(rc), s0, bsz=bsz, seq=nctx, name="rw_scan_ctx")
            yl, _ = _rw_scan(*scan_in(rl), s_ctx, bsz=bsz, seq=seq, name="rw_scan")
            xl = _rw_out(yl, rl[4], rl[3], ln_w, ln_b, w_out, xl, g1, seq=seq, tm=tm_l, name="rw_out")
            if need_ctx:
                xc = _rw_out(yc, rc[4], rc[3], ln_w, ln_b, w_out, xc, cg1, seq=nctx, tm=tm_c, name="rw_out_ctx")

        w_up = ffn_w_up[i].astype(BF16)
        w_down = ffn_w_down[i].astype(BF16)
        conv_w = ffn_conv_w[i].reshape(9, D_FF)
        conv_b = ffn_conv_b[i].reshape(1, D_FF)
        fin_g = final_norm_g.reshape(1, d)
        xl = _ffn(xl, mul2, sh2, w_up, conv_w, conv_b, w_down, g2, fin_g, seq=seq, width=GRID_W, tm=tm_ffn,
                  final_norm=(i == depth - 1), name="ffn")
        if need_ctx:
            xc = _ffn(xc, cmul2, csh2, w_up, conv_w, conv_b, w_down, cg2, fin_g, seq=nctx, width=nctx, tm=tm_c,
                      final_norm=False, name="ffn_ctx")
    return xl.reshape(bsz, seq, d)


def kernel(x, c, ctx, c_ctx, ada_w, ada_b, norm1_g, norm2_g, ffn_w_up, ffn_conv_w, ffn_conv_b, ffn_w_down, hg_w_in, hg_lb_logits, hg_norm_g, hg_w_out, at_w_qkv, at_b_qkv, at_sink, at_w_out, rt_w_in, rt_decay_exp, rt_w_out, rw_mix, rw_w_rkv, rw_w0, rw_w1, rw_w2, rw_a0, rw_a1, rw_a2, rw_g1, rw_g2, rw_k_k, rw_k_a, rw_r_k, rw_ln_w, rw_ln_b, rw_w_out, final_norm_g):
    return _forward(x, c, ctx, c_ctx, ada_w, ada_b, norm1_g, norm2_g, ffn_w_up, ffn_conv_w, ffn_conv_b, ffn_w_down,
                    hg_w_in, hg_lb_logits, hg_norm_g, hg_w_out, at_w_qkv, at_b_qkv, at_sink, at_w_out,
                    rt_w_in, rt_decay_exp, rt_w_out, rw_mix, rw_w_rkv, rw_w0, rw_w1, rw_w2, rw_a0, rw_a1, rw_a2,
                    rw_g1, rw_g2, rw_k_k, rw_k_a, rw_r_k, rw_ln_w, rw_ln_b, rw_w_out, final_norm_g, DEPTH)
```

```python
import functools
import math

import numpy as np
import jax
import jax.numpy as jnp
from jax import lax
from jax.experimental import pallas as pl
from jax.experimental.pallas import tpu as pltpu

F32 = jnp.float32
BF16 = jnp.bfloat16

D_MODEL = 1024
DEPTH = 4
GRID_W = 64
NORM_EPS = 1e-6

HG_DK = 128
HG_HEADS = D_MODEL // HG_DK
HG_CHUNK = 128

AT_HD = 64
AT_HEADS = D_MODEL // AT_HD
AT_KV_HEADS = AT_HEADS // 4
AT_GROUP = 4
AT_KV_DIM = AT_KV_HEADS * AT_HD
AT_BLOCK = 128
ROPE_BASE = 10000.0
AXIS_DIM = AT_HD // 2

RT_DK = 256
RT_HEADS = D_MODEL // RT_DK
RT_DV = 2 * RT_DK
RT_V_DIM = RT_HEADS * RT_DV
RT_CHUNK = 256

RW_HEAD = 64
RW_HEADS = D_MODEL // RW_HEAD
RW_GN_EPS = 64e-5
RW_CHUNK = 64
RW_GROUP_HEADS = 4
RW_GROUP = RW_GROUP_HEADS * RW_HEAD
RW_NGROUPS = D_MODEL // RW_GROUP

D_FF = 2816
FF_CHUNK = 256
FF_DOWN_PARTS = 2

VMEM_LIMIT = 56 * 1024 * 1024


def _cp(*sem):
    return pltpu.CompilerParams(dimension_semantics=sem, vmem_limit_bytes=VMEM_LIMIT)


def _bdot(a, b):
    return jnp.dot(a.astype(BF16), b.astype(BF16), preferred_element_type=F32)


def _bdot_nt(a, b):
    return lax.dot_general(a.astype(BF16), b.astype(BF16), (((1,), (1,)), ((), ())), preferred_element_type=F32)


def _bdot_tn(a, b):
    return lax.dot_general(a.astype(BF16), b.astype(BF16), (((0,), (0,)), ((), ())), preferred_element_type=F32)


def _split3(x):
    hi = x.astype(BF16)
    r1 = x - hi.astype(F32)
    mid = r1.astype(BF16)
    lo = (r1 - mid.astype(F32)).astype(BF16)
    return hi, mid, lo


def _exact_dot(m_bf16, x):
    hi, mid, lo = _split3(x)
    d = lambda v: jnp.dot(m_bf16, v, preferred_element_type=F32)
    return d(hi) + d(mid) + d(lo)


def _exact_dot2(m_bf16, x):
    hi = x.astype(BF16)
    lo = (x - hi.astype(F32)).astype(BF16)
    return jnp.dot(m_bf16, hi, preferred_element_type=F32) + jnp.dot(m_bf16, lo, preferred_element_type=F32)


def _neg_abs(x):
    bits = lax.bitcast_convert_type(x, jnp.uint32) | jnp.uint32(0x80000000)
    return lax.bitcast_convert_type(bits, F32)


def _sigmoid(x):
    return 1.0 / (1.0 + jnp.exp(-x))


def _silu(x):
    return x * _sigmoid(x)


def _rms(x):
    return x * lax.rsqrt(jnp.mean(x * x, axis=-1, keepdims=True) + NORM_EPS)


def _adaln_kernel(c_ref, w_ref, b_ref, o_ref):
    o_ref[...] = _bdot(_silu(c_ref[...]), w_ref[...]) + b_ref[...]


def _adaln(cvecs, ada_w, ada_b):
    depth, d, n = ada_w.shape
    tn = 1536
    return pl.pallas_call(
        _adaln_kernel,
        out_shape=jax.ShapeDtypeStruct((depth, 8, n), F32),
        grid=(depth, n // tn),
        in_specs=[pl.BlockSpec((8, d), lambda l, j: (0, 0)),
                  pl.BlockSpec((None, d, tn), lambda l, j: (l, 0, j)),
                  pl.BlockSpec((None, 1, tn), lambda l, j: (l, 0, j))],
        out_specs=pl.BlockSpec((None, 8, tn), lambda l, j: (l, 0, j)),
        compiler_params=_cp("parallel", "parallel"),
        name="adaln",
    )(cvecs, ada_w.astype(BF16), ada_b.reshape(depth, 1, n))


def _proj_kernel(x_ref, mul_ref, add_ref, w_ref, b_ref, *out_refs, n_lo, tn):
    h = (_rms(x_ref[...]) * mul_ref[...] + add_ref[...]).astype(BF16)
    for n0 in range(0, w_ref.shape[1], tn):
        y = jnp.dot(h, w_ref[:, n0:n0 + tn], preferred_element_type=F32) + b_ref[:, n0:n0 + tn]
        if n0 < n_lo:
            out_refs[0][:, n0:n0 + tn] = y.astype(BF16)
        else:
            out_refs[1][:, n0 - n_lo:n0 - n_lo + tn] = y


def _proj(x2d, mul, add, w, bias, *, seq, tm, name, n_f32_tail=0):
    m, d = x2d.shape
    n = w.shape[1]
    n_lo = n - n_f32_tail
    tn = 512 if (n % 512 == 0 and n_lo % 512 == 0) else 256
    tpb = seq // tm
    out_shape = [jax.ShapeDtypeStruct((m, n_lo), BF16)]
    out_specs = [pl.BlockSpec((tm, n_lo), lambda i: (i, 0))]
    if n_f32_tail:
        out_shape.append(jax.ShapeDtypeStruct((m, n_f32_tail), F32))
        out_specs.append(pl.BlockSpec((tm, n_f32_tail), lambda i: (i, 0)))
    res = pl.pallas_call(
        functools.partial(_proj_kernel, n_lo=n_lo, tn=tn),
        out_shape=tuple(out_shape),
        grid=(m // tm,),
        in_specs=[pl.BlockSpec((tm, d), lambda i: (i, 0)),
                  pl.BlockSpec((None, 1, d), lambda i: (i // tpb, 0, 0)),
                  pl.BlockSpec((None, 1, d), lambda i: (i // tpb, 0, 0)),
                  pl.BlockSpec((d, n), lambda i: (0, 0)),
                  pl.BlockSpec((1, n), lambda i: (0, 0))],
        out_specs=tuple(out_specs),
        compiler_params=_cp("parallel"),
        name=name,
    )(x2d, mul, add, w, bias)
    return res if n_f32_tail else res[0]


def _mm_res_kernel(a_ref, w_ref, x_ref, g_ref, o_ref):
    o_ref[...] = x_ref[...] + g_ref[...] * _bdot(a_ref[...], w_ref[...])


def _mm_res(act, w, x2d, gate, *, seq, tm, name):
    m, k = act.shape
    d = w.shape[1]
    tpb = seq // tm
    return pl.pallas_call(
        _mm_res_kernel,
        out_shape=jax.ShapeDtypeStruct((m, d), F32),
        grid=(m // tm,),
        in_specs=[pl.BlockSpec((tm, k), lambda i: (i, 0)),
                  pl.BlockSpec((k, d), lambda i: (0, 0)),
                  pl.BlockSpec((tm, d), lambda i: (i, 0)),
                  pl.BlockSpec((None, 1, d), lambda i: (i // tpb, 0, 0))],
        out_specs=pl.BlockSpec((tm, d), lambda i: (i, 0)),
        compiler_params=_cp("parallel"),
        name=name,
    )(act, w, x2d, gate)


def _ffn_kernel(x_ref, xp_ref, xn_ref, mul_ref, add_ref, wu_ref, cw_ref, cb_ref, wd_ref, g_ref, fg_ref, o_ref,
                act_scr, *, tm, width, tpi, has_rows, final_norm):
    i = pl.program_id(0)
    mul = mul_ref[...]
    add = add_ref[...]
    hmod = lambda x: (_rms(x) * mul + add).astype(BF16)
    x = x_ref[...]
    hm = hmod(x)
    if has_rows:
        first = (i % tpi) == 0
        last = (i % tpi) == tpi - 1
        hp = jnp.where(first, jnp.zeros((), BF16), hmod(xp_ref[...]))
        hn = jnp.where(last, jnp.zeros((), BF16), hmod(xn_ref[...]))
        h_ext = jnp.concatenate([hp, hm, hn], axis=0)
    else:
        h_ext = hm
    col = lax.broadcasted_iota(jnp.int32, (tm, 1), 0) & (width - 1)
    not_left = col != 0
    not_right = col != width - 1
    inv_sqrt2 = 1.0 / math.sqrt(2.0)
    def up(c0):
        return (jnp.dot(h_ext, wu_ref[:, c0:c0 + FF_CHUNK], preferred_element_type=F32),
                jnp.dot(hm, wu_ref[:, D_FF + c0:D_FF + c0 + FF_CHUNK], preferred_element_type=F32))

    starts = list(range(0, D_FF, FF_CHUNK))
    cuts = [starts[(len(starts) * n) // FF_DOWN_PARTS] for n in range(1, FF_DOWN_PARTS)]
    down = lambda lo, hi: jnp.dot(act_scr[:, lo:hi], wd_ref[lo:hi, :], preferred_element_type=F32)
    nxt = up(starts[0])
    y_mm = None
    done = 0
    for n_c, c0 in enumerate(starts):
        sl = slice(c0, c0 + FF_CHUNK)
        u_ext, vv = nxt
        if n_c + 1 < len(starts):
            nxt = up(starts[n_c + 1])
        if c0 in cuts:
            part = down(done, c0)
            y_mm = part if y_mm is None else y_mm + part
            done = c0
        if has_rows:
            rows = (u_ext[:tm], u_ext[width:width + tm], u_ext[2 * width:2 * width + tm])
            colsum = [cw_ref[b:b + 1, sl] * rows[0] + cw_ref[3 + b:4 + b, sl] * rows[1]
                      + cw_ref[6 + b:7 + b, sl] * rows[2] for b in range(3)]
        else:
            colsum = [cw_ref[3 + b:4 + b, sl] * u_ext for b in range(3)]
        acc = (colsum[1] + cb_ref[:, sl]
               + jnp.where(not_left, pltpu.roll(colsum[0], 1, 0), 0.0)
               + jnp.where(not_right, pltpu.roll(colsum[2], tm - 1, 0), 0.0))
        gelu = 0.5 * acc * (1.0 + lax.erf(acc * inv_sqrt2))
        act_scr[:, sl] = (gelu * vv).astype(BF16)
    y = x + g_ref[...] * (y_mm + down(done, D_FF))
    if final_norm:
        y = _rms(y) * fg_ref[...]
    o_ref[...] = y


def _ffn(x2d, mul, add, w_up, conv_w, conv_b, w_down, gate, final_g, *, seq, width, tm, final_norm, name):
    m = x2d.shape[0]
    tpi = seq // tm
    rpt = tm // width
    nrow = m // width
    has_rows = seq > width
    kern = functools.partial(_ffn_kernel, tm=tm, width=width, tpi=tpi, has_rows=has_rows, final_norm=final_norm)
    resident = lambda shape: pl.BlockSpec(shape, lambda i: (0,) * len(shape), pipeline_mode=pl.Buffered(1))
    return pl.pallas_call(
        kern,
        out_shape=jax.ShapeDtypeStruct((m, D_MODEL), F32),
        grid=(m // tm,),
        in_specs=[pl.BlockSpec((tm, D_MODEL), lambda i: (i, 0)),
                  pl.BlockSpec((width, D_MODEL), lambda i: (jnp.maximum(i * rpt - 1, 0), 0)),
                  pl.BlockSpec((width, D_MODEL), lambda i: (jnp.minimum((i + 1) * rpt, nrow - 1), 0)),
                  pl.BlockSpec((None, 1, D_MODEL), lambda i: (i // tpi, 0, 0)),
                  pl.BlockSpec((None, 1, D_MODEL), lambda i: (i // tpi, 0, 0)),
                  resident((D_MODEL, 2 * D_FF)),
                  pl.BlockSpec((9, D_FF), lambda i: (0, 0)),
                  pl.BlockSpec((1, D_FF), lambda i: (0, 0)),
                  resident((D_FF, D_MODEL)),
                  pl.BlockSpec((None, 1, D_MODEL), lambda i: (i // tpi, 0, 0)),
                  pl.BlockSpec((1, D_MODEL), lambda i: (0, 0))],
        out_specs=pl.BlockSpec((tm, D_MODEL), lambda i: (i, 0)),
        scratch_shapes=[pltpu.VMEM((tm, D_FF), BF16)],
        compiler_params=_cp("parallel"),
        name=name,
    )(x2d, x2d, x2d, mul, add, w_up, conv_w, conv_b, w_down, gate, final_g)


def _hgrn_levels(c):
    levels = []
    m = c // 2
    while m >= 1:
        levels.append(m)
        m //= 2
    return levels


def _hgrn_tables(c, rev):
    levels = _hgrn_levels(c)
    msk = np.zeros((len(levels) + 1, c, c), np.float32)
    t = np.arange(c)
    for l, m in enumerate(levels):
        blk = t // (2 * m)
        second = (t % (2 * m)) >= m
        msk[l] = (blk[:, None] == blk[None, :]) & second[:, None] & (~second)[None, :]
    msk[len(levels)] = np.eye(c)
    tri = np.tril(np.ones((c, c), np.float32))
    if rev:
        msk = msk[:, ::-1, ::-1]
        tri = tri.T
    return jnp.asarray(tri, BF16), jnp.asarray(np.ascontiguousarray(msk), F32)


def _level_ref(bcum, m, rev, row):
    c = bcum.shape[0]
    r = m if rev else m - 1
    if 2 * m >= 8:
        return jnp.concatenate([jnp.broadcast_to(bcum[s + r:s + r + 1, :], (2 * m, bcum.shape[1]))
                                for s in range(0, c, 2 * m)], axis=0)
    phase = row & (2 * m - 1)
    out = bcum
    for ph in range(2 * m):
        if ph != r:
            out = jnp.where(phase == ph, pltpu.roll(bcum, (ph - r) % c, 0), out)
    return out


def _hgrn_scan_kernel(q_ref, v_ref, z_ref, lb_ref, tri_ref, m_ref, s0_ref, o_ref, sf_ref, st_scr, *, c, rev):
    j = pl.program_id(0)

    @pl.when(j == 0)
    def _():
        st_scr[...] = s0_ref[...]

    levels = _hgrn_levels(c)
    nl = len(levels)
    lb = lb_ref[...]
    tri = tri_ref[...]
    row = lax.broadcasted_iota(jnp.int32, (c, 1), 0)
    sls = [slice(h * HG_DK, (h + 1) * HG_DK) for h in range(HG_HEADS)]
    nbat = q_ref.shape[0]
    items = [(b, h) for b in range(nbat) for h in range(HG_HEADS)]
    q, kin, v, bcum, btot = [], [], [], [], []
    for b in range(nbat):
        z = z_ref[b]
        logf = jnp.log(lb + (1.0 - lb) * _sigmoid(z))
        kin.append((1.0 - lb) * _sigmoid(-z))
        log2f = logf * (1.0 / math.log(2.0))
        bcum.append(_exact_dot2(tri, log2f))
        btot.append(jnp.sum(log2f, axis=0, keepdims=True))
        q.append(q_ref[b].astype(F32))
        v.append(v_ref[b])
    a = [m_ref[nl] * _bdot_nt(q[b][:, sls[h]], kin[b][:, sls[h]]) for b, h in items]
    for l, m in enumerate(levels):
        qf, kf = [], []
        for b in range(nbat):
            f = jnp.exp2(_neg_abs(bcum[b] - _level_ref(bcum[b], m, rev, row)))
            qf.append((q[b] * f).astype(BF16))
            kf.append((kin[b] * f).astype(BF16))
        a = [a[i] + m_ref[l] * _bdot_nt(qf[b][:, sls[h]], kf[b][:, sls[h]]) for i, (b, h) in enumerate(items)]
    qtop = [(q[b] * jnp.exp2(bcum[b])).astype(BF16) for b in range(nbat)]
    kend = [(kin[b] * jnp.exp2(btot[b] - bcum[b])).astype(BF16) for b in range(nbat)]
    dec_tot = [jnp.exp2(btot[b]) for b in range(nbat)]
    for i, (b, h) in enumerate(items):
        sl = sls[h]
        st = st_scr[b, h]
        o_ref[b, :, sl] = (_bdot_nt(qtop[b][:, sl], st) + _bdot(a[i], v[b][:, sl])).astype(o_ref.dtype)
        st_scr[b, h] = st * dec_tot[b][:, sl] + _bdot_tn(v[b][:, sl], kend[b][:, sl])

    @pl.when(j == pl.num_programs(0) - 1)
    def _():
        sf_ref[...] = st_scr[...]


def _hgrn_scan(p, zf, lb_row, s0, *, bsz, seq, name):
    c = min(HG_CHUNK, seq)
    nc = seq // c
    st_shape = (HG_HEADS, HG_DK, HG_DK)
    p3 = p.reshape(bsz, seq, -1)
    z3 = zf.reshape(bsz, seq, -1)
    outs, finals = [], []
    for d in range(2):
        tri, masks = _hgrn_tables(c, rev=bool(d))
        crow = (lambda j: nc - 1 - j) if d else (lambda j: j)
        blk = lambda col, crow=crow: pl.BlockSpec((bsz, c, D_MODEL), lambda j: (0, crow(j), col))
        o, sf = pl.pallas_call(
            functools.partial(_hgrn_scan_kernel, c=c, rev=bool(d)),
            out_shape=(jax.ShapeDtypeStruct((bsz, seq, D_MODEL), BF16),
                       jax.ShapeDtypeStruct((bsz,) + st_shape, F32)),
            grid=(nc,),
            in_specs=[blk(0), blk(1), blk(d),
                      pl.BlockSpec((1, D_MODEL), lambda j: (0, 0)),
                      pl.BlockSpec(tri.shape, lambda j: (0, 0)),
                      pl.BlockSpec(masks.shape, lambda j: (0, 0, 0)),
                      pl.BlockSpec((bsz, None) + st_shape, lambda j, d=d: (0, d, 0, 0, 0))],
            out_specs=(blk(0), pl.BlockSpec((bsz,) + st_shape, lambda j: (0, 0, 0, 0))),
            scratch_shapes=[pltpu.VMEM((bsz,) + st_shape, F32)],
            compiler_params=_cp("arbitrary"),
            name=f"{name}_d{d}",
        )(p3, p3, z3, lb_row, tri, masks, s0)
        outs.append(o.reshape(bsz * seq, D_MODEL))
        finals.append(sf)
    return outs, jnp.stack(finals, axis=1)


def _hgrn_out_kernel(o0_ref, o1_ref, gate_ref, ng_ref, w_ref, x_ref, g_ref, out_ref, act_scr):
    o = o0_ref[...].astype(F32) + o1_ref[...].astype(F32)
    gate = gate_ref[...].astype(F32)
    ng = ng_ref[...]
    for h in range(HG_HEADS):
        sl = slice(h * HG_DK, (h + 1) * HG_DK)
        act_scr[:, sl] = (_rms(o[:, sl]) * ng[:, sl] * _silu(gate[:, sl])).astype(BF16)
    out_ref[...] = x_ref[...] + g_ref[...] * jnp.dot(act_scr[...], w_ref[...], preferred_element_type=F32)


def _hgrn_out(o, p, norm_row, w_out, x2d, gate, *, seq, tm, name):
    m = x2d.shape[0]
    tpb = seq // tm
    return pl.pallas_call(
        _hgrn_out_kernel,
        out_shape=jax.ShapeDtypeStruct((m, D_MODEL), F32),
        grid=(m // tm,),
        in_specs=[pl.BlockSpec((tm, D_MODEL), lambda i: (i, 0)),
                  pl.BlockSpec((tm, D_MODEL), lambda i: (i, 0)),
                  pl.BlockSpec((tm, D_MODEL), lambda i: (i, 2)),
                  pl.BlockSpec((1, D_MODEL), lambda i: (0, 0)),
                  pl.BlockSpec((D_MODEL, D_MODEL), lambda i: (0, 0)),
                  pl.BlockSpec((tm, D_MODEL), lambda i: (i, 0)),
                  pl.BlockSpec((None, 1, D_MODEL), lambda i: (i // tpb, 0, 0))],
        out_specs=pl.BlockSpec((tm, D_MODEL), lambda i: (i, 0)),
        scratch_shapes=[pltpu.VMEM((tm, D_MODEL), BF16)],
        compiler_params=_cp("parallel"),
        name=name,
    )(o[0], o[1], p, norm_row, w_out, x2d, gate)


def _rope_kernel(q_ref, qr_ref, k_ref, kr_ref, cos_ref, sin_ref, qo_ref, ko_ref):
    cos = jnp.concatenate([cos_ref[...]] * (D_MODEL // 128), axis=1)
    sin = jnp.concatenate([sin_ref[...]] * (D_MODEL // 128), axis=1)
    f32 = lambda ref: ref[...].astype(F32)
    qo_ref[...] = ((f32(q_ref) * cos + f32(qr_ref) * sin) * (AT_HD ** -0.5)).astype(BF16)
    ko_ref[...] = (f32(k_ref) * cos[:, :AT_KV_DIM] + f32(kr_ref) * sin[:, :AT_KV_DIM]).astype(BF16)


def _rope(p, cos_t, sin_t, *, seq, tm):
    m = p.shape[0]
    tpb = seq // tm
    return pl.pallas_call(
        _rope_kernel,
        out_shape=(jax.ShapeDtypeStruct((m, D_MODEL), BF16), jax.ShapeDtypeStruct((m, AT_KV_DIM), BF16)),
        grid=(m // tm,),
        in_specs=[pl.BlockSpec((tm, D_MODEL), lambda i: (i, 0)),
                  pl.BlockSpec((tm, D_MODEL), lambda i: (i, 1)),
                  pl.BlockSpec((tm, AT_KV_DIM), lambda i: (i, 8)),
                  pl.BlockSpec((tm, AT_KV_DIM), lambda i: (i, 10)),
                  pl.BlockSpec((tm, 128), lambda i: (i % tpb, 0)),
                  pl.BlockSpec((tm, 128), lambda i: (i % tpb, 0))],
        out_specs=(pl.BlockSpec((tm, D_MODEL), lambda i: (i, 0)),
                   pl.BlockSpec((tm, AT_KV_DIM), lambda i: (i, 0))),
        compiler_params=_cp("parallel"),
        name="at_rope",
    )(p, p, p, p, cos_t, sin_t)


def _sink_attend(qs, keys, vals, mask, sink_ref, o_ref):
    t = qs[0].shape[0]
    hsl = lambda hd: slice(hd * AT_HD, (hd + 1) * AT_HD)
    items = [(n, g) for n in range(len(qs)) for g in range(AT_KV_HEADS)]
    q4 = [jnp.concatenate([qs[n][:, hsl(g * AT_GROUP + r)] for r in range(AT_GROUP)], axis=0) for n, g in items]
    logits = [_bdot_nt(q4[i], keys[n][:, hsl(g)]) for i, (n, g) in enumerate(items)]
    if mask is not None:
        pen = jnp.where(mask, 0.0, -jnp.inf)
        pen4 = jnp.concatenate([pen] * AT_GROUP, axis=0)
        logits = [lg + pen4 for lg in logits]
    s = [jnp.concatenate([jnp.full((t, 1), sink_ref[g * AT_GROUP + r], F32) for r in range(AT_GROUP)], axis=0)
         for n, g in items]
    mx = [jnp.maximum(jnp.max(lg, axis=-1, keepdims=True), s[i]) for i, lg in enumerate(logits)]
    p = [jnp.exp(lg - mx[i]) for i, lg in enumerate(logits)]
    denom = [jnp.sum(p[i], axis=-1, keepdims=True) + jnp.exp(s[i] - mx[i]) for i in range(len(items))]
    o4 = [_bdot(p[i], vals[n][:, hsl(g)]) / denom[i] for i, (n, g) in enumerate(items)]
    for i, (n, g) in enumerate(items):
        for r in range(AT_GROUP):
            o_ref[n, :, hsl(g * AT_GROUP + r)] = o4[i][r * t:(r + 1) * t].astype(o_ref.dtype)


def _attn_kernel(sink_ref, q_ref, kp_ref, kc_ref, kn_ref, vp_ref, vc_ref, vn_ref, ck_ref, cv_ref, o_ref, *, nctx):
    i = pl.program_id(0)
    nb = pl.num_programs(0)
    bs = range(q_ref.shape[0])
    keys = [jnp.concatenate([ck_ref[n], kp_ref[n], kc_ref[n], kn_ref[n]], axis=0) for n in bs]
    vals = [jnp.concatenate([cv_ref[n], vp_ref[n], vc_ref[n], vn_ref[n]], axis=0) for n in bs]
    nk = nctx + 3 * AT_BLOCK
    ti = lax.broadcasted_iota(jnp.int32, (AT_BLOCK, nk), 0)
    kj = lax.broadcasted_iota(jnp.int32, (AT_BLOCK, nk), 1) - nctx
    lo = jnp.where(i > 0, 0, AT_BLOCK)
    hi = jnp.where(i < nb - 1, 3 * AT_BLOCK, 2 * AT_BLOCK)
    rel = kj - ti
    mask = (kj < 0) | ((rel >= 0) & (rel <= 2 * AT_BLOCK) & (kj >= lo) & (kj < hi))
    _sink_attend([q_ref[n] for n in bs], keys, vals, mask, sink_ref, o_ref)


def _attn(sink, q_r, k_r, p, p_ctx, *, bsz, seq, nctx):
    nb = seq // AT_BLOCK
    cur = lambda i: i
    prv = lambda i: jnp.maximum(i - 1, 0)
    nxt = lambda i: jnp.minimum(i + 1, nb - 1)
    kspec = lambda f: pl.BlockSpec((bsz, AT_BLOCK, AT_KV_DIM), lambda i: (0, f(i), 0))
    vspec = lambda f: pl.BlockSpec((bsz, AT_BLOCK, AT_KV_DIM), lambda i: (0, f(i), 9))
    q3 = q_r.reshape(bsz, seq, D_MODEL)
    k3 = k_r.reshape(bsz, seq, AT_KV_DIM)
    p3 = p.reshape(bsz, seq, -1)
    c3 = p_ctx.reshape(bsz, nctx, -1)
    out = pl.pallas_call(
        functools.partial(_attn_kernel, nctx=nctx),
        out_shape=jax.ShapeDtypeStruct((bsz, seq, D_MODEL), BF16),
        grid=(nb,),
        in_specs=[pl.BlockSpec(memory_space=pltpu.SMEM),
                  pl.BlockSpec((bsz, AT_BLOCK, D_MODEL), lambda i: (0, i, 0)),
                  kspec(prv), kspec(cur), kspec(nxt), vspec(prv), vspec(cur), vspec(nxt),
                  pl.BlockSpec((bsz, nctx, AT_KV_DIM), lambda i: (0, 0, 8)),
                  pl.BlockSpec((bsz, nctx, AT_KV_DIM), lambda i: (0, 0, 9))],
        out_specs=pl.BlockSpec((bsz, AT_BLOCK, D_MODEL), lambda i: (0, i, 0)),
        compiler_params=_cp("parallel"),
        name="at_window",
    )(sink, q3, k3, k3, k3, p3, p3, p3, c3, c3)
    return out.reshape(bsz * seq, D_MODEL)


def _attn_ctx_kernel(sink_ref, q_ref, k_ref, v_ref, o_ref):
    _sink_attend([q_ref[0] * (AT_HD ** -0.5)], [k_ref[0]], [v_ref[0]], None, sink_ref, o_ref)


def _attn_ctx(sink, p_ctx, *, bsz, nctx):
    c3 = p_ctx.reshape(bsz, nctx, -1)
    out = pl.pallas_call(
        _attn_ctx_kernel,
        out_shape=jax.ShapeDtypeStruct((bsz, nctx, D_MODEL), BF16),
        grid=(bsz,),
        in_specs=[pl.BlockSpec(memory_space=pltpu.SMEM),
                  pl.BlockSpec((1, nctx, D_MODEL), lambda b: (b, 0, 0)),
                  pl.BlockSpec((1, nctx, AT_KV_DIM), lambda b: (b, 0, 8)),
                  pl.BlockSpec((1, nctx, AT_KV_DIM), lambda b: (b, 0, 9))],
        out_specs=pl.BlockSpec((1, nctx, D_MODEL), lambda b: (b, 0, 0)),
        compiler_params=_cp("parallel"),
        name="at_ctx",
    )(sink, c3, c3, c3)
    return out.reshape(bsz * nctx, D_MODEL)


def _ret_scan_kernel(q_ref, k_ref, v_ref, cos_ref, sin_ref, dm_ref, qin_ref, kout_ref, car_ref, s0_ref,
                     o_ref, sf_ref, st_scr, *, with_intra):
    j = pl.program_id(0)

    @pl.when(j == 0)
    def _():
        st_scr[...] = s0_ref[...]

    cos = cos_ref[...]
    sin = sin_ref[...]
    half = RT_DK // 2

    def rope(x):
        rot = jnp.concatenate([-x[:, half:], x[:, :half]], axis=1)
        return x * cos + rot * sin

    sk = [slice(h * RT_DK, (h + 1) * RT_DK) for h in range(RT_HEADS)]
    sv = [slice(h * RT_DV, (h + 1) * RT_DV) for h in range(RT_HEADS)]
    items = [(b, h) for b in range(q_ref.shape[0]) for h in range(RT_HEADS)]
    qh = [rope(q_ref[b, :, sk[h]].astype(F32)) for b, h in items]
    kh = [rope(k_ref[b, :, sk[h]].astype(F32) * (RT_DK ** -0.5)) for b, h in items]
    vh = [v_ref[b, :, sv[h]] for b, h in items]
    st = [st_scr[b, h] for b, h in items]
    o = [_bdot(qh[i] * qin_ref[:, sk[h]], st[i]) for i, (b, h) in enumerate(items)]
    if with_intra:
        s = [_bdot_nt(qh[i], kh[i]) * dm_ref[h] for i, (b, h) in enumerate(items)]
        o = [o[i] + _bdot(s[i], vh[i]) for i in range(len(items))]
    for i, (b, h) in enumerate(items):
        o_ref[b, :, sv[h]] = o[i].astype(o_ref.dtype)
        st_scr[b, h] = car_ref[:, sv[h]] * st[i] + _bdot_tn(kh[i] * kout_ref[:, sk[h]], vh[i])

    @pl.when(j == pl.num_programs(0) - 1)
    def _():
        sf_ref[...] = st_scr[...]


def _ret_tables(decay_exp, c):
    lg = jnp.log1p(-jnp.exp2(decay_exp.astype(F32)))
    idx = jnp.arange(c, dtype=F32)
    pos = jnp.stack([idx, c - 1.0 - idx])
    rel = pos[:, :, None] - pos[:, None, :]
    lgh = lg[:, :, None, None]
    dm = jnp.where(rel[:, None] >= 0, jnp.exp(lgh * jnp.maximum(rel[:, None], 0.0)), 0.0)
    qin = jnp.exp(lg[:, None, :] * (pos[:, :, None] + 1.0))
    kout = jnp.exp(lg[:, None, :] * (c - 1.0 - pos[:, :, None]))
    car = jnp.exp(lg * c)
    qin = jnp.repeat(qin, RT_DK, axis=-1)
    kout = jnp.repeat(kout, RT_DK, axis=-1)
    car = jnp.repeat(car, RT_DV, axis=-1)[:, None, :]
    return dm, qin, kout, car


def _ret_scan(p, cos_t, sin_t, decay_exp, s0, *, bsz, seq, name):
    c = min(RT_CHUNK, seq)
    nc = seq // c
    dm, qin, kout, car = _ret_tables(decay_exp, c)
    dm_both = dm[0] + dm[1]
    st_shape = (RT_HEADS, RT_DK, RT_DV)
    p3 = p.reshape(bsz, seq, -1)
    outs, finals = [], []
    for d in range(2):
        crow = (lambda j: nc - 1 - j) if d else (lambda j: j)
        o, sf = pl.pallas_call(
            functools.partial(_ret_scan_kernel, with_intra=(d == 0)),
            out_shape=(jax.ShapeDtypeStruct((bsz, seq, RT_V_DIM), BF16),
                       jax.ShapeDtypeStruct((bsz,) + st_shape, F32)),
            grid=(nc,),
            in_specs=[pl.BlockSpec((bsz, c, D_MODEL), lambda j, crow=crow: (0, crow(j), 0)),
                      pl.BlockSpec((bsz, c, D_MODEL), lambda j, crow=crow: (0, crow(j), 1)),
                      pl.BlockSpec((bsz, c, RT_V_DIM), lambda j, crow=crow: (0, crow(j), 1)),
                      pl.BlockSpec((c, RT_DK), lambda j, crow=crow: (crow(j), 0)),
                      pl.BlockSpec((c, RT_DK), lambda j, crow=crow: (crow(j), 0)),
                      pl.BlockSpec((RT_HEADS, c, c), lambda j: (0, 0, 0)),
                      pl.BlockSpec((None, c, D_MODEL), lambda j, d=d: (d, 0, 0)),
                      pl.BlockSpec((None, c, D_MODEL), lambda j, d=d: (d, 0, 0)),
                      pl.BlockSpec((None, 1, RT_V_DIM), lambda j, d=d: (d, 0, 0)),
                      pl.BlockSpec((bsz, None) + st_shape, lambda j, d=d: (0, d, 0, 0, 0))],
            out_specs=(pl.BlockSpec((bsz, c, RT_V_DIM), lambda j, crow=crow: (0, crow(j), 0)),
                       pl.BlockSpec((bsz,) + st_shape, lambda j: (0, 0, 0, 0))),
            scratch_shapes=[pltpu.VMEM((bsz,) + st_shape, F32)],
            compiler_params=_cp("arbitrary"),
            name=f"{name}_d{d}",
        )(p3, p3, p3, cos_t, sin_t, dm_both, qin, kout, car, s0)
        outs.append(o.reshape(bsz * seq, RT_V_DIM))
        finals.append(sf)
    return outs, jnp.stack(finals, axis=1)


def _ret_out_kernel(o0_ref, o1_ref, gate_ref, w_ref, x_ref, g_ref, out_ref, act_scr):
    for h in range(RT_HEADS):
        sv = slice(h * RT_DV, (h + 1) * RT_DV)
        o = o0_ref[:, sv].astype(F32) + o1_ref[:, sv].astype(F32)
        act_scr[:, sv] = (_silu(gate_ref[:, sv].astype(F32)) * _rms(o)).astype(BF16)
    out_ref[...] = x_ref[...] + g_ref[...] * jnp.dot(act_scr[...], w_ref[...], preferred_element_type=F32)


def _ret_out(o, p, w_out, x2d, gate, *, seq, tm, name):
    m = x2d.shape[0]
    tpb = seq // tm
    return pl.pallas_call(
        _ret_out_kernel,
        out_shape=jax.ShapeDtypeStruct((m, D_MODEL), F32),
        grid=(m // tm,),
        in_specs=[pl.BlockSpec((tm, RT_V_DIM), lambda i: (i, 0)),
                  pl.BlockSpec((tm, RT_V_DIM), lambda i: (i, 0)),
                  pl.BlockSpec((tm, RT_V_DIM), lambda i: (i, 2)),
                  pl.BlockSpec((RT_V_DIM, D_MODEL), lambda i: (0, 0)),
                  pl.BlockSpec((tm, D_MODEL), lambda i: (i, 0)),
                  pl.BlockSpec((None, 1, D_MODEL), lambda i: (i // tpb, 0, 0))],
        out_specs=pl.BlockSpec((tm, D_MODEL), lambda i: (i, 0)),
        scratch_shapes=[pltpu.VMEM((tm, RT_V_DIM), BF16)],
        compiler_params=_cp("parallel"),
        name=name,
    )(o[0], o[1], p, w_out, x2d, gate)


def _seg_sum(x, e_down, e_up):
    s = jnp.dot(x.astype(BF16), e_down, preferred_element_type=F32)
    s_hi = s.astype(BF16)
    s_lo = (s - s_hi.astype(F32)).astype(BF16)
    return jnp.dot(s_hi, e_up, preferred_element_type=F32) + jnp.dot(s_lo, e_up, preferred_element_type=F32)


def _rw_proj_kernel(x_ref, xp_ref, xn_ref, mul_ref, add_ref, mix_ref, wrkv_ref, w0_ref, w1_ref, w2_ref,
                    a0_ref, a1_ref, a2_ref, g1_ref, g2_ref, kk_ref, ka_ref, rk_ref, ed_ref, eu_ref,
                    r_out, v_out, kn_out, gate_out, bonus_out, lw_out, kd_out, bb_out,
                    *, tm, width, tpi, quarters):
    i = pl.program_id(0)
    first = (i % tpi) == 0
    last = (i % tpi) == tpi - 1
    mul = mul_ref[...]
    add = add_ref[...]
    hmod = lambda x: _rms(x) * mul + add
    hm = hmod(x_ref[...])
    col = lax.broadcasted_iota(jnp.int32, (tm, 1), 0) & (width - 1)
    srcs = {}
    if 'l' in quarters:
        srcs['l'] = jnp.where(col != 0, pltpu.roll(hm, 1, 0), 0.0)
    if 'r' in quarters:
        srcs['r'] = jnp.where(col != width - 1, pltpu.roll(hm, tm - 1, 0), 0.0)
    if 'u' in quarters:
        hp = jnp.where(first, 0.0, hmod(xp_ref[...]))
        srcs['u'] = jnp.concatenate([hp, hm[: tm - width]], axis=0) if tm > width else hp
    if 'd' in quarters:
        hn = jnp.where(last, 0.0, hmod(xn_ref[...]))
        srcs['d'] = jnp.concatenate([hm[width:], hn], axis=0) if tm > width else hn
    qd = D_MODEL // 4
    shifted = jnp.concatenate([srcs[q][:, n * qd:(n + 1) * qd] for n, q in enumerate(quarters)], axis=1)
    xx = shifted - hm
    mixed = lambda n: hm + xx * mix_ref[n:n + 1, :]
    r = _bdot(mixed(0), wrkv_ref[0])
    k = _bdot(mixed(2), wrkv_ref[1])
    v = _bdot(mixed(3), wrkv_ref[2])
    xw = mixed(1)
    xa = mixed(4)
    ed = ed_ref[...]
    eu = eu_ref[...]
    kkh = k * kk_ref[...]
    nrm = jnp.sqrt(_seg_sum(kkh * kkh, ed, eu))
    kn = kkh / jnp.maximum(nrm, 1e-12)
    ksum = jnp.zeros_like(k)
    for z in range(2):
        w_raw = w0_ref[z] + _bdot(jnp.tanh(_bdot(xw, w1_ref[z])), w2_ref[z])
        lw_out[z] = -math.exp(-0.5) * _sigmoid(w_raw)
        a = _sigmoid(a0_ref[z] + _bdot(_bdot(xa, a1_ref[z]), a2_ref[z]))
        kd = k * (1.0 + (a - 1.0) * ka_ref[...])
        kd_out[z] = kd.astype(BF16)
        bb_out[z] = (kn * a).astype(BF16)
        ksum = ksum + kd
    r_out[...] = r.astype(BF16)
    v_out[...] = v.astype(BF16)
    kn_out[...] = kn.astype(BF16)
    gate_out[...] = _bdot(_sigmoid(_bdot(mixed(5), g1_ref[...])), g2_ref[...]).astype(BF16)
    bonus_out[...] = (_seg_sum(r * ksum * rk_ref[...], ed, eu) * v).astype(BF16)


def _seg_mats():
    hid = np.arange(D_MODEL) // RW_HEAD
    e_down = (hid[:, None] == np.arange(128)[None, :]).astype(np.float32)
    return jnp.asarray(e_down, BF16), jnp.asarray(e_down.T, BF16)


def _rw_proj(x2d, mul, add, wts, *, seq, width, tm, quarters, name):
    m = x2d.shape[0]
    tpi = seq // tm
    rpt = tm // width
    nrow = m // width
    e_down, e_up = _seg_mats()
    full = lambda a: pl.BlockSpec(a.shape, lambda i: (0,) * a.ndim)
    consts = [wts['mix'], wts['w_rkv'], wts['w0'], wts['w1'], wts['w2'], wts['a0'], wts['a1'], wts['a2'],
              wts['g1'], wts['g2'], wts['k_k'], wts['k_a'], wts['r_k'], e_down, e_up]
    kern = functools.partial(_rw_proj_kernel, tm=tm, width=width, tpi=tpi, quarters=quarters)
    one = jax.ShapeDtypeStruct((m, D_MODEL), BF16)
    two = jax.ShapeDtypeStruct((2, m, D_MODEL), BF16)
    two_f32 = jax.ShapeDtypeStruct((2, m, D_MODEL), F32)
    ospec1 = pl.BlockSpec((tm, D_MODEL), lambda i: (i, 0))
    ospec2 = pl.BlockSpec((2, tm, D_MODEL), lambda i: (0, i, 0))
    return pl.pallas_call(
        kern,
        out_shape=(one, one, one, one, one, two_f32, two, two),
        grid=(m // tm,),
        in_specs=[pl.BlockSpec((tm, D_MODEL), lambda i: (i, 0)),
                  pl.BlockSpec((width, D_MODEL), lambda i: (jnp.maximum(i * rpt - 1, 0), 0)),
                  pl.BlockSpec((width, D_MODEL), lambda i: (jnp.minimum((i + 1) * rpt, nrow - 1), 0)),
                  pl.BlockSpec((None, 1, D_MODEL), lambda i: (i // tpi, 0, 0)),
                  pl.BlockSpec((None, 1, D_MODEL), lambda i: (i // tpi, 0, 0))] + [full(a) for a in consts],
        out_specs=(ospec1, ospec1, ospec1, ospec1, ospec1, ospec2, ospec2, ospec2),
        compiler_params=_cp("parallel"),
        name=name,
    )(x2d, x2d, x2d, mul, add, *consts)


def _rw_tables(c):
    t = np.arange(c)[:, None]
    s = np.arange(c * RW_GROUP_HEADS)[None, :] % c
    ms = np.stack([s < t, s > t]).astype(np.float32)
    mi = np.stack([s <= t, s >= t]).astype(np.float32)
    tri0 = np.tril(np.ones((c, c)))
    tri = np.stack([tri0, tri0.T])
    hrow = np.arange(c * RW_GROUP_HEADS) // c
    lane_h = np.arange(RW_GROUP) // RW_HEAD
    hmask = hrow[:, None] == lane_h[None, :]
    bdc = hrow[:, None] == hrow[None, :]
    bdm = lane_h[:, None] == lane_h[None, :]
    return (jnp.asarray(tri, BF16), jnp.asarray(ms), jnp.asarray(mi), jnp.asarray(hmask.astype(np.float32), BF16),
            jnp.asarray(bdc.astype(np.float32), BF16), jnp.asarray(bdm.astype(np.float32)))


def _rw_scan_kernel(r_ref, kn_ref, v_ref, lw_ref, kd_ref, bb_ref, tri_ref, ms_ref, mi_ref, hm_ref, bdc_ref, bdm_ref,
                    s0_ref, y_ref, sf_ref, st_scr, *, c):
    j = pl.program_id(1)

    @pl.when(j == 0)
    def _():
        st_scr[...] = s0_ref[...]

    hm = hm_ref[...]
    ms = ms_ref[...]
    mi = mi_ref[...]
    bdc = bdc_ref[...]
    bdm = bdm_ref[...]
    tri = tri_ref[...]
    ncat = c * RW_GROUP_HEADS
    eye = (lax.broadcasted_iota(jnp.int32, (c, ncat), 0)
           == (lax.broadcasted_iota(jnp.int32, (c, ncat), 1) & (c - 1))).astype(F32)
    tile = lambda x: jnp.concatenate([x] * RW_GROUP_HEADS, axis=0)
    stacked = lambda x: tile(x) * hm
    blockdiag = lambda x: tile(x) * bdc
    rows = lambda a, b: jnp.concatenate([a, b], axis=0)
    nsq = int(math.log2(c)) - 1
    b16 = lambda x: x.astype(BF16)
    dot = lambda a, b: jnp.dot(a, b, preferred_element_type=F32)
    dot_nt = lambda a, b: lax.dot_general(a, b, (((1,), (1,)), ((), ())), preferred_element_type=F32)
    sls = [slice(gi * RW_GROUP, (gi + 1) * RW_GROUP) for gi in range(RW_NGROUPS)]

    pre = []
    for b in range(r_ref.shape[0]):
        lw = lw_ref[b]
        g = _exact_dot(tri, lw)
        gtot = jnp.sum(lw, axis=0, keepdims=True)
        e_inv = jnp.exp(-g)
        e_end = jnp.exp(gtot - g)
        kd = kd_ref[b].astype(F32)
        bb = bb_ref[b].astype(F32)
        pre.append(dict(kt=b16(kn_ref[b].astype(F32) * jnp.exp(g - lw)), rt=b16(r_ref[b].astype(F32) * jnp.exp(g)),
                        kh=b16(kd * e_inv), bh=b16(bb * e_inv), ke=b16(kd * e_end), be=b16(bb * e_end),
                        v=v_ref[b], dec=jnp.exp(gtot)))

    items = [(b, gi) for b in range(len(pre)) for gi in range(RW_NGROUPS)]
    ids = range(len(items))
    part = lambda name: [pre[b][name][:, sls[gi]] for b, gi in items]
    kr = [rows(a, b) for a, b in zip(part('kt'), part('rt'))]
    yb = [stacked(x) for x in part('bh')]
    yk = [stacked(x) for x in part('kh')]
    vm = [stacked(x) for x in part('v')]
    by_b = [dot_nt(kr[i], yb[i]) for i in ids]
    by_k = [dot_nt(kr[i], yk[i]) for i in ids]
    n = [-(by_b[i][:c] * ms) for i in ids]
    m_b = [b16(by_b[i][c:] * mi) for i in ids]
    lm_k = [rows(b16(by_k[i][:c] * ms), b16(by_k[i][c:] * mi)) for i in ids]
    p = [eye + n[i] for i in ids]
    nb = [b16(n[i]) for i in ids]
    n = [dot(nb[i], blockdiag(nb[i])) for i in ids]
    for lvl in range(nsq):
        nb = [b16(n[i]) for i in ids]
        nbd = [blockdiag(nb[i]) for i in ids]
        if lvl + 1 < nsq:
            both = [dot(rows(nb[i], b16(p[i])), nbd[i]) for i in ids]
            n = [both[i][:c] for i in ids]
            p = [p[i] + both[i][c:] for i in ids]
        else:
            p = [p[i] + dot(b16(p[i]), nbd[i]) for i in ids]
    st = [st_scr[b, gi] for b, gi in items]
    by_s = [dot_nt(kr[i], b16(st[i])) for i in ids]
    by_v = [dot(lm_k[i], vm[i]) for i in ids]
    u = [dot(b16(p[i]), stacked(b16(by_s[i][:c] + by_v[i][:c]))) for i in ids]
    y = [by_s[i][c:] + by_v[i][c:] - dot(m_b[i], stacked(b16(u[i]))) for i in ids]
    for i, (b, gi) in enumerate(items):
        sl = sls[gi]
        y_ref[b, :, sl] = y[i].astype(y_ref.dtype)
        lhs = rows(pre[b]['v'][:, sl], b16(-u[i]))
        rhs_s = rows(pre[b]['ke'][:, sl], pre[b]['be'][:, sl])
        st_scr[b, gi] = st[i] * pre[b]['dec'][:, sl] + bdm * _bdot_tn(lhs, rhs_s)

    @pl.when(j == pl.num_programs(1) - 1)
    def _():
        sf_ref[...] = st_scr[...]


def _rw_scan(r, kn, v, lw, kd, bb, s0, *, bsz, seq, name):
    c = min(RW_CHUNK, seq)
    nc = seq // c
    tri, ms, mi, hmask, bdc, bdm = _rw_tables(c)
    crow = lambda d, j: j + d * (nc - 1 - 2 * j)
    st_shape = (RW_NGROUPS, RW_GROUP, RW_GROUP)
    shared = pl.BlockSpec((bsz, c, D_MODEL), lambda d, j: (0, crow(d, j), 0))
    perdir = pl.BlockSpec((None, bsz, c, D_MODEL), lambda d, j: (d, 0, crow(d, j), 0))
    stspec = pl.BlockSpec((bsz, None) + st_shape, lambda d, j: (0, d, 0, 0, 0))
    nrow = c * RW_GROUP_HEADS
    b3 = lambda a: a.reshape(bsz, seq, D_MODEL)
    b4 = lambda a: a.reshape(2, bsz, seq, D_MODEL)
    y, sf = pl.pallas_call(
        functools.partial(_rw_scan_kernel, c=c),
        out_shape=(jax.ShapeDtypeStruct((2, bsz, seq, D_MODEL), BF16),
                   jax.ShapeDtypeStruct((bsz, 2) + st_shape, F32)),
        grid=(2, nc),
        in_specs=[shared, shared, shared, perdir, perdir, perdir,
                  pl.BlockSpec((None, c, c), lambda d, j: (d, 0, 0)),
                  pl.BlockSpec((None, c, nrow), lambda d, j: (d, 0, 0)),
                  pl.BlockSpec((None, c, nrow), lambda d, j: (d, 0, 0)),
                  pl.BlockSpec((nrow, RW_GROUP), lambda d, j: (0, 0)),
                  pl.BlockSpec((nrow, nrow), lambda d, j: (0, 0)),
                  pl.BlockSpec((RW_GROUP, RW_GROUP), lambda d, j: (0, 0)),
                  stspec],
        out_specs=(perdir, stspec),
        scratch_shapes=[pltpu.VMEM((bsz,) + st_shape, F32)],
        compiler_params=_cp("parallel", "arbitrary"),
        name=name,
    )(b3(r), b3(kn), b3(v), b4(lw), b4(kd), b4(bb), tri, ms, mi, hmask, bdc, bdm, s0)
    return y.reshape(2, bsz * seq, D_MODEL), sf


def _rw_out_kernel(y0_ref, y1_ref, bonus_ref, gate_ref, lnw_ref, lnb_ref, ed_ref, eu_ref, w_ref, x_ref, g_ref,
                   out_ref):
    y = y0_ref[...].astype(F32) + y1_ref[...].astype(F32)
    ed = ed_ref[...]
    eu = eu_ref[...]
    mu = _seg_sum(y, ed, eu) * (1.0 / RW_HEAD)
    dlt = y - mu
    var = _seg_sum(dlt * dlt, ed, eu) * (1.0 / RW_HEAD)
    yn = dlt * lax.rsqrt(var + RW_GN_EPS) * lnw_ref[...] + lnb_ref[...] + bonus_ref[...].astype(F32)
    out_ref[...] = x_ref[...] + g_ref[...] * _bdot(yn * gate_ref[...].astype(F32), w_ref[...])


def _rw_out(y, bonus, gate_act, ln_w, ln_b, w_out, x2d, gate, *, seq, tm, name):
    m = x2d.shape[0]
    tpb = seq // tm
    e_down, e_up = _seg_mats()
    rowspec = pl.BlockSpec((tm, D_MODEL), lambda i: (i, 0))
    full = lambda a: pl.BlockSpec(a.shape, lambda i: (0,) * a.ndim)
    return pl.pallas_call(
        _rw_out_kernel,
        out_shape=jax.ShapeDtypeStruct((m, D_MODEL), F32),
        grid=(m // tm,),
        in_specs=[pl.BlockSpec((None, tm, D_MODEL), lambda i: (0, i, 0)),
                  pl.BlockSpec((None, tm, D_MODEL), lambda i: (1, i, 0)),
                  rowspec, rowspec, full(ln_w), full(ln_b), full(e_down), full(e_up), full(w_out), rowspec,
                  pl.BlockSpec((None, 1, D_MODEL), lambda i: (i // tpb, 0, 0))],
        out_specs=rowspec,
        compiler_params=_cp("parallel"),
        name=name,
    )(y, y, bonus, gate_act, ln_w, ln_b, e_down, e_up, w_out, x2d, gate)


def _rotate_half_cols(w, n_seg):
    sh = w.shape
    ws = w.reshape(sh[:-1] + (n_seg, 2, sh[-1] // (2 * n_seg)))
    return jnp.concatenate([-ws[..., 1:, :], ws[..., :1, :]], axis=-2).reshape(sh)


def _axial_tables(n_tokens):
    t = jnp.arange(n_tokens)
    row = (t // GRID_W).astype(F32)
    col = (t % GRID_W).astype(F32)
    inv = ROPE_BASE ** (-jnp.arange(0, AXIS_DIM, 2, dtype=F32) / AXIS_DIM)
    ang_r = row[:, None] * inv[None, :]
    ang_c = col[:, None] * inv[None, :]
    ang = jnp.concatenate([ang_r, ang_r, ang_c, ang_c], axis=-1)
    return jnp.tile(jnp.cos(ang), (1, 2)), jnp.tile(jnp.sin(ang), (1, 2))


def _retention_tables(n_tokens):
    t = jnp.arange(n_tokens, dtype=F32)
    inv = ROPE_BASE ** (-jnp.linspace(0.0, 1.0, RT_DK // 2, dtype=F32))
    ang = t[:, None] * inv[None, :]
    ang = jnp.concatenate([ang, ang], axis=-1)
    return jnp.cos(ang), jnp.sin(ang)


def _forward(x, c, ctx, c_ctx, ada_w, ada_b, norm1_g, norm2_g, ffn_w_up, ffn_conv_w, ffn_conv_b, ffn_w_down,
             hg_w_in, hg_lb_logits, hg_norm_g, hg_w_out, at_w_qkv, at_b_qkv, at_sink, at_w_out,
             rt_w_in, rt_decay_exp, rt_w_out, rw_mix, rw_w_rkv, rw_w0, rw_w1, rw_w2, rw_a0, rw_a1, rw_a2,
             rw_g1, rw_g2, rw_k_k, rw_k_a, rw_r_k, rw_ln_w, rw_ln_b, rw_w_out, final_norm_g, depth):
    bsz, seq, d = x.shape
    nctx = ctx.shape[1]
    tm_l = 512 if seq % 512 == 0 else seq
    tm_f = 256 if seq % 256 == 0 else seq
    tm_ffn = 1024 if seq % 1024 == 0 else tm_l
    tm_c = nctx

    cvecs = jnp.zeros((8, d), F32).at[:bsz].set(c).at[bsz].set(c_ctx)
    mod = _adaln(cvecs, ada_w[:depth], ada_b[:depth])
    lb_all = jnp.cumsum(jax.nn.softmax(hg_lb_logits.astype(F32), axis=0), axis=0)
    zero_b = lambda n: jnp.zeros((1, n), F32)

    xl = x.reshape(bsz * seq, d)
    xc = ctx.reshape(bsz * nctx, d)
    for i in range(depth):
        kind, j = i % 4, i // 4
        need_ctx = i < depth - 1
        ml = mod[i, :bsz].reshape(bsz, 6, 1, d)
        mc = jnp.broadcast_to(mod[i, bsz].reshape(1, 6, 1, d), (bsz, 6, 1, d))
        sh1, sc1, g1, sh2, sc2, g2 = (ml[:, n] for n in range(6))
        csh1, csc1, cg1, csh2, csc2, cg2 = (mc[:, n] for n in range(6))
        mul1, cmul1 = norm1_g[i] * (1.0 + sc1), norm1_g[i] * (1.0 + csc1)
        mul2, cmul2 = norm2_g[i] * (1.0 + sc2), norm2_g[i] * (1.0 + csc2)

        if kind == 0:
            w_in = hg_w_in[j].astype(BF16)
            w_out = hg_w_out[j].astype(BF16)
            lb_row = lb_all[i].reshape(1, d)
            ng_row = jnp.tile(hg_norm_g[j], HG_HEADS).reshape(1, d)
            pc, zc = _proj(xc, cmul1, csh1, w_in, zero_b(5 * d), seq=nctx, tm=tm_c, name="hg_proj_ctx",
                           n_f32_tail=2 * d)
            pL, zl = _proj(xl, mul1, sh1, w_in, zero_b(5 * d), seq=seq, tm=tm_l, name="hg_proj", n_f32_tail=2 * d)
            s0 = jnp.zeros((bsz, 2, HG_HEADS, HG_DK, HG_DK), F32)
            oc, s_ctx = _hgrn_scan(pc, zc, lb_row, s0, bsz=bsz, seq=nctx, name="hg_scan_ctx")
            ol, _ = _hgrn_scan(pL, zl, lb_row, s_ctx, bsz=bsz, seq=seq, name="hg_scan")
            xl = _hgrn_out(ol, pL, ng_row, w_out, xl, g1, seq=seq, tm=tm_l, name="hg_out")
            if need_ctx:
                xc = _hgrn_out(oc, pc, ng_row, w_out, xc, cg1, seq=nctx, tm=tm_c, name="hg_out_ctx")
        elif kind == 1:
            wq = at_w_qkv[j]
            bq = at_b_qkv[j]
            qd = AT_HEADS * AT_HD
            wq_h = wq[:, :qd].reshape(d, AT_HEADS, AT_HD)
            wk_h = wq[:, qd:qd + AT_KV_DIM].reshape(d, AT_KV_HEADS, AT_HD)
            bq_h = bq[:qd].reshape(AT_HEADS, AT_HD)
            bk_h = bq[qd:qd + AT_KV_DIM].reshape(AT_KV_HEADS, AT_HD)
            w_ext = jnp.concatenate([wq[:, :qd], _rotate_half_cols(wq_h, 2).reshape(d, qd), wq[:, qd:],
                                     _rotate_half_cols(wk_h, 2).reshape(d, AT_KV_DIM)], axis=1).astype(BF16)
            b_ext = jnp.concatenate([bq[:qd], _rotate_half_cols(bq_h, 2).reshape(qd), bq[qd:],
                                     _rotate_half_cols(bk_h, 2).reshape(AT_KV_DIM)]).reshape(1, -1)
            w_out = at_w_out[j].astype(BF16)
            sink = at_sink[j].astype(F32)
            cos2, sin2 = _axial_tables(seq)
            pc = _proj(xc, cmul1, csh1, w_ext, b_ext, seq=nctx, tm=tm_c, name="at_proj_ctx")
            pL = _proj(xl, mul1, sh1, w_ext, b_ext, seq=seq, tm=tm_l, name="at_proj")
            q_r, k_r = _rope(pL, cos2, sin2, seq=seq, tm=tm_l)
            ol = _attn(sink, q_r, k_r, pL, pc, bsz=bsz, seq=seq, nctx=nctx)
            xl = _mm_res(ol, w_out, xl, g1, seq=seq, tm=tm_l, name="at_out")
            if need_ctx:
                oc = _attn_ctx(sink, pc, bsz=bsz, nctx=nctx)
                xc = _mm_res(oc, w_out, xc, cg1, seq=nctx, tm=tm_c, name="at_out_ctx")
        elif kind == 2:
            w_in = rt_w_in[j].astype(BF16)
            w_out = rt_w_out[j].astype(BF16)
            cos1, sin1 = _retention_tables(seq)
            ones_t, zeros_t = jnp.ones((nctx, RT_DK), F32), jnp.zeros((nctx, RT_DK), F32)
            pc = _proj(xc, cmul1, csh1, w_in, zero_b(6 * d), seq=nctx, tm=tm_c, name="rt_proj_ctx")
            pL = _proj(xl, mul1, sh1, w_in, zero_b(6 * d), seq=seq, tm=tm_l, name="rt_proj")
            s0 = jnp.zeros((bsz, 2, RT_HEADS, RT_DK, RT_DV), F32)
            oc, s_ctx = _ret_scan(pc, ones_t, zeros_t, rt_decay_exp[j], s0, bsz=bsz, seq=nctx, name="rt_scan_ctx")
            ol, _ = _ret_scan(pL, cos1, sin1, rt_decay_exp[j], s_ctx, bsz=bsz, seq=seq, name="rt_scan")
            xl = _ret_out(ol, pL, w_out, xl, g1, seq=seq, tm=tm_l, name="rt_out")
            if need_ctx:
                xc = _ret_out(oc, pc, w_out, xc, cg1, seq=nctx, tm=tm_c, name="rt_out_ctx")
        else:
            wts = dict(mix=rw_mix[j], w_rkv=rw_w_rkv[j].astype(BF16), w0=rw_w0[j].reshape(2, 1, d),
                       w1=rw_w1[j].astype(BF16), w2=rw_w2[j].astype(BF16), a0=rw_a0[j].reshape(2, 1, d),
                       a1=rw_a1[j].astype(BF16), a2=rw_a2[j].astype(BF16), g1=rw_g1[j].astype(BF16),
                       g2=rw_g2[j].astype(BF16), k_k=rw_k_k[j].reshape(1, d), k_a=rw_k_a[j].reshape(1, d),
                       r_k=rw_r_k[j].reshape(1, d))
            w_out = rw_w_out[j].astype(BF16)
            ln_w, ln_b = rw_ln_w[j].reshape(1, d), rw_ln_b[j].reshape(1, d)
            rc = _rw_proj(xc, cmul1, csh1, wts, seq=nctx, width=nctx, tm=tm_c, quarters=('l', 'l', 'r', 'r'),
                          name="rw_proj_ctx")
            rl = _rw_proj(xl, mul1, sh1, wts, seq=seq, width=GRID_W, tm=tm_f, quarters=('l', 'r', 'u', 'd'),
                          name="rw_proj")
            s0 = jnp.zeros((bsz, 2, RW_NGROUPS, RW_GROUP, RW_GROUP), F32)
            scan_in = lambda t: (t[0], t[2], t[1], t[5], t[6], t[7])
            yc, s_ctx = _rw_scan(*scan_in(rc), s0, bsz=bsz, seq=nctx, name="rw_scan_ctx")
            yl, _ = _rw_scan(*scan_in(rl), s_ctx, bsz=bsz, seq=seq, name="rw_scan")
            xl = _rw_out(yl, rl[4], rl[3], ln_w, ln_b, w_out, xl, g1, seq=seq, tm=tm_l, name="rw_out")
            if need_ctx:
                xc = _rw_out(yc, rc[4], rc[3], ln_w, ln_b, w_out, xc, cg1, seq=nctx, tm=tm_c, name="rw_out_ctx")

        w_up = ffn_w_up[i].astype(BF16)
        w_down = ffn_w_down[i].astype(BF16)
        conv_w = ffn_conv_w[i].reshape(9, D_FF)
        conv_b = ffn_conv_b[i].reshape(1, D_FF)
        fin_g = final_norm_g.reshape(1, d)
        xl = _ffn(xl, mul2, sh2, w_up, conv_w, conv_b, w_down, g2, fin_g, seq=seq, width=GRID_W, tm=tm_ffn,
                  final_norm=(i == depth - 1), name="ffn")
        if need_ctx:
            xc = _ffn(xc, cmul2, csh2, w_up, conv_w, conv_b, w_down, cg2, fin_g, seq=nctx, width=nctx, tm=tm_c,
                      final_norm=False, name="ffn_ctx")
    return xl.reshape(bsz, seq, d)


def kernel(x, c, ctx, c_ctx, ada_w, ada_b, norm1_g, norm2_g, ffn_w_up, ffn_conv_w, ffn_conv_b, ffn_w_down, hg_w_in, hg_lb_logits, hg_norm_g, hg_w_out, at_w_qkv, at_b_qkv, at_sink, at_w_out, rt_w_in, rt_decay_exp, rt_w_out, rw_mix, rw_w_rkv, rw_w0, rw_w1, rw_w2, rw_a0, rw_a1, rw_a2, rw_g1, rw_g2, rw_k_k, rw_k_a, rw_r_k, rw_ln_w, rw_ln_b, rw_w_out, final_norm_g):
    return _forward(x, c, ctx, c_ctx, ada_w, ada_b, norm1_g, norm2_g, ffn_w_up, ffn_conv_w, ffn_conv_b, ffn_w_down,
                    hg_w_in, hg_lb_logits, hg_norm_g, hg_w_out, at_w_qkv, at_b_qkv, at_sink, at_w_out,
                    rt_w_in, rt_decay_exp, rt_w_out, rw_mix, rw_w_rkv, rw_w0, rw_w1, rw_w2, rw_a0, rw_a1, rw_a2,
                    rw_g1, rw_g2, rw_k_k, rw_k_a, rw_r_k, rw_ln_w, rw_ln_b, rw_w_out, final_norm_g, DEPTH)
```

```python
import functools
import math

import numpy as np
import jax
import jax.numpy as jnp
from jax import lax
from jax.experimental import pallas as pl
from jax.experimental.pallas import tpu as pltpu

F32 = jnp.float32
BF16 = jnp.bfloat16

D_MODEL = 1024
DEPTH = 4
GRID_W = 64
NORM_EPS = 1e-6

HG_DK = 128
HG_HEADS = D_MODEL // HG_DK
HG_CHUNK = 128

AT_HD = 64
AT_HEADS = D_MODEL // AT_HD
AT_KV_HEADS = AT_HEADS // 4
AT_GROUP = 4
AT_KV_DIM = AT_KV_HEADS * AT_HD
AT_BLOCK = 128
ROPE_BASE = 10000.0
AXIS_DIM = AT_HD // 2

RT_DK = 256
RT_HEADS = D_MODEL // RT_DK
RT_DV = 2 * RT_DK
RT_V_DIM = RT_HEADS * RT_DV
RT_CHUNK = 256

RW_HEAD = 64
RW_HEADS = D_MODEL // RW_HEAD
RW_GN_EPS = 64e-5
RW_CHUNK = 64
RW_GROUP_HEADS = 4
RW_GROUP = RW_GROUP_HEADS * RW_HEAD
RW_NGROUPS = D_MODEL // RW_GROUP

D_FF = 2816
FF_CHUNK = 256
FF_DOWN_PARTS = 2

VMEM_LIMIT = 56 * 1024 * 1024


def _cp(*sem):
    return pltpu.CompilerParams(dimension_semantics=sem, vmem_limit_bytes=VMEM_LIMIT)


def _bdot(a, b):
    return jnp.dot(a.astype(BF16), b.astype(BF16), preferred_element_type=F32)


def _bdot_nt(a, b):
    return lax.dot_general(a.astype(BF16), b.astype(BF16), (((1,), (1,)), ((), ())), preferred_element_type=F32)


def _bdot_tn(a, b):
    return lax.dot_general(a.astype(BF16), b.astype(BF16), (((0,), (0,)), ((), ())), preferred_element_type=F32)


def _split3(x):
    hi = x.astype(BF16)
    r1 = x - hi.astype(F32)
    mid = r1.astype(BF16)
    lo = (r1 - mid.astype(F32)).astype(BF16)
    return hi, mid, lo


def _exact_dot(m_bf16, x):
    hi, mid, lo = _split3(x)
    d = lambda v: jnp.dot(m_bf16, v, preferred_element_type=F32)
    return d(hi) + d(mid) + d(lo)


def _exact_dot2(m_bf16, x):
    hi = x.astype(BF16)
    lo = (x - hi.astype(F32)).astype(BF16)
    return jnp.dot(m_bf16, hi, preferred_element_type=F32) + jnp.dot(m_bf16, lo, preferred_element_type=F32)


def _neg_abs(x):
    bits = lax.bitcast_convert_type(x, jnp.uint32) | jnp.uint32(0x80000000)
    return lax.bitcast_convert_type(bits, F32)


def _sigmoid(x):
    return 1.0 / (1.0 + jnp.exp(-x))


def _silu(x):
    return x * _sigmoid(x)


def _rms(x):
    return x * lax.rsqrt(jnp.mean(x * x, axis=-1, keepdims=True) + NORM_EPS)


def _adaln_kernel(c_ref, w_ref, b_ref, o_ref):
    o_ref[...] = _bdot(_silu(c_ref[...]), w_ref[...]) + b_ref[...]


def _adaln(cvecs, ada_w, ada_b):
    depth, d, n = ada_w.shape
    tn = 1536
    return pl.pallas_call(
        _adaln_kernel,
        out_shape=jax.ShapeDtypeStruct((depth, 8, n), F32),
        grid=(depth, n // tn),
        in_specs=[pl.BlockSpec((8, d), lambda l, j: (0, 0)),
                  pl.BlockSpec((None, d, tn), lambda l, j: (l, 0, j)),
                  pl.BlockSpec((None, 1, tn), lambda l, j: (l, 0, j))],
        out_specs=pl.BlockSpec((None, 8, tn), lambda l, j: (l, 0, j)),
        compiler_params=_cp("parallel", "parallel"),
        name="adaln",
    )(cvecs, ada_w, ada_b.reshape(depth, 1, n))


def _proj_kernel(x_ref, mul_ref, add_ref, w_ref, b_ref, *out_refs, n_lo, tn):
    h = (_rms(x_ref[...]) * mul_ref[...] + add_ref[...]).astype(BF16)
    for n0 in range(0, w_ref.shape[1], tn):
        y = jnp.dot(h, w_ref[:, n0:n0 + tn], preferred_element_type=F32) + b_ref[:, n0:n0 + tn]
        if n0 < n_lo:
            out_refs[0][:, n0:n0 + tn] = y.astype(BF16)
        else:
            out_refs[1][:, n0 - n_lo:n0 - n_lo + tn] = y


def _proj(x2d, mul, add, w, bias, *, seq, tm, name, n_f32_tail=0):
    m, d = x2d.shape
    n = w.shape[1]
    n_lo = n - n_f32_tail
    tn = 512 if (n % 512 == 0 and n_lo % 512 == 0) else 256
    tpb = seq // tm
    out_shape = [jax.ShapeDtypeStruct((m, n_lo), BF16)]
    out_specs = [pl.BlockSpec((tm, n_lo), lambda i: (i, 0))]
    if n_f32_tail:
        out_shape.append(jax.ShapeDtypeStruct((m, n_f32_tail), F32))
        out_specs.append(pl.BlockSpec((tm, n_f32_tail), lambda i: (i, 0)))
    res = pl.pallas_call(
        functools.partial(_proj_kernel, n_lo=n_lo, tn=tn),
        out_shape=tuple(out_shape),
        grid=(m // tm,),
        in_specs=[pl.BlockSpec((tm, d), lambda i: (i, 0)),
                  pl.BlockSpec((None, 1, d), lambda i: (i // tpb, 0, 0)),
                  pl.BlockSpec((None, 1, d), lambda i: (i // tpb, 0, 0)),
                  pl.BlockSpec((d, n), lambda i: (0, 0)),
                  pl.BlockSpec((1, n), lambda i: (0, 0))],
        out_specs=tuple(out_specs),
        compiler_params=_cp("parallel"),
        name=name,
    )(x2d, mul, add, w, bias)
    return res if n_f32_tail else res[0]


def _mm_res_kernel(a_ref, w_ref, x_ref, g_ref, o_ref):
    o_ref[...] = x_ref[...] + g_ref[...] * _bdot(a_ref[...], w_ref[...])


def _mm_res(act, w, x2d, gate, *, seq, tm, name):
    m, k = act.shape
    d = w.shape[1]
    tpb = seq // tm
    return pl.pallas_call(
        _mm_res_kernel,
        out_shape=jax.ShapeDtypeStruct((m, d), F32),
        grid=(m // tm,),
        in_specs=[pl.BlockSpec((tm, k), lambda i: (i, 0)),
                  pl.BlockSpec((k, d), lambda i: (0, 0)),
                  pl.BlockSpec((tm, d), lambda i: (i, 0)),
                  pl.BlockSpec((None, 1, d), lambda i: (i // tpb, 0, 0))],
        out_specs=pl.BlockSpec((tm, d), lambda i: (i, 0)),
        compiler_params=_cp("parallel"),
        name=name,
    )(act, w, x2d, gate)


def _ffn_kernel(x_ref, xp_ref, xn_ref, mul_ref, add_ref, wu_ref, cw_ref, cb_ref, wd_ref, g_ref, fg_ref, o_ref,
                act_scr, *, tm, width, tpi, has_rows, final_norm):
    i = pl.program_id(0)
    mul = mul_ref[...]
    add = add_ref[...]
    hmod = lambda x: (_rms(x) * mul + add).astype(BF16)
    x = x_ref[...]
    hm = hmod(x)
    if has_rows:
        first = (i % tpi) == 0
        last = (i % tpi) == tpi - 1
        hp = jnp.where(first, jnp.zeros((), BF16), hmod(xp_ref[...]))
        hn = jnp.where(last, jnp.zeros((), BF16), hmod(xn_ref[...]))
        h_ext = jnp.concatenate([hp, hm, hn], axis=0)
    else:
        h_ext = hm
    col = lax.broadcasted_iota(jnp.int32, (tm, 1), 0) & (width - 1)
    not_left = col != 0
    not_right = col != width - 1
    inv_sqrt2 = 1.0 / math.sqrt(2.0)
    def up(c0):
        return (jnp.dot(h_ext, wu_ref[:, c0:c0 + FF_CHUNK], preferred_element_type=F32),
                jnp.dot(hm, wu_ref[:, D_FF + c0:D_FF + c0 + FF_CHUNK], preferred_element_type=F32))

    starts = list(range(0, D_FF, FF_CHUNK))
    cuts = [starts[(len(starts) * n) // FF_DOWN_PARTS] for n in range(1, FF_DOWN_PARTS)]
    down = lambda lo, hi: jnp.dot(act_scr[:, lo:hi], wd_ref[lo:hi, :], preferred_element_type=F32)
    nxt = up(starts[0])
    y_mm = None
    done = 0
    for n_c, c0 in enumerate(starts):
        sl = slice(c0, c0 + FF_CHUNK)
        u_ext, vv = nxt
        if n_c + 1 < len(starts):
            nxt = up(starts[n_c + 1])
        if c0 in cuts:
            part = down(done, c0)
            y_mm = part if y_mm is None else y_mm + part
            done = c0
        if has_rows:
            rows = (u_ext[:tm], u_ext[width:width + tm], u_ext[2 * width:2 * width + tm])
            colsum = [cw_ref[b:b + 1, sl] * rows[0] + cw_ref[3 + b:4 + b, sl] * rows[1]
                      + cw_ref[6 + b:7 + b, sl] * rows[2] for b in range(3)]
        else:
            colsum = [cw_ref[3 + b:4 + b, sl] * u_ext for b in range(3)]
        acc = (colsum[1] + cb_ref[:, sl]
               + jnp.where(not_left, pltpu.roll(colsum[0], 1, 0), 0.0)
               + jnp.where(not_right, pltpu.roll(colsum[2], tm - 1, 0), 0.0))
        gelu = 0.5 * acc * (1.0 + lax.erf(acc * inv_sqrt2))
        act_scr[:, sl] = (gelu * vv).astype(BF16)
    y = x + g_ref[...] * (y_mm + down(done, D_FF))
    if final_norm:
        y = _rms(y) * fg_ref[...]
    o_ref[...] = y


def _ffn(x2d, mul, add, w_up, conv_w, conv_b, w_down, gate, final_g, *, seq, width, tm, final_norm, name):
    m = x2d.shape[0]
    tpi = seq // tm
    rpt = tm // width
    nrow = m // width
    has_rows = seq > width
    kern = functools.partial(_ffn_kernel, tm=tm, width=width, tpi=tpi, has_rows=has_rows, final_norm=final_norm)
    resident = lambda shape: pl.BlockSpec(shape, lambda i: (0,) * len(shape), pipeline_mode=pl.Buffered(1))
    return pl.pallas_call(
        kern,
        out_shape=jax.ShapeDtypeStruct((m, D_MODEL), F32),
        grid=(m // tm,),
        in_specs=[pl.BlockSpec((tm, D_MODEL), lambda i: (i, 0)),
                  pl.BlockSpec((width, D_MODEL), lambda i: (jnp.maximum(i * rpt - 1, 0), 0)),
                  pl.BlockSpec((width, D_MODEL), lambda i: (jnp.minimum((i + 1) * rpt, nrow - 1), 0)),
                  pl.BlockSpec((None, 1, D_MODEL), lambda i: (i // tpi, 0, 0)),
                  pl.BlockSpec((None, 1, D_MODEL), lambda i: (i // tpi, 0, 0)),
                  resident((D_MODEL, 2 * D_FF)),
                  pl.BlockSpec((9, D_FF), lambda i: (0, 0)),
                  pl.BlockSpec((1, D_FF), lambda i: (0, 0)),
                  resident((D_FF, D_MODEL)),
                  pl.BlockSpec((None, 1, D_MODEL), lambda i: (i // tpi, 0, 0)),
                  pl.BlockSpec((1, D_MODEL), lambda i: (0, 0))],
        out_specs=pl.BlockSpec((tm, D_MODEL), lambda i: (i, 0)),
        scratch_shapes=[pltpu.VMEM((tm, D_FF), BF16)],
        compiler_params=_cp("parallel"),
        name=name,
    )(x2d, x2d, x2d, mul, add, w_up, conv_w, conv_b, w_down, gate, final_g)


def _hgrn_levels(c):
    levels = []
    m = c // 2
    while m >= 1:
        levels.append(m)
        m //= 2
    return levels


def _hgrn_tables(c, rev):
    levels = _hgrn_levels(c)
    msk = np.zeros((len(levels) + 1, c, c), np.float32)
    t = np.arange(c)
    for l, m in enumerate(levels):
        blk = t // (2 * m)
        second = (t % (2 * m)) >= m
        msk[l] = (blk[:, None] == blk[None, :]) & second[:, None] & (~second)[None, :]
    msk[len(levels)] = np.eye(c)
    tri = np.tril(np.ones((c, c), np.float32))
    if rev:
        msk = msk[:, ::-1, ::-1]
        tri = tri.T
    return jnp.asarray(tri, BF16), jnp.asarray(np.ascontiguousarray(msk), F32)


def _level_ref(bcum, m, rev, row):
    c = bcum.shape[0]
    r = m if rev else m - 1
    if 2 * m >= 8:
        return jnp.concatenate([jnp.broadcast_to(bcum[s + r:s + r + 1, :], (2 * m, bcum.shape[1]))
                                for s in range(0, c, 2 * m)], axis=0)
    phase = row & (2 * m - 1)
    out = bcum
    for ph in range(2 * m):
        if ph != r:
            out = jnp.where(phase == ph, pltpu.roll(bcum, (ph - r) % c, 0), out)
    return out


def _hgrn_scan_kernel(q_ref, v_ref, z_ref, lb_ref, tri_ref, m_ref, s0_ref, o_ref, sf_ref, st_scr, *, c, rev):
    j = pl.program_id(0)

    @pl.when(j == 0)
    def _():
        st_scr[...] = s0_ref[...]

    levels = _hgrn_levels(c)
    nl = len(levels)
    lb = lb_ref[...]
    tri = tri_ref[...]
    row = lax.broadcasted_iota(jnp.int32, (c, 1), 0)
    sls = [slice(h * HG_DK, (h + 1) * HG_DK) for h in range(HG_HEADS)]
    nbat = q_ref.shape[0]
    items = [(b, h) for b in range(nbat) for h in range(HG_HEADS)]
    q, kin, v, bcum, btot = [], [], [], [], []
    for b in range(nbat):
        z = z_ref[b]
        logf = jnp.log(lb + (1.0 - lb) * _sigmoid(z))
        kin.append((1.0 - lb) * _sigmoid(-z))
        log2f = logf * (1.0 / math.log(2.0))
        bcum.append(_exact_dot2(tri, log2f))
        btot.append(jnp.sum(log2f, axis=0, keepdims=True))
        q.append(q_ref[b].astype(F32))
        v.append(v_ref[b])
    a = [m_ref[nl] * _bdot_nt(q[b][:, sls[h]], kin[b][:, sls[h]]) for b, h in items]
    for l, m in enumerate(levels):
        qf, kf = [], []
        for b in range(nbat):
            f = jnp.exp2(_neg_abs(bcum[b] - _level_ref(bcum[b], m, rev, row)))
            qf.append((q[b] * f).astype(BF16))
            kf.append((kin[b] * f).astype(BF16))
        a = [a[i] + m_ref[l] * _bdot_nt(qf[b][:, sls[h]], kf[b][:, sls[h]]) for i, (b, h) in enumerate(items)]
    qtop = [(q[b] * jnp.exp2(bcum[b])).astype(BF16) for b in range(nbat)]
    kend = [(kin[b] * jnp.exp2(btot[b] - bcum[b])).astype(BF16) for b in range(nbat)]
    dec_tot = [jnp.exp2(btot[b]) for b in range(nbat)]
    for i, (b, h) in enumerate(items):
        sl = sls[h]
        st = st_scr[b, h]
        o_ref[b, :, sl] = (_bdot_nt(qtop[b][:, sl], st) + _bdot(a[i], v[b][:, sl])).astype(o_ref.dtype)
        st_scr[b, h] = st * dec_tot[b][:, sl] + _bdot_tn(v[b][:, sl], kend[b][:, sl])

    @pl.when(j == pl.num_programs(0) - 1)
    def _():
        sf_ref[...] = st_scr[...]


def _hgrn_scan(p, zf, lb_row, s0, *, bsz, seq, name):
    c = min(HG_CHUNK, seq)
    nc = seq // c
    st_shape = (HG_HEADS, HG_DK, HG_DK)
    p3 = p.reshape(bsz, seq, -1)
    z3 = zf.reshape(bsz, seq, -1)
    outs, finals = [], []
    for d in range(2):
        tri, masks = _hgrn_tables(c, rev=bool(d))
        crow = (lambda j: nc - 1 - j) if d else (lambda j: j)
        blk = lambda col, crow=crow: pl.BlockSpec((bsz, c, D_MODEL), lambda j: (0, crow(j), col))
        o, sf = pl.pallas_call(
            functools.partial(_hgrn_scan_kernel, c=c, rev=bool(d)),
            out_shape=(jax.ShapeDtypeStruct((bsz, seq, D_MODEL), BF16),
                       jax.ShapeDtypeStruct((bsz,) + st_shape, F32)),
            grid=(nc,),
            in_specs=[blk(0), blk(1), blk(d),
                      pl.BlockSpec((1, D_MODEL), lambda j: (0, 0)),
                      pl.BlockSpec(tri.shape, lambda j: (0, 0)),
                      pl.BlockSpec(masks.shape, lambda j: (0, 0, 0)),
                      pl.BlockSpec((bsz, None) + st_shape, lambda j, d=d: (0, d, 0, 0, 0))],
            out_specs=(blk(0), pl.BlockSpec((bsz,) + st_shape, lambda j: (0, 0, 0, 0))),
            scratch_shapes=[pltpu.VMEM((bsz,) + st_shape, F32)],
            compiler_params=_cp("arbitrary"),
            name=f"{name}_d{d}",
        )(p3, p3, z3, lb_row, tri, masks, s0)
        outs.append(o.reshape(bsz * seq, D_MODEL))
        finals.append(sf)
    return outs, jnp.stack(finals, axis=1)


def _hgrn_out_kernel(o0_ref, o1_ref, gate_ref, ng_ref, w_ref, x_ref, g_ref, out_ref, act_scr):
    o = o0_ref[...].astype(F32) + o1_ref[...].astype(F32)
    gate = gate_ref[...].astype(F32)
    ng = ng_ref[...]
    for h in range(HG_HEADS):
        sl = slice(h * HG_DK, (h + 1) * HG_DK)
        act_scr[:, sl] = (_rms(o[:, sl]) * ng[:, sl] * _silu(gate[:, sl])).astype(BF16)
    out_ref[...] = x_ref[...] + g_ref[...] * jnp.dot(act_scr[...], w_ref[...], preferred_element_type=F32)


def _hgrn_out(o, p, norm_row, w_out, x2d, gate, *, seq, tm, name):
    m = x2d.shape[0]
    tpb = seq // tm
    return pl.pallas_call(
        _hgrn_out_kernel,
        out_shape=jax.ShapeDtypeStruct((m, D_MODEL), F32),
        grid=(m // tm,),
        in_specs=[pl.BlockSpec((tm, D_MODEL), lambda i: (i, 0)),
                  pl.BlockSpec((tm, D_MODEL), lambda i: (i, 0)),
                  pl.BlockSpec((tm, D_MODEL), lambda i: (i, 2)),
                  pl.BlockSpec((1, D_MODEL), lambda i: (0, 0)),
                  pl.BlockSpec((D_MODEL, D_MODEL), lambda i: (0, 0)),
                  pl.BlockSpec((tm, D_MODEL), lambda i: (i, 0)),
                  pl.BlockSpec((None, 1, D_MODEL), lambda i: (i // tpb, 0, 0))],
        out_specs=pl.BlockSpec((tm, D_MODEL), lambda i: (i, 0)),
        scratch_shapes=[pltpu.VMEM((tm, D_MODEL), BF16)],
        compiler_params=_cp("parallel"),
        name=name,
    )(o[0], o[1], p, norm_row, w_out, x2d, gate)


def _sink_attend(qs, keys, vals, mask, sink_ref, o_ref):
    t = qs[0].shape[0]
    hsl = lambda hd: slice(hd * AT_HD, (hd + 1) * AT_HD)
    items = [(n, g) for n in range(len(qs)) for g in range(AT_KV_HEADS)]
    q4 = [jnp.concatenate([qs[n][:, hsl(g * AT_GROUP + r)] for r in range(AT_GROUP)], axis=0) for n, g in items]
    logits = [_bdot_nt(q4[i], keys[n][:, hsl(g)]) for i, (n, g) in enumerate(items)]
    if mask is not None:
        pen = jnp.where(mask, 0.0, -jnp.inf)
        pen4 = jnp.concatenate([pen] * AT_GROUP, axis=0)
        logits = [lg + pen4 for lg in logits]
    s = [jnp.concatenate([jnp.full((t, 1), sink_ref[g * AT_GROUP + r], F32) for r in range(AT_GROUP)], axis=0)
         for n, g in items]
    mx = [jnp.maximum(jnp.max(lg, axis=-1, keepdims=True), s[i]) for i, lg in enumerate(logits)]
    p = [jnp.exp(lg - mx[i]) for i, lg in enumerate(logits)]
    denom = [jnp.sum(p[i], axis=-1, keepdims=True) + jnp.exp(s[i] - mx[i]) for i in range(len(items))]
    o4 = [_bdot(p[i], vals[n][:, hsl(g)]) / denom[i] for i, (n, g) in enumerate(items)]
    for i, (n, g) in enumerate(items):
        for r in range(AT_GROUP):
            o_ref[n, :, hsl(g * AT_GROUP + r)] = o4[i][r * t:(r + 1) * t].astype(o_ref.dtype)


def _attn_kernel(sink_ref, q_ref, qr_ref, kp_ref, kc_ref, kn_ref, krp_ref, krc_ref, krn_ref, vp_ref, vc_ref, vn_ref,
                 cosp_ref, cosc_ref, cosn_ref, sinp_ref, sinc_ref, sinn_ref, ck_ref, cv_ref, o_ref, *, nctx):
    i = pl.program_id(0)
    nb = pl.num_programs(0)
    bs = range(q_ref.shape[0])
    f32 = lambda x: x.astype(F32)
    wide = lambda t, n: jnp.concatenate([t] * n, axis=1)
    cos_q, sin_q = wide(cosc_ref[...], D_MODEL // 128), wide(sinc_ref[...], D_MODEL // 128)
    rope_k = lambda k_ref, kr_ref, c_ref, s_ref, n: (
        f32(k_ref[n]) * wide(c_ref[...], AT_KV_DIM // 128) + f32(kr_ref[n]) * wide(s_ref[...], AT_KV_DIM // 128)
    ).astype(BF16)
    qs = [((f32(q_ref[n]) * cos_q + f32(qr_ref[n]) * sin_q) * (AT_HD ** -0.5)).astype(BF16) for n in bs]
    keys = [jnp.concatenate([ck_ref[n], rope_k(kp_ref, krp_ref, cosp_ref, sinp_ref, n),
                             rope_k(kc_ref, krc_ref, cosc_ref, sinc_ref, n),
                             rope_k(kn_ref, krn_ref, cosn_ref, sinn_ref, n)], axis=0) for n in bs]
    vals = [jnp.concatenate([cv_ref[n], vp_ref[n], vc_ref[n], vn_ref[n]], axis=0) for n in bs]
    nk = nctx + 3 * AT_BLOCK
    ti = lax.broadcasted_iota(jnp.int32, (AT_BLOCK, nk), 0)
    kj = lax.broadcasted_iota(jnp.int32, (AT_BLOCK, nk), 1) - nctx
    lo = jnp.where(i > 0, 0, AT_BLOCK)
    hi = jnp.where(i < nb - 1, 3 * AT_BLOCK, 2 * AT_BLOCK)
    rel = kj - ti
    mask = (kj < 0) | ((rel >= 0) & (rel <= 2 * AT_BLOCK) & (kj >= lo) & (kj < hi))
    _sink_attend(qs, keys, vals, mask, sink_ref, o_ref)


def _attn(sink, p, p_ctx, cos_t, sin_t, *, bsz, seq, nctx):
    nb = seq // AT_BLOCK
    cur = lambda i: i
    prv = lambda i: jnp.maximum(i - 1, 0)
    nxt = lambda i: jnp.minimum(i + 1, nb - 1)
    kvspec = lambda f, col: pl.BlockSpec((bsz, AT_BLOCK, AT_KV_DIM), lambda i: (0, f(i), col))
    tspec = lambda f: pl.BlockSpec((AT_BLOCK, 128), lambda i: (f(i), 0))
    three = lambda mk: [mk(prv), mk(cur), mk(nxt)]
    p3 = p.reshape(bsz, seq, -1)
    c3 = p_ctx.reshape(bsz, nctx, -1)
    out = pl.pallas_call(
        functools.partial(_attn_kernel, nctx=nctx),
        out_shape=jax.ShapeDtypeStruct((bsz, seq, D_MODEL), BF16),
        grid=(nb,),
        in_specs=[pl.BlockSpec(memory_space=pltpu.SMEM),
                  pl.BlockSpec((bsz, AT_BLOCK, D_MODEL), lambda i: (0, i, 0)),
                  pl.BlockSpec((bsz, AT_BLOCK, D_MODEL), lambda i: (0, i, 1))]
                 + three(lambda f: kvspec(f, 8)) + three(lambda f: kvspec(f, 10)) + three(lambda f: kvspec(f, 9))
                 + three(tspec) + three(tspec)
                 + [pl.BlockSpec((bsz, nctx, AT_KV_DIM), lambda i: (0, 0, 8)),
                    pl.BlockSpec((bsz, nctx, AT_KV_DIM), lambda i: (0, 0, 9))],
        out_specs=pl.BlockSpec((bsz, AT_BLOCK, D_MODEL), lambda i: (0, i, 0)),
        compiler_params=_cp("parallel"),
        name="at_window",
    )(sink, p3, p3, p3, p3, p3, p3, p3, p3, p3, p3, p3, cos_t, cos_t, cos_t, sin_t, sin_t, sin_t, c3, c3)
    return out.reshape(bsz * seq, D_MODEL)


def _attn_ctx_kernel(sink_ref, q_ref, k_ref, v_ref, o_ref):
    _sink_attend([q_ref[0] * (AT_HD ** -0.5)], [k_ref[0]], [v_ref[0]], None, sink_ref, o_ref)


def _attn_ctx(sink, p_ctx, *, bsz, nctx):
    c3 = p_ctx.reshape(bsz, nctx, -1)
    out = pl.pallas_call(
        _attn_ctx_kernel,
        out_shape=jax.ShapeDtypeStruct((bsz, nctx, D_MODEL), BF16),
        grid=(bsz,),
        in_specs=[pl.BlockSpec(memory_space=pltpu.SMEM),
                  pl.BlockSpec((1, nctx, D_MODEL), lambda b: (b, 0, 0)),
                  pl.BlockSpec((1, nctx, AT_KV_DIM), lambda b: (b, 0, 8)),
                  pl.BlockSpec((1, nctx, AT_KV_DIM), lambda b: (b, 0, 9))],
        out_specs=pl.BlockSpec((1, nctx, D_MODEL), lambda b: (b, 0, 0)),
        compiler_params=_cp("parallel"),
        name="at_ctx",
    )(sink, c3, c3, c3)
    return out.reshape(bsz * nctx, D_MODEL)


def _ret_scan_kernel(q_ref, k_ref, v_ref, cos_ref, sin_ref, dm_ref, qin_ref, kout_ref, car_ref, s0_ref,
                     o_ref, sf_ref, st_scr, *, with_intra):
    j = pl.program_id(0)

    @pl.when(j == 0)
    def _():
        st_scr[...] = s0_ref[...]

    half = RT_DK // 2
    cos = jnp.concatenate([cos_ref[...]] * 2, axis=1)
    sin = jnp.concatenate([sin_ref[...]] * 2, axis=1)

    def rope(x):
        rot = jnp.concatenate([-x[:, half:], x[:, :half]], axis=1)
        return x * cos + rot * sin

    sk = [slice(h * RT_DK, (h + 1) * RT_DK) for h in range(RT_HEADS)]
    sv = [slice(h * RT_DV, (h + 1) * RT_DV) for h in range(RT_HEADS)]
    items = [(b, h) for b in range(q_ref.shape[0]) for h in range(RT_HEADS)]
    qh = [rope(q_ref[b, :, sk[h]].astype(F32)) for b, h in items]
    kh = [rope(k_ref[b, :, sk[h]].astype(F32) * (RT_DK ** -0.5)) for b, h in items]
    vh = [v_ref[b, :, sv[h]] for b, h in items]
    st = [st_scr[b, h] for b, h in items]
    o = [_bdot(qh[i] * qin_ref[:, sk[h]], st[i]) for i, (b, h) in enumerate(items)]
    if with_intra:
        s = [_bdot_nt(qh[i], kh[i]) * dm_ref[h] for i, (b, h) in enumerate(items)]
        o = [o[i] + _bdot(s[i], vh[i]) for i in range(len(items))]
    for i, (b, h) in enumerate(items):
        o_ref[b, :, sv[h]] = o[i].astype(o_ref.dtype)
        st_scr[b, h] = car_ref[:, sv[h]] * st[i] + _bdot_tn(kh[i] * kout_ref[:, sk[h]], vh[i])

    @pl.when(j == pl.num_programs(0) - 1)
    def _():
        sf_ref[...] = st_scr[...]


def _ret_tables(decay_exp, c):
    lg = jnp.log1p(-jnp.exp2(decay_exp.astype(F32)))
    idx = jnp.arange(c, dtype=F32)
    pos = jnp.stack([idx, c - 1.0 - idx])
    rel = pos[:, :, None] - pos[:, None, :]
    lgh = lg[:, :, None, None]
    dm = jnp.where(rel[:, None] >= 0, jnp.exp(lgh * jnp.maximum(rel[:, None], 0.0)), 0.0)
    qin = jnp.exp(lg[:, None, :] * (pos[:, :, None] + 1.0))
    kout = jnp.exp(lg[:, None, :] * (c - 1.0 - pos[:, :, None]))
    car = jnp.exp(lg * c)
    qin = jnp.repeat(qin, RT_DK, axis=-1)
    kout = jnp.repeat(kout, RT_DK, axis=-1)
    car = jnp.repeat(car, RT_DV, axis=-1)[:, None, :]
    return dm, qin, kout, car


def _ret_scan(p, cos_t, sin_t, decay_exp, s0, *, bsz, seq, name):
    c = min(RT_CHUNK, seq)
    nc = seq // c
    dm, qin, kout, car = _ret_tables(decay_exp, c)
    dm_both = dm[0] + dm[1]
    st_shape = (RT_HEADS, RT_DK, RT_DV)
    p3 = p.reshape(bsz, seq, -1)
    outs, finals = [], []
    for d in range(2):
        crow = (lambda j: nc - 1 - j) if d else (lambda j: j)
        o, sf = pl.pallas_call(
            functools.partial(_ret_scan_kernel, with_intra=(d == 0)),
            out_shape=(jax.ShapeDtypeStruct((bsz, seq, RT_V_DIM), BF16),
                       jax.ShapeDtypeStruct((bsz,) + st_shape, F32)),
            grid=(nc,),
            in_specs=[pl.BlockSpec((bsz, c, D_MODEL), lambda j, crow=crow: (0, crow(j), 0)),
                      pl.BlockSpec((bsz, c, D_MODEL), lambda j, crow=crow: (0, crow(j), 1)),
                      pl.BlockSpec((bsz, c, RT_V_DIM), lambda j, crow=crow: (0, crow(j), 1)),
                      pl.BlockSpec((c, RT_DK // 2), lambda j, crow=crow: (crow(j), 0)),
                      pl.BlockSpec((c, RT_DK // 2), lambda j, crow=crow: (crow(j), 0)),
                      pl.BlockSpec((RT_HEADS, c, c), lambda j: (0, 0, 0)),
                      pl.BlockSpec((None, c, D_MODEL), lambda j, d=d: (d, 0, 0)),
                      pl.BlockSpec((None, c, D_MODEL), lambda j, d=d: (d, 0, 0)),
                      pl.BlockSpec((None, 1, RT_V_DIM), lambda j, d=d: (d, 0, 0)),
                      pl.BlockSpec((bsz, None) + st_shape, lambda j, d=d: (0, d, 0, 0, 0))],
            out_specs=(pl.BlockSpec((bsz, c, RT_V_DIM), lambda j, crow=crow: (0, crow(j), 0)),
                       pl.BlockSpec((bsz,) + st_shape, lambda j: (0, 0, 0, 0))),
            scratch_shapes=[pltpu.VMEM((bsz,) + st_shape, F32)],
            compiler_params=_cp("arbitrary"),
            name=f"{name}_d{d}",
        )(p3, p3, p3, cos_t, sin_t, dm_both, qin, kout, car, s0)
        outs.append(o.reshape(bsz * seq, RT_V_DIM))
        finals.append(sf)
    return outs, jnp.stack(finals, axis=1)


def _ret_out_kernel(o0_ref, o1_ref, gate_ref, w_ref, x_ref, g_ref, out_ref, act_scr):
    for h in range(RT_HEADS):
        sv = slice(h * RT_DV, (h + 1) * RT_DV)
        o = o0_ref[:, sv].astype(F32) + o1_ref[:, sv].astype(F32)
        act_scr[:, sv] = (_silu(gate_ref[:, sv].astype(F32)) * _rms(o)).astype(BF16)
    out_ref[...] = x_ref[...] + g_ref[...] * jnp.dot(act_scr[...], w_ref[...], preferred_element_type=F32)


def _ret_out(o, p, w_out, x2d, gate, *, seq, tm, name):
    m = x2d.shape[0]
    tpb = seq // tm
    return pl.pallas_call(
        _ret_out_kernel,
        out_shape=jax.ShapeDtypeStruct((m, D_MODEL), F32),
        grid=(m // tm,),
        in_specs=[pl.BlockSpec((tm, RT_V_DIM), lambda i: (i, 0)),
                  pl.BlockSpec((tm, RT_V_DIM), lambda i: (i, 0)),
                  pl.BlockSpec((tm, RT_V_DIM), lambda i: (i, 2)),
                  pl.BlockSpec((RT_V_DIM, D_MODEL), lambda i: (0, 0)),
                  pl.BlockSpec((tm, D_MODEL), lambda i: (i, 0)),
                  pl.BlockSpec((None, 1, D_MODEL), lambda i: (i // tpb, 0, 0))],
        out_specs=pl.BlockSpec((tm, D_MODEL), lambda i: (i, 0)),
        scratch_shapes=[pltpu.VMEM((tm, RT_V_DIM), BF16)],
        compiler_params=_cp("parallel"),
        name=name,
    )(o[0], o[1], p, w_out, x2d, gate)


def _seg_sum(x, e_down, e_up):
    s = jnp.dot(x.astype(BF16), e_down, preferred_element_type=F32)
    s_hi = s.astype(BF16)
    s_lo = (s - s_hi.astype(F32)).astype(BF16)
    return jnp.dot(s_hi, e_up, preferred_element_type=F32) + jnp.dot(s_lo, e_up, preferred_element_type=F32)


def _rw_proj_kernel(x_ref, xp_ref, xn_ref, mul_ref, add_ref, mix_ref, wrkv_ref, w0_ref, w1_ref, w2_ref,
                    a0_ref, a1_ref, a2_ref, g1_ref, g2_ref, kk_ref, ka_ref, rk_ref, ed_ref, eu_ref,
                    r_out, v_out, kn_out, gate_out, bonus_out, lw_out, kd_out, bb_out,
                    *, tm, width, tpi, quarters):
    i = pl.program_id(0)
    first = (i % tpi) == 0
    last = (i % tpi) == tpi - 1
    mul = mul_ref[...]
    add = add_ref[...]
    hmod = lambda x: _rms(x) * mul + add
    hm = hmod(x_ref[...])
    col = lax.broadcasted_iota(jnp.int32, (tm, 1), 0) & (width - 1)
    srcs = {}
    if 'l' in quarters:
        srcs['l'] = jnp.where(col != 0, pltpu.roll(hm, 1, 0), 0.0)
    if 'r' in quarters:
        srcs['r'] = jnp.where(col != width - 1, pltpu.roll(hm, tm - 1, 0), 0.0)
    if 'u' in quarters:
        hp = jnp.where(first, 0.0, hmod(xp_ref[...]))
        srcs['u'] = jnp.concatenate([hp, hm[: tm - width]], axis=0) if tm > width else hp
    if 'd' in quarters:
        hn = jnp.where(last, 0.0, hmod(xn_ref[...]))
        srcs['d'] = jnp.concatenate([hm[width:], hn], axis=0) if tm > width else hn
    qd = D_MODEL // 4
    shifted = jnp.concatenate([srcs[q][:, n * qd:(n + 1) * qd] for n, q in enumerate(quarters)], axis=1)
    xx = shifted - hm
    mixed = lambda n: hm + xx * mix_ref[n:n + 1, :]
    r = _bdot(mixed(0), wrkv_ref[0])
    k = _bdot(mixed(2), wrkv_ref[1])
    v = _bdot(mixed(3), wrkv_ref[2])
    xw = mixed(1)
    xa = mixed(4)
    ed = ed_ref[...]
    eu = eu_ref[...]
    kkh = k * kk_ref[...]
    nrm = jnp.sqrt(_seg_sum(kkh * kkh, ed, eu))
    kn = kkh / jnp.maximum(nrm, 1e-12)
    ksum = jnp.zeros_like(k)
    for z in range(2):
        w_raw = w0_ref[z] + _bdot(jnp.tanh(_bdot(xw, w1_ref[z])), w2_ref[z])
        lw_out[z] = -math.exp(-0.5) * _sigmoid(w_raw)
        a = _sigmoid(a0_ref[z] + _bdot(_bdot(xa, a1_ref[z]), a2_ref[z]))
        kd = k * (1.0 + (a - 1.0) * ka_ref[...])
        kd_out[z] = kd.astype(BF16)
        bb_out[z] = (kn * a).astype(BF16)
        ksum = ksum + kd
    r_out[...] = r.astype(BF16)
    v_out[...] = v.astype(BF16)
    kn_out[...] = kn.astype(BF16)
    gate_out[...] = _bdot(_sigmoid(_bdot(mixed(5), g1_ref[...])), g2_ref[...]).astype(BF16)
    bonus_out[...] = (_seg_sum(r * ksum * rk_ref[...], ed, eu) * v).astype(BF16)


def _seg_mats():
    hid = np.arange(D_MODEL) // RW_HEAD
    e_down = (hid[:, None] == np.arange(128)[None, :]).astype(np.float32)
    return jnp.asarray(e_down, BF16), jnp.asarray(e_down.T, BF16)


def _rw_proj(x2d, mul, add, wts, *, seq, width, tm, quarters, name):
    m = x2d.shape[0]
    tpi = seq // tm
    rpt = tm // width
    nrow = m // width
    e_down, e_up = _seg_mats()
    full = lambda a: pl.BlockSpec(a.shape, lambda i: (0,) * a.ndim)
    consts = [wts['mix'], wts['w_rkv'], wts['w0'], wts['w1'], wts['w2'], wts['a0'], wts['a1'], wts['a2'],
              wts['g1'], wts['g2'], wts['k_k'], wts['k_a'], wts['r_k'], e_down, e_up]
    kern = functools.partial(_rw_proj_kernel, tm=tm, width=width, tpi=tpi, quarters=quarters)
    one = jax.ShapeDtypeStruct((m, D_MODEL), BF16)
    two = jax.ShapeDtypeStruct((2, m, D_MODEL), BF16)
    two_f32 = jax.ShapeDtypeStruct((2, m, D_MODEL), F32)
    ospec1 = pl.BlockSpec((tm, D_MODEL), lambda i: (i, 0))
    ospec2 = pl.BlockSpec((2, tm, D_MODEL), lambda i: (0, i, 0))
    return pl.pallas_call(
        kern,
        out_shape=(one, one, one, one, one, two_f32, two, two),
        grid=(m // tm,),
        in_specs=[pl.BlockSpec((tm, D_MODEL), lambda i: (i, 0)),
                  pl.BlockSpec((width, D_MODEL), lambda i: (jnp.maximum(i * rpt - 1, 0), 0)),
                  pl.BlockSpec((width, D_MODEL), lambda i: (jnp.minimum((i + 1) * rpt, nrow - 1), 0)),
                  pl.BlockSpec((None, 1, D_MODEL), lambda i: (i // tpi, 0, 0)),
                  pl.BlockSpec((None, 1, D_MODEL), lambda i: (i // tpi, 0, 0))] + [full(a) for a in consts],
        out_specs=(ospec1, ospec1, ospec1, ospec1, ospec1, ospec2, ospec2, ospec2),
        compiler_params=_cp("parallel"),
        name=name,
    )(x2d, x2d, x2d, mul, add, *consts)


def _rw_tables(c):
    t = np.arange(c)[:, None]
    s = np.arange(c * RW_GROUP_HEADS)[None, :] % c
    ms = np.stack([s < t, s > t]).astype(np.float32)
    mi = np.stack([s <= t, s >= t]).astype(np.float32)
    tri0 = np.tril(np.ones((c, c)))
    tri = np.stack([tri0, tri0.T])
    hrow = np.arange(c * RW_GROUP_HEADS) // c
    lane_h = np.arange(RW_GROUP) // RW_HEAD
    hmask = hrow[:, None] == lane_h[None, :]
    bdc = hrow[:, None] == hrow[None, :]
    bdm = lane_h[:, None] == lane_h[None, :]
    return (jnp.asarray(tri, BF16), jnp.asarray(ms), jnp.asarray(mi), jnp.asarray(hmask.astype(np.float32), BF16),
            jnp.asarray(bdc.astype(np.float32), BF16), jnp.asarray(bdm.astype(np.float32)))


def _rw_scan_kernel(r_ref, kn_ref, v_ref, lw_ref, kd_ref, bb_ref, tri_ref, ms_ref, mi_ref, hm_ref, bdc_ref, bdm_ref,
                    s0_ref, y_ref, sf_ref, st_scr, *, c):
    j = pl.program_id(1)

    @pl.when(j == 0)
    def _():
        st_scr[...] = s0_ref[...]

    hm = hm_ref[...]
    ms = ms_ref[...]
    mi = mi_ref[...]
    bdc = bdc_ref[...]
    bdm = bdm_ref[...]
    tri = tri_ref[...]
    ncat = c * RW_GROUP_HEADS
    eye = (lax.broadcasted_iota(jnp.int32, (c, ncat), 0)
           == (lax.broadcasted_iota(jnp.int32, (c, ncat), 1) & (c - 1))).astype(F32)
    tile = lambda x: jnp.concatenate([x] * RW_GROUP_HEADS, axis=0)
    stacked = lambda x: tile(x) * hm
    blockdiag = lambda x: tile(x) * bdc
    rows = lambda a, b: jnp.concatenate([a, b], axis=0)
    nsq = int(math.log2(c)) - 1
    b16 = lambda x: x.astype(BF16)
    dot = lambda a, b: jnp.dot(a, b, preferred_element_type=F32)
    dot_nt = lambda a, b: lax.dot_general(a, b, (((1,), (1,)), ((), ())), preferred_element_type=F32)
    sls = [slice(gi * RW_GROUP, (gi + 1) * RW_GROUP) for gi in range(RW_NGROUPS)]

    pre = []
    for b in range(r_ref.shape[0]):
        lw = lw_ref[b]
        g = _exact_dot(tri, lw)
        gtot = jnp.sum(lw, axis=0, keepdims=True)
        e_inv = jnp.exp(-g)
        e_end = jnp.exp(gtot - g)
        kd = kd_ref[b].astype(F32)
        bb = bb_ref[b].astype(F32)
        pre.append(dict(kt=b16(kn_ref[b].astype(F32) * jnp.exp(g - lw)), rt=b16(r_ref[b].astype(F32) * jnp.exp(g)),
                        kh=b16(kd * e_inv), bh=b16(bb * e_inv), ke=b16(kd * e_end), be=b16(bb * e_end),
                        v=v_ref[b], dec=jnp.exp(gtot)))

    items = [(b, gi) for b in range(len(pre)) for gi in range(RW_NGROUPS)]
    ids = range(len(items))
    part = lambda name: [pre[b][name][:, sls[gi]] for b, gi in items]
    kr = [rows(a, b) for a, b in zip(part('kt'), part('rt'))]
    yb = [stacked(x) for x in part('bh')]
    yk = [stacked(x) for x in part('kh')]
    vm = [stacked(x) for x in part('v')]
    by_b = [dot_nt(kr[i], yb[i]) for i in ids]
    by_k = [dot_nt(kr[i], yk[i]) for i in ids]
    n = [-(by_b[i][:c] * ms) for i in ids]
    m_b = [b16(by_b[i][c:] * mi) for i in ids]
    lm_k = [rows(b16(by_k[i][:c] * ms), b16(by_k[i][c:] * mi)) for i in ids]
    p = [eye + n[i] for i in ids]
    nb = [b16(n[i]) for i in ids]
    n = [dot(nb[i], blockdiag(nb[i])) for i in ids]
    for lvl in range(nsq):
        nb = [b16(n[i]) for i in ids]
        nbd = [blockdiag(nb[i]) for i in ids]
        if lvl + 1 < nsq:
            both = [dot(rows(nb[i], b16(p[i])), nbd[i]) for i in ids]
            n = [both[i][:c] for i in ids]
            p = [p[i] + both[i][c:] for i in ids]
        else:
            p = [p[i] + dot(b16(p[i]), nbd[i]) for i in ids]
    st = [st_scr[b, gi] for b, gi in items]
    by_s = [dot_nt(kr[i], b16(st[i])) for i in ids]
    by_v = [dot(lm_k[i], vm[i]) for i in ids]
    u = [dot(b16(p[i]), stacked(b16(by_s[i][:c] + by_v[i][:c]))) for i in ids]
    y = [by_s[i][c:] + by_v[i][c:] - dot(m_b[i], stacked(b16(u[i]))) for i in ids]
    for i, (b, gi) in enumerate(items):
        sl = sls[gi]
        y_ref[b, :, sl] = y[i].astype(y_ref.dtype)
        lhs = rows(pre[b]['v'][:, sl], b16(-u[i]))
        rhs_s = rows(pre[b]['ke'][:, sl], pre[b]['be'][:, sl])
        st_scr[b, gi] = st[i] * pre[b]['dec'][:, sl] + bdm * _bdot_tn(lhs, rhs_s)

    @pl.when(j == pl.num_programs(1) - 1)
    def _():
        sf_ref[...] = st_scr[...]


def _rw_scan(r, kn, v, lw, kd, bb, s0, *, bsz, seq, name):
    c = min(RW_CHUNK, seq)
    nc = seq // c
    tri, ms, mi, hmask, bdc, bdm = _rw_tables(c)
    crow = lambda d, j: j + d * (nc - 1 - 2 * j)
    st_shape = (RW_NGROUPS, RW_GROUP, RW_GROUP)
    shared = pl.BlockSpec((bsz, c, D_MODEL), lambda d, j: (0, crow(d, j), 0))
    perdir = pl.BlockSpec((None, bsz, c, D_MODEL), lambda d, j: (d, 0, crow(d, j), 0))
    stspec = pl.BlockSpec((bsz, None) + st_shape, lambda d, j: (0, d, 0, 0, 0))
    nrow = c * RW_GROUP_HEADS
    b3 = lambda a: a.reshape(bsz, seq, D_MODEL)
    b4 = lambda a: a.reshape(2, bsz, seq, D_MODEL)
    y, sf = pl.pallas_call(
        functools.partial(_rw_scan_kernel, c=c),
        out_shape=(jax.ShapeDtypeStruct((2, bsz, seq, D_MODEL), BF16),
                   jax.ShapeDtypeStruct((bsz, 2) + st_shape, F32)),
        grid=(2, nc),
        in_specs=[shared, shared, shared, perdir, perdir, perdir,
                  pl.BlockSpec((None, c, c), lambda d, j: (d, 0, 0)),
                  pl.BlockSpec((None, c, nrow), lambda d, j: (d, 0, 0)),
                  pl.BlockSpec((None, c, nrow), lambda d, j: (d, 0, 0)),
                  pl.BlockSpec((nrow, RW_GROUP), lambda d, j: (0, 0)),
                  pl.BlockSpec((nrow, nrow), lambda d, j: (0, 0)),
                  pl.BlockSpec((RW_GROUP, RW_GROUP), lambda d, j: (0, 0)),
                  stspec],
        out_specs=(perdir, stspec),
        scratch_shapes=[pltpu.VMEM((bsz,) + st_shape, F32)],
        compiler_params=_cp("parallel", "arbitrary"),
        name=name,
    )(b3(r), b3(kn), b3(v), b4(lw), b4(kd), b4(bb), tri, ms, mi, hmask, bdc, bdm, s0)
    return y.reshape(2, bsz * seq, D_MODEL), sf


def _rw_out_kernel(y0_ref, y1_ref, bonus_ref, gate_ref, lnw_ref, lnb_ref, ed_ref, eu_ref, w_ref, x_ref, g_ref,
                   out_ref):
    y = y0_ref[...].astype(F32) + y1_ref[...].astype(F32)
    ed = ed_ref[...]
    eu = eu_ref[...]
    mu = _seg_sum(y, ed, eu) * (1.0 / RW_HEAD)
    dlt = y - mu
    var = _seg_sum(dlt * dlt, ed, eu) * (1.0 / RW_HEAD)
    yn = dlt * lax.rsqrt(var + RW_GN_EPS) * lnw_ref[...] + lnb_ref[...] + bonus_ref[...].astype(F32)
    out_ref[...] = x_ref[...] + g_ref[...] * _bdot(yn * gate_ref[...].astype(F32), w_ref[...])


def _rw_out(y, bonus, gate_act, ln_w, ln_b, w_out, x2d, gate, *, seq, tm, name):
    m = x2d.shape[0]
    tpb = seq // tm
    e_down, e_up = _seg_mats()
    rowspec = pl.BlockSpec((tm, D_MODEL), lambda i: (i, 0))
    full = lambda a: pl.BlockSpec(a.shape, lambda i: (0,) * a.ndim)
    return pl.pallas_call(
        _rw_out_kernel,
        out_shape=jax.ShapeDtypeStruct((m, D_MODEL), F32),
        grid=(m // tm,),
        in_specs=[pl.BlockSpec((None, tm, D_MODEL), lambda i: (0, i, 0)),
                  pl.BlockSpec((None, tm, D_MODEL), lambda i: (1, i, 0)),
                  rowspec, rowspec, full(ln_w), full(ln_b), full(e_down), full(e_up), full(w_out), rowspec,
                  pl.BlockSpec((None, 1, D_MODEL), lambda i: (i // tpb, 0, 0))],
        out_specs=rowspec,
        compiler_params=_cp("parallel"),
        name=name,
    )(y, y, bonus, gate_act, ln_w, ln_b, e_down, e_up, w_out, x2d, gate)


def _rotate_half_cols(w, n_seg):
    sh = w.shape
    ws = w.reshape(sh[:-1] + (n_seg, 2, sh[-1] // (2 * n_seg)))
    return jnp.concatenate([-ws[..., 1:, :], ws[..., :1, :]], axis=-2).reshape(sh)


def _axial_tables(n_tokens):
    t = jnp.arange(n_tokens)
    row = (t // GRID_W).astype(F32)
    col = (t % GRID_W).astype(F32)
    inv = ROPE_BASE ** (-jnp.arange(0, AXIS_DIM, 2, dtype=F32) / AXIS_DIM)
    ang_r = row[:, None] * inv[None, :]
    ang_c = col[:, None] * inv[None, :]
    ang = jnp.concatenate([ang_r, ang_r, ang_c, ang_c], axis=-1)
    return jnp.tile(jnp.cos(ang), (1, 2)), jnp.tile(jnp.sin(ang), (1, 2))


def _retention_tables(n_tokens):
    t = jnp.arange(n_tokens, dtype=F32)
    inv = ROPE_BASE ** (-jnp.linspace(0.0, 1.0, RT_DK // 2, dtype=F32))
    ang = t[:, None] * inv[None, :]
    return jnp.cos(ang), jnp.sin(ang)


def _forward(x, c, ctx, c_ctx, ada_w, ada_b, norm1_g, norm2_g, ffn_w_up, ffn_conv_w, ffn_conv_b, ffn_w_down,
             hg_w_in, hg_lb_logits, hg_norm_g, hg_w_out, at_w_qkv, at_b_qkv, at_sink, at_w_out,
             rt_w_in, rt_decay_exp, rt_w_out, rw_mix, rw_w_rkv, rw_w0, rw_w1, rw_w2, rw_a0, rw_a1, rw_a2,
             rw_g1, rw_g2, rw_k_k, rw_k_a, rw_r_k, rw_ln_w, rw_ln_b, rw_w_out, final_norm_g, depth):
    bsz, seq, d = x.shape
    nctx = ctx.shape[1]
    tm_l = 512 if seq % 512 == 0 else seq
    tm_f = 256 if seq % 256 == 0 else seq
    tm_ffn = 1024 if seq % 1024 == 0 else tm_l
    tm_c = nctx

    cvecs = jnp.zeros((8, d), F32).at[:bsz].set(c).at[bsz].set(c_ctx)
    mod = _adaln(cvecs, ada_w[:depth], ada_b[:depth])
    lb_all = jnp.cumsum(jax.nn.softmax(hg_lb_logits.astype(F32), axis=0), axis=0)
    zero_b = lambda n: jnp.zeros((1, n), F32)

    xl = x.reshape(bsz * seq, d)
    xc = ctx.reshape(bsz * nctx, d)
    for i in range(depth):
        kind, j = i % 4, i // 4
        need_ctx = i < depth - 1
        ml = mod[i, :bsz].reshape(bsz, 6, 1, d)
        mc = jnp.broadcast_to(mod[i, bsz].reshape(1, 6, 1, d), (bsz, 6, 1, d))
        sh1, sc1, g1, sh2, sc2, g2 = (ml[:, n] for n in range(6))
        csh1, csc1, cg1, csh2, csc2, cg2 = (mc[:, n] for n in range(6))
        mul1, cmul1 = norm1_g[i] * (1.0 + sc1), norm1_g[i] * (1.0 + csc1)
        mul2, cmul2 = norm2_g[i] * (1.0 + sc2), norm2_g[i] * (1.0 + csc2)

        if kind == 0:
            w_in = hg_w_in[j].astype(BF16)
            w_out = hg_w_out[j].astype(BF16)
            lb_row = lb_all[i].reshape(1, d)
            ng_row = jnp.tile(hg_norm_g[j], HG_HEADS).reshape(1, d)
            pc, zc = _proj(xc, cmul1, csh1, w_in, zero_b(5 * d), seq=nctx, tm=tm_c, name="hg_proj_ctx",
                           n_f32_tail=2 * d)
            pL, zl = _proj(xl, mul1, sh1, w_in, zero_b(5 * d), seq=seq, tm=tm_l, name="hg_proj", n_f32_tail=2 * d)
            s0 = jnp.zeros((bsz, 2, HG_HEADS, HG_DK, HG_DK), F32)
            oc, s_ctx = _hgrn_scan(pc, zc, lb_row, s0, bsz=bsz, seq=nctx, name="hg_scan_ctx")
            ol, _ = _hgrn_scan(pL, zl, lb_row, s_ctx, bsz=bsz, seq=seq, name="hg_scan")
            xl = _hgrn_out(ol, pL, ng_row, w_out, xl, g1, seq=seq, tm=tm_l, name="hg_out")
            if need_ctx:
                xc = _hgrn_out(oc, pc, ng_row, w_out, xc, cg1, seq=nctx, tm=tm_c, name="hg_out_ctx")
        elif kind == 1:
            wq = at_w_qkv[j]
            bq = at_b_qkv[j]
            qd = AT_HEADS * AT_HD
            wq_h = wq[:, :qd].reshape(d, AT_HEADS, AT_HD)
            wk_h = wq[:, qd:qd + AT_KV_DIM].reshape(d, AT_KV_HEADS, AT_HD)
            bq_h = bq[:qd].reshape(AT_HEADS, AT_HD)
            bk_h = bq[qd:qd + AT_KV_DIM].reshape(AT_KV_HEADS, AT_HD)
            w_ext = jnp.concatenate([wq[:, :qd], _rotate_half_cols(wq_h, 2).reshape(d, qd), wq[:, qd:],
                                     _rotate_half_cols(wk_h, 2).reshape(d, AT_KV_DIM)], axis=1).astype(BF16)
            b_ext = jnp.concatenate([bq[:qd], _rotate_half_cols(bq_h, 2).reshape(qd), bq[qd:],
                                     _rotate_half_cols(bk_h, 2).reshape(AT_KV_DIM)]).reshape(1, -1)
            w_out = at_w_out[j].astype(BF16)
            sink = at_sink[j].astype(F32)
            cos2, sin2 = _axial_tables(seq)
            pc = _proj(xc, cmul1, csh1, w_ext, b_ext, seq=nctx, tm=tm_c, name="at_proj_ctx")
            pL = _proj(xl, mul1, sh1, w_ext, b_ext, seq=seq, tm=tm_l, name="at_proj")
            ol = _attn(sink, pL, pc, cos2, sin2, bsz=bsz, seq=seq, nctx=nctx)
            xl = _mm_res(ol, w_out, xl, g1, seq=seq, tm=tm_l, name="at_out")
            if need_ctx:
                oc = _attn_ctx(sink, pc, bsz=bsz, nctx=nctx)
                xc = _mm_res(oc, w_out, xc, cg1, seq=nctx, tm=tm_c, name="at_out_ctx")
        elif kind == 2:
            w_in = rt_w_in[j].astype(BF16)
            w_out = rt_w_out[j].astype(BF16)
            cos1, sin1 = _retention_tables(seq)
            ones_t, zeros_t = jnp.ones((nctx, RT_DK // 2), F32), jnp.zeros((nctx, RT_DK // 2), F32)
            pc = _proj(xc, cmul1, csh1, w_in, zero_b(6 * d), seq=nctx, tm=tm_c, name="rt_proj_ctx")
            pL = _proj(xl, mul1, sh1, w_in, zero_b(6 * d), seq=seq, tm=tm_l, name="rt_proj")
            s0 = jnp.zeros((bsz, 2, RT_HEADS, RT_DK, RT_DV), F32)
            oc, s_ctx = _ret_scan(pc, ones_t, zeros_t, rt_decay_exp[j], s0, bsz=bsz, seq=nctx, name="rt_scan_ctx")
            ol, _ = _ret_scan(pL, cos1, sin1, rt_decay_exp[j], s_ctx, bsz=bsz, seq=seq, name="rt_scan")
            xl = _ret_out(ol, pL, w_out, xl, g1, seq=seq, tm=tm_l, name="rt_out")
            if need_ctx:
                xc = _ret_out(oc, pc, w_out, xc, cg1, seq=nctx, tm=tm_c, name="rt_out_ctx")
        else:
            wts = dict(mix=rw_mix[j], w_rkv=rw_w_rkv[j].astype(BF16), w0=rw_w0[j].reshape(2, 1, d),
                       w1=rw_w1[j].astype(BF16), w2=rw_w2[j].astype(BF16), a0=rw_a0[j].reshape(2, 1, d),
                       a1=rw_a1[j].astype(BF16), a2=rw_a2[j].astype(BF16), g1=rw_g1[j].astype(BF16),
                       g2=rw_g2[j].astype(BF16), k_k=rw_k_k[j].reshape(1, d), k_a=rw_k_a[j].reshape(1, d),
                       r_k=rw_r_k[j].reshape(1, d))
            w_out = rw_w_out[j].astype(BF16)
            ln_w, ln_b = rw_ln_w[j].reshape(1, d), rw_ln_b[j].reshape(1, d)
            rc = _rw_proj(xc, cmul1, csh1, wts, seq=nctx, width=nctx, tm=tm_c, quarters=('l', 'l', 'r', 'r'),
                          name="rw_proj_ctx")
            rl = _rw_proj(xl, mul1, sh1, wts, seq=seq, width=GRID_W, tm=tm_f, quarters=('l', 'r', 'u', 'd'),
                          name="rw_proj")
            s0 = jnp.zeros((bsz, 2, RW_NGROUPS, RW_GROUP, RW_GROUP), F32)
            scan_in = lambda t: (t[0], t[2], t[1], t[5], t[6], t[7])
            yc, s_ctx = _rw_scan(*scan_in(rc), s0, bsz=bsz, seq=nctx, name="rw_scan_ctx")
            yl, _ = _rw_scan(*scan_in(rl), s_ctx, bsz=bsz, seq=seq, name="rw_scan")
            xl = _rw_out(yl, rl[4], rl[3], ln_w, ln_b, w_out, xl, g1, seq=seq, tm=tm_l, name="rw_out")
            if need_ctx:
                xc = _rw_out(yc, rc[4], rc[3], ln_w, ln_b, w_out, xc, cg1, seq=nctx, tm=tm_c, name="rw_out_ctx")

        w_up = ffn_w_up[i].astype(BF16)
        w_down = ffn_w_down[i].astype(BF16)
        conv_w = ffn_conv_w[i].reshape(9, D_FF)
        conv_b = ffn_conv_b[i].reshape(1, D_FF)
        fin_g = final_norm_g.reshape(1, d)
        xl = _ffn(xl, mul2, sh2, w_up, conv_w, conv_b, w_down, g2, fin_g, seq=seq, width=GRID_W, tm=tm_ffn,
                  final_norm=(i == depth - 1), name="ffn")
        if need_ctx:
            xc = _ffn(xc, cmul2, csh2, w_up, conv_w, conv_b, w_down, cg2, fin_g, seq=nctx, width=nctx, tm=tm_c,
                      final_norm=False, name="ffn_ctx")
    return xl.reshape(bsz, seq, d)


def kernel(x, c, ctx, c_ctx, ada_w, ada_b, norm1_g, norm2_g, ffn_w_up, ffn_conv_w, ffn_conv_b, ffn_w_down, hg_w_in, hg_lb_logits, hg_norm_g, hg_w_out, at_w_qkv, at_b_qkv, at_sink, at_w_out, rt_w_in, rt_decay_exp, rt_w_out, rw_mix, rw_w_rkv, rw_w0, rw_w1, rw_w2, rw_a0, rw_a1, rw_a2, rw_g1, rw_g2, rw_k_k, rw_k_a, rw_r_k, rw_ln_w, rw_ln_b, rw_w_out, final_norm_g):
    return _forward(x, c, ctx, c_ctx, ada_w, ada_b, norm1_g, norm2_g, ffn_w_up, ffn_conv_w, ffn_conv_b, ffn_w_down,
                    hg_w_in, hg_lb_logits, hg_norm_g, hg_w_out, at_w_qkv, at_b_qkv, at_sink, at_w_out,
                    rt_w_in, rt_decay_exp, rt_w_out, rw_mix, rw_w_rkv, rw_w0, rw_w1, rw_w2, rw_a0, rw_a1, rw_a2,
                    rw_g1, rw_g2, rw_k_k, rw_k_a, rw_r_k, rw_ln_w, rw_ln_b, rw_w_out, final_norm_g, DEPTH)
```

```python
import functools
import math

import numpy as np
import jax
import jax.numpy as jnp
from jax import lax
from jax.experimental import pallas as pl
from jax.experimental.pallas import tpu as pltpu

F32 = jnp.float32
BF16 = jnp.bfloat16

D_MODEL = 1024
DEPTH = 4
GRID_W = 64
NORM_EPS = 1e-6

HG_DK = 128
HG_HEADS = D_MODEL // HG_DK
HG_CHUNK = 128

AT_HD = 64
AT_HEADS = D_MODEL // AT_HD
AT_KV_HEADS = AT_HEADS // 4
AT_GROUP = 4
AT_KV_DIM = AT_KV_HEADS * AT_HD
AT_BLOCK = 128
ROPE_BASE = 10000.0
AXIS_DIM = AT_HD // 2

RT_DK = 256
RT_HEADS = D_MODEL // RT_DK
RT_DV = 2 * RT_DK
RT_V_DIM = RT_HEADS * RT_DV
RT_CHUNK = 256

RW_HEAD = 64
RW_HEADS = D_MODEL // RW_HEAD
RW_GN_EPS = 64e-5
RW_CHUNK = 64
RW_GROUP_HEADS = 4
RW_GROUP = RW_GROUP_HEADS * RW_HEAD
RW_NGROUPS = D_MODEL // RW_GROUP

D_FF = 2816
FF_CHUNK = 256
FF_DOWN_PARTS = 2

VMEM_LIMIT = 56 * 1024 * 1024


def _cp(*sem):
    return pltpu.CompilerParams(dimension_semantics=sem, vmem_limit_bytes=VMEM_LIMIT)


def _bdot(a, b):
    return jnp.dot(a.astype(BF16), b.astype(BF16), preferred_element_type=F32)


def _bdot_nt(a, b):
    return lax.dot_general(a.astype(BF16), b.astype(BF16), (((1,), (1,)), ((), ())), preferred_element_type=F32)


def _bdot_tn(a, b):
    return lax.dot_general(a.astype(BF16), b.astype(BF16), (((0,), (0,)), ((), ())), preferred_element_type=F32)


def _split3(x):
    hi = x.astype(BF16)
    r1 = x - hi.astype(F32)
    mid = r1.astype(BF16)
    lo = (r1 - mid.astype(F32)).astype(BF16)
    return hi, mid, lo


def _exact_dot(m_bf16, x):
    hi, mid, lo = _split3(x)
    d = lambda v: jnp.dot(m_bf16, v, preferred_element_type=F32)
    return d(hi) + d(mid) + d(lo)


def _exact_dot2(m_bf16, x):
    hi = x.astype(BF16)
    lo = (x - hi.astype(F32)).astype(BF16)
    return jnp.dot(m_bf16, hi, preferred_element_type=F32) + jnp.dot(m_bf16, lo, preferred_element_type=F32)


def _neg_abs(x):
    bits = lax.bitcast_convert_type(x, jnp.uint32) | jnp.uint32(0x80000000)
    return lax.bitcast_convert_type(bits, F32)


def _sigmoid(x):
    return 1.0 / (1.0 + jnp.exp(-x))


def _silu(x):
    return x * _sigmoid(x)


def _rms(x):
    return x * lax.rsqrt(jnp.mean(x * x, axis=-1, keepdims=True) + NORM_EPS)


def _adaln_kernel(c_ref, w_ref, b_ref, o_ref):
    o_ref[...] = _bdot(_silu(c_ref[...]), w_ref[...]) + b_ref[...]


def _adaln(cvecs, ada_w, ada_b):
    depth, d, n = ada_w.shape
    tn = 1536
    return pl.pallas_call(
        _adaln_kernel,
        out_shape=jax.ShapeDtypeStruct((depth, 8, n), F32),
        grid=(depth, n // tn),
        in_specs=[pl.BlockSpec((8, d), lambda l, j: (0, 0)),
                  pl.BlockSpec((None, d, tn), lambda l, j: (l, 0, j)),
                  pl.BlockSpec((None, 1, tn), lambda l, j: (l, 0, j))],
        out_specs=pl.BlockSpec((None, 8, tn), lambda l, j: (l, 0, j)),
        compiler_params=_cp("parallel", "parallel"),
        name="adaln",
    )(cvecs, ada_w, ada_b.reshape(depth, 1, n))


def _proj_kernel(x_ref, mul_ref, add_ref, w_ref, b_ref, *out_refs, n_lo, tn):
    h = (_rms(x_ref[...]) * mul_ref[...] + add_ref[...]).astype(BF16)
    for n0 in range(0, w_ref.shape[1], tn):
        y = jnp.dot(h, w_ref[:, n0:n0 + tn], preferred_element_type=F32) + b_ref[:, n0:n0 + tn]
        if n0 < n_lo:
            out_refs[0][:, n0:n0 + tn] = y.astype(BF16)
        else:
            out_refs[1][:, n0 - n_lo:n0 - n_lo + tn] = y


def _proj(x2d, mul, add, w, bias, *, seq, tm, name, n_f32_tail=0):
    m, d = x2d.shape
    n = w.shape[1]
    n_lo = n - n_f32_tail
    tn = 512 if (n % 512 == 0 and n_lo % 512 == 0) else 256
    tpb = seq // tm
    out_shape = [jax.ShapeDtypeStruct((m, n_lo), BF16)]
    out_specs = [pl.BlockSpec((tm, n_lo), lambda i: (i, 0))]
    if n_f32_tail:
        out_shape.append(jax.ShapeDtypeStruct((m, n_f32_tail), F32))
        out_specs.append(pl.BlockSpec((tm, n_f32_tail), lambda i: (i, 0)))
    res = pl.pallas_call(
        functools.partial(_proj_kernel, n_lo=n_lo, tn=tn),
        out_shape=tuple(out_shape),
        grid=(m // tm,),
        in_specs=[pl.BlockSpec((tm, d), lambda i: (i, 0)),
                  pl.BlockSpec((None, 1, d), lambda i: (i // tpb, 0, 0)),
                  pl.BlockSpec((None, 1, d), lambda i: (i // tpb, 0, 0)),
                  pl.BlockSpec((d, n), lambda i: (0, 0)),
                  pl.BlockSpec((1, n), lambda i: (0, 0))],
        out_specs=tuple(out_specs),
        compiler_params=_cp("parallel"),
        name=name,
    )(x2d, mul, add, w, bias)
    return res if n_f32_tail else res[0]


def _mm_res_kernel(a_ref, w_ref, x_ref, g_ref, o_ref):
    o_ref[...] = x_ref[...] + g_ref[...] * _bdot(a_ref[...], w_ref[...])


def _mm_res(act, w, x2d, gate, *, seq, tm, name):
    m, k = act.shape
    d = w.shape[1]
    tpb = seq // tm
    return pl.pallas_call(
        _mm_res_kernel,
        out_shape=jax.ShapeDtypeStruct((m, d), F32),
        grid=(m // tm,),
        in_specs=[pl.BlockSpec((tm, k), lambda i: (i, 0)),
                  pl.BlockSpec((k, d), lambda i: (0, 0)),
                  pl.BlockSpec((tm, d), lambda i: (i, 0)),
                  pl.BlockSpec((None, 1, d), lambda i: (i // tpb, 0, 0))],
        out_specs=pl.BlockSpec((tm, d), lambda i: (i, 0)),
        compiler_params=_cp("parallel"),
        name=name,
    )(act, w, x2d, gate)


def _ffn_kernel(x_ref, xp_ref, xn_ref, mul_ref, add_ref, wu_ref, cw_ref, cb_ref, wd_ref, g_ref, fg_ref, o_ref,
                act_scr, *, tm, width, tpi, has_rows, final_norm):
    i = pl.program_id(0)
    mul = mul_ref[...]
    add = add_ref[...]
    hmod = lambda x: (_rms(x) * mul + add).astype(BF16)
    x = x_ref[...]
    hm = hmod(x)
    if has_rows:
        first = (i % tpi) == 0
        last = (i % tpi) == tpi - 1
        hp = jnp.where(first, jnp.zeros((), BF16), hmod(xp_ref[...]))
        hn = jnp.where(last, jnp.zeros((), BF16), hmod(xn_ref[...]))
        h_ext = jnp.concatenate([hp, hm, hn], axis=0)
    else:
        h_ext = hm
    col = lax.broadcasted_iota(jnp.int32, (tm, 1), 0) & (width - 1)
    not_left = col != 0
    not_right = col != width - 1
    inv_sqrt2 = 1.0 / math.sqrt(2.0)
    def up(c0):
        return (jnp.dot(h_ext, wu_ref[:, c0:c0 + FF_CHUNK], preferred_element_type=F32),
                jnp.dot(hm, wu_ref[:, D_FF + c0:D_FF + c0 + FF_CHUNK], preferred_element_type=F32))

    starts = list(range(0, D_FF, FF_CHUNK))
    cuts = [starts[(len(starts) * n) // FF_DOWN_PARTS] for n in range(1, FF_DOWN_PARTS)]
    down = lambda lo, hi: jnp.dot(act_scr[:, lo:hi], wd_ref[lo:hi, :], preferred_element_type=F32)
    nxt = up(starts[0])
    y_mm = None
    done = 0
    for n_c, c0 in enumerate(starts):
        sl = slice(c0, c0 + FF_CHUNK)
        u_ext, vv = nxt
        if n_c + 1 < len(starts):
            nxt = up(starts[n_c + 1])
        if c0 in cuts:
            part = down(done, c0)
            y_mm = part if y_mm is None else y_mm + part
            done = c0
        if has_rows:
            rows = (u_ext[:tm], u_ext[width:width + tm], u_ext[2 * width:2 * width + tm])
            colsum = [cw_ref[b:b + 1, sl] * rows[0] + cw_ref[3 + b:4 + b, sl] * rows[1]
                      + cw_ref[6 + b:7 + b, sl] * rows[2] for b in range(3)]
        else:
            colsum = [cw_ref[3 + b:4 + b, sl] * u_ext for b in range(3)]
        acc = (colsum[1] + cb_ref[:, sl]
               + jnp.where(not_left, pltpu.roll(colsum[0], 1, 0), 0.0)
               + jnp.where(not_right, pltpu.roll(colsum[2], tm - 1, 0), 0.0))
        gelu = 0.5 * acc * (1.0 + lax.erf(acc * inv_sqrt2))
        act_scr[:, sl] = (gelu * vv).astype(BF16)
    y = x + g_ref[...] * (y_mm + down(done, D_FF))
    if final_norm:
        y = _rms(y) * fg_ref[...]
    o_ref[...] = y


def _ffn(x2d, mul, add, w_up, conv_w, conv_b, w_down, gate, final_g, *, seq, width, tm, final_norm, name):
    m = x2d.shape[0]
    tpi = seq // tm
    rpt = tm // width
    nrow = m // width
    has_rows = seq > width
    kern = functools.partial(_ffn_kernel, tm=tm, width=width, tpi=tpi, has_rows=has_rows, final_norm=final_norm)
    resident = lambda shape: pl.BlockSpec(shape, lambda i: (0,) * len(shape), pipeline_mode=pl.Buffered(1))
    return pl.pallas_call(
        kern,
        out_shape=jax.ShapeDtypeStruct((m, D_MODEL), F32),
        grid=(m // tm,),
        in_specs=[pl.BlockSpec((tm, D_MODEL), lambda i: (i, 0)),
                  pl.BlockSpec((width, D_MODEL), lambda i: (jnp.maximum(i * rpt - 1, 0), 0)),
                  pl.BlockSpec((width, D_MODEL), lambda i: (jnp.minimum((i + 1) * rpt, nrow - 1), 0)),
                  pl.BlockSpec((None, 1, D_MODEL), lambda i: (i // tpi, 0, 0)),
                  pl.BlockSpec((None, 1, D_MODEL), lambda i: (i // tpi, 0, 0)),
                  resident((D_MODEL, 2 * D_FF)),
                  pl.BlockSpec((9, D_FF), lambda i: (0, 0)),
                  pl.BlockSpec((1, D_FF), lambda i: (0, 0)),
                  resident((D_FF, D_MODEL)),
                  pl.BlockSpec((None, 1, D_MODEL), lambda i: (i // tpi, 0, 0)),
                  pl.BlockSpec((1, D_MODEL), lambda i: (0, 0))],
        out_specs=pl.BlockSpec((tm, D_MODEL), lambda i: (i, 0)),
        scratch_shapes=[pltpu.VMEM((tm, D_FF), BF16)],
        compiler_params=_cp("parallel"),
        name=name,
    )(x2d, x2d, x2d, mul, add, w_up, conv_w, conv_b, w_down, gate, final_g)


def _hgrn_levels(c):
    levels = []
    m = c // 2
    while m >= 1:
        levels.append(m)
        m //= 2
    return levels


def _hgrn_tables(c, rev):
    levels = _hgrn_levels(c)
    msk = np.zeros((len(levels) + 1, c, c), np.float32)
    t = np.arange(c)
    for l, m in enumerate(levels):
        blk = t // (2 * m)
        second = (t % (2 * m)) >= m
        msk[l] = (blk[:, None] == blk[None, :]) & second[:, None] & (~second)[None, :]
    msk[len(levels)] = np.eye(c)
    tri = np.tril(np.ones((c, c), np.float32))
    if rev:
        msk = msk[:, ::-1, ::-1]
        tri = tri.T
    return jnp.asarray(tri, BF16), jnp.asarray(np.ascontiguousarray(msk), F32)


def _level_ref(bcum, m, rev, row):
    c = bcum.shape[0]
    r = m if rev else m - 1
    if 2 * m >= 8:
        return jnp.concatenate([jnp.broadcast_to(bcum[s + r:s + r + 1, :], (2 * m, bcum.shape[1]))
                                for s in range(0, c, 2 * m)], axis=0)
    phase = row & (2 * m - 1)
    out = bcum
    for ph in range(2 * m):
        if ph != r:
            out = jnp.where(phase == ph, pltpu.roll(bcum, (ph - r) % c, 0), out)
    return out


def _hgrn_scan_kernel(q_ref, v_ref, z_ref, lb_ref, tri_ref, m_ref, s0_ref, o_ref, sf_ref, st_scr, *, c, rev):
    j = pl.program_id(0)

    @pl.when(j == 0)
    def _():
        st_scr[...] = s0_ref[...]

    levels = _hgrn_levels(c)
    nl = len(levels)
    lb = lb_ref[...]
    tri = tri_ref[...]
    row = lax.broadcasted_iota(jnp.int32, (c, 1), 0)
    sls = [slice(h * HG_DK, (h + 1) * HG_DK) for h in range(HG_HEADS)]
    nbat = q_ref.shape[0]
    items = [(b, h) for b in range(nbat) for h in range(HG_HEADS)]
    q, kin, v, bcum, btot = [], [], [], [], []
    for b in range(nbat):
        z = z_ref[b]
        logf = jnp.log(lb + (1.0 - lb) * _sigmoid(z))
        kin.append((1.0 - lb) * _sigmoid(-z))
        log2f = logf * (1.0 / math.log(2.0))
        bcum.append(_exact_dot2(tri, log2f))
        btot.append(jnp.sum(log2f, axis=0, keepdims=True))
        q.append(q_ref[b].astype(F32))
        v.append(v_ref[b])
    kin16 = [x.astype(BF16) for x in kin]
    a = [m_ref[nl] * _bdot_nt(q[b][:, sls[h]], kin[b][:, sls[h]]) for b, h in items]
    for l, m in enumerate(levels):
        qf, kf = [], []
        for b in range(nbat):
            f = jnp.exp2(_neg_abs(bcum[b] - _level_ref(bcum[b], m, rev, row))).astype(BF16)
            qf.append(q_ref[b] * f)
            kf.append(kin16[b] * f)
        a = [a[i] + m_ref[l] * _bdot_nt(qf[b][:, sls[h]], kf[b][:, sls[h]]) for i, (b, h) in enumerate(items)]
    qtop = [(q[b] * jnp.exp2(bcum[b])).astype(BF16) for b in range(nbat)]
    kend = [(kin[b] * jnp.exp2(btot[b] - bcum[b])).astype(BF16) for b in range(nbat)]
    dec_tot = [jnp.exp2(btot[b]) for b in range(nbat)]
    for i, (b, h) in enumerate(items):
        sl = sls[h]
        st = st_scr[b, h]
        o_ref[b, :, sl] = (_bdot_nt(qtop[b][:, sl], st) + _bdot(a[i], v[b][:, sl])).astype(o_ref.dtype)
        st_scr[b, h] = st * dec_tot[b][:, sl] + _bdot_tn(v[b][:, sl], kend[b][:, sl])

    @pl.when(j == pl.num_programs(0) - 1)
    def _():
        sf_ref[...] = st_scr[...]


def _hgrn_scan(p, zf, lb_row, s0, *, bsz, seq, name):
    c = min(HG_CHUNK, seq)
    nc = seq // c
    st_shape = (HG_HEADS, HG_DK, HG_DK)
    p3 = p.reshape(bsz, seq, -1)
    z3 = zf.reshape(bsz, seq, -1)
    outs, finals = [], []
    for d in range(2):
        tri, masks = _hgrn_tables(c, rev=bool(d))
        crow = (lambda j: nc - 1 - j) if d else (lambda j: j)
        blk = lambda col, crow=crow: pl.BlockSpec((bsz, c, D_MODEL), lambda j: (0, crow(j), col))
        o, sf = pl.pallas_call(
            functools.partial(_hgrn_scan_kernel, c=c, rev=bool(d)),
            out_shape=(jax.ShapeDtypeStruct((bsz, seq, D_MODEL), BF16),
                       jax.ShapeDtypeStruct((bsz,) + st_shape, F32)),
            grid=(nc,),
            in_specs=[blk(0), blk(1), blk(d),
                      pl.BlockSpec((1, D_MODEL), lambda j: (0, 0)),
                      pl.BlockSpec(tri.shape, lambda j: (0, 0)),
                      pl.BlockSpec(masks.shape, lambda j: (0, 0, 0)),
                      pl.BlockSpec((bsz, None) + st_shape, lambda j, d=d: (0, d, 0, 0, 0))],
            out_specs=(blk(0), pl.BlockSpec((bsz,) + st_shape, lambda j: (0, 0, 0, 0))),
            scratch_shapes=[pltpu.VMEM((bsz,) + st_shape, F32)],
            compiler_params=_cp("arbitrary"),
            name=f"{name}_d{d}",
        )(p3, p3, z3, lb_row, tri, masks, s0)
        outs.append(o.reshape(bsz * seq, D_MODEL))
        finals.append(sf)
    return outs, jnp.stack(finals, axis=1)


def _hgrn_out_kernel(o0_ref, o1_ref, gate_ref, ng_ref, w_ref, x_ref, g_ref, out_ref, act_scr):
    o = o0_ref[...].astype(F32) + o1_ref[...].astype(F32)
    gate = gate_ref[...].astype(F32)
    ng = ng_ref[...]
    for h in range(HG_HEADS):
        sl = slice(h * HG_DK, (h + 1) * HG_DK)
        act_scr[:, sl] = (_rms(o[:, sl]) * ng[:, sl] * _silu(gate[:, sl])).astype(BF16)
    out_ref[...] = x_ref[...] + g_ref[...] * jnp.dot(act_scr[...], w_ref[...], preferred_element_type=F32)


def _hgrn_out(o, p, norm_row, w_out, x2d, gate, *, seq, tm, name):
    m = x2d.shape[0]
    tpb = seq // tm
    return pl.pallas_call(
        _hgrn_out_kernel,
        out_shape=jax.ShapeDtypeStruct((m, D_MODEL), F32),
        grid=(m // tm,),
        in_specs=[pl.BlockSpec((tm, D_MODEL), lambda i: (i, 0)),
                  pl.BlockSpec((tm, D_MODEL), lambda i: (i, 0)),
                  pl.BlockSpec((tm, D_MODEL), lambda i: (i, 2)),
                  pl.BlockSpec((1, D_MODEL), lambda i: (0, 0)),
                  pl.BlockSpec((D_MODEL, D_MODEL), lambda i: (0, 0)),
                  pl.BlockSpec((tm, D_MODEL), lambda i: (i, 0)),
                  pl.BlockSpec((None, 1, D_MODEL), lambda i: (i // tpb, 0, 0))],
        out_specs=pl.BlockSpec((tm, D_MODEL), lambda i: (i, 0)),
        scratch_shapes=[pltpu.VMEM((tm, D_MODEL), BF16)],
        compiler_params=_cp("parallel"),
        name=name,
    )(o[0], o[1], p, norm_row, w_out, x2d, gate)


def _sink_attend(qs, keys, vals, mask, sink_ref, o_ref):
    t = qs[0].shape[0]
    hsl = lambda hd: slice(hd * AT_HD, (hd + 1) * AT_HD)
    items = [(n, g) for n in range(len(qs)) for g in range(AT_KV_HEADS)]
    q4 = [jnp.concatenate([qs[n][:, hsl(g * AT_GROUP + r)] for r in range(AT_GROUP)], axis=0) for n, g in items]
    logits = [_bdot_nt(q4[i], keys[n][:, hsl(g)]) for i, (n, g) in enumerate(items)]
    if mask is not None:
        pen = jnp.where(mask, 0.0, -jnp.inf)
        pen4 = jnp.concatenate([pen] * AT_GROUP, axis=0)
        logits = [lg + pen4 for lg in logits]
    s = [jnp.concatenate([jnp.full((t, 1), sink_ref[g * AT_GROUP + r], F32) for r in range(AT_GROUP)], axis=0)
         for n, g in items]
    mx = [jnp.maximum(jnp.max(lg, axis=-1, keepdims=True), s[i]) for i, lg in enumerate(logits)]
    p = [jnp.exp(lg - mx[i]) for i, lg in enumerate(logits)]
    denom = [jnp.sum(p[i], axis=-1, keepdims=True) + jnp.exp(s[i] - mx[i]) for i in range(len(items))]
    o4 = [_bdot(p[i], vals[n][:, hsl(g)]) / denom[i] for i, (n, g) in enumerate(items)]
    for i, (n, g) in enumerate(items):
        for r in range(AT_GROUP):
            o_ref[n, :, hsl(g * AT_GROUP + r)] = o4[i][r * t:(r + 1) * t].astype(o_ref.dtype)


def _attn_kernel(sink_ref, q_ref, qr_ref, kp_ref, kc_ref, kn_ref, krp_ref, krc_ref, krn_ref, vp_ref, vc_ref, vn_ref,
                 cosp_ref, cosc_ref, cosn_ref, sinp_ref, sinc_ref, sinn_ref, ck_ref, cv_ref, o_ref, *, nctx):
    i = pl.program_id(0)
    nb = pl.num_programs(0)
    bs = range(q_ref.shape[0])
    f32 = lambda x: x.astype(F32)
    wide = lambda t, n: jnp.concatenate([t] * n, axis=1)
    cos_q, sin_q = wide(cosc_ref[...], D_MODEL // 128), wide(sinc_ref[...], D_MODEL // 128)
    rope_k = lambda k_ref, kr_ref, c_ref, s_ref, n: (
        f32(k_ref[n]) * wide(c_ref[...], AT_KV_DIM // 128) + f32(kr_ref[n]) * wide(s_ref[...], AT_KV_DIM // 128)
    ).astype(BF16)
    qs = [((f32(q_ref[n]) * cos_q + f32(qr_ref[n]) * sin_q) * (AT_HD ** -0.5)).astype(BF16) for n in bs]
    keys = [jnp.concatenate([ck_ref[n], rope_k(kp_ref, krp_ref, cosp_ref, sinp_ref, n),
                             rope_k(kc_ref, krc_ref, cosc_ref, sinc_ref, n),
                             rope_k(kn_ref, krn_ref, cosn_ref, sinn_ref, n)], axis=0) for n in bs]
    vals = [jnp.concatenate([cv_ref[n], vp_ref[n], vc_ref[n], vn_ref[n]], axis=0) for n in bs]
    nk = nctx + 3 * AT_BLOCK
    ti = lax.broadcasted_iota(jnp.int32, (AT_BLOCK, nk), 0)
    kj = lax.broadcasted_iota(jnp.int32, (AT_BLOCK, nk), 1) - nctx
    lo = jnp.where(i > 0, 0, AT_BLOCK)
    hi = jnp.where(i < nb - 1, 3 * AT_BLOCK, 2 * AT_BLOCK)
    rel = kj - ti
    mask = (kj < 0) | ((rel >= 0) & (rel <= 2 * AT_BLOCK) & (kj >= lo) & (kj < hi))
    _sink_attend(qs, keys, vals, mask, sink_ref, o_ref)


def _attn(sink, p, p_ctx, cos_t, sin_t, *, bsz, seq, nctx):
    nb = seq // AT_BLOCK
    cur = lambda i: i
    prv = lambda i: jnp.maximum(i - 1, 0)
    nxt = lambda i: jnp.minimum(i + 1, nb - 1)
    kvspec = lambda f, col: pl.BlockSpec((bsz, AT_BLOCK, AT_KV_DIM), lambda i: (0, f(i), col))
    tspec = lambda f: pl.BlockSpec((AT_BLOCK, 128), lambda i: (f(i), 0))
    three = lambda mk: [mk(prv), mk(cur), mk(nxt)]
    p3 = p.reshape(bsz, seq, -1)
    c3 = p_ctx.reshape(bsz, nctx, -1)
    out = pl.pallas_call(
        functools.partial(_attn_kernel, nctx=nctx),
        out_shape=jax.ShapeDtypeStruct((bsz, seq, D_MODEL), BF16),
        grid=(nb,),
        in_specs=[pl.BlockSpec(memory_space=pltpu.SMEM),
                  pl.BlockSpec((bsz, AT_BLOCK, D_MODEL), lambda i: (0, i, 0)),
                  pl.BlockSpec((bsz, AT_BLOCK, D_MODEL), lambda i: (0, i, 1))]
                 + three(lambda f: kvspec(f, 8)) + three(lambda f: kvspec(f, 10)) + three(lambda f: kvspec(f, 9))
                 + three(tspec) + three(tspec)
                 + [pl.BlockSpec((bsz, nctx, AT_KV_DIM), lambda i: (0, 0, 8)),
                    pl.BlockSpec((bsz, nctx, AT_KV_DIM), lambda i: (0, 0, 9))],
        out_specs=pl.BlockSpec((bsz, AT_BLOCK, D_MODEL), lambda i: (0, i, 0)),
        compiler_params=_cp("parallel"),
        name="at_window",
    )(sink, p3, p3, p3, p3, p3, p3, p3, p3, p3, p3, p3, cos_t, cos_t, cos_t, sin_t, sin_t, sin_t, c3, c3)
    return out.reshape(bsz * seq, D_MODEL)


def _attn_ctx_kernel(sink_ref, q_ref, k_ref, v_ref, o_ref):
    _sink_attend([q_ref[0] * (AT_HD ** -0.5)], [k_ref[0]], [v_ref[0]], None, sink_ref, o_ref)


def _attn_ctx(sink, p_ctx, *, bsz, nctx):
    c3 = p_ctx.reshape(bsz, nctx, -1)
    out = pl.pallas_call(
        _attn_ctx_kernel,
        out_shape=jax.ShapeDtypeStruct((bsz, nctx, D_MODEL), BF16),
        grid=(bsz,),
        in_specs=[pl.BlockSpec(memory_space=pltpu.SMEM),
                  pl.BlockSpec((1, nctx, D_MODEL), lambda b: (b, 0, 0)),
                  pl.BlockSpec((1, nctx, AT_KV_DIM), lambda b: (b, 0, 8)),
                  pl.BlockSpec((1, nctx, AT_KV_DIM), lambda b: (b, 0, 9))],
        out_specs=pl.BlockSpec((1, nctx, D_MODEL), lambda b: (b, 0, 0)),
        compiler_params=_cp("parallel"),
        name="at_ctx",
    )(sink, c3, c3, c3)
    return out.reshape(bsz * nctx, D_MODEL)


def _ret_scan_kernel(q_ref, k_ref, v_ref, cos_ref, sin_ref, dm_ref, qin_ref, kout_ref, car_ref, s0_ref,
                     o_ref, sf_ref, st_scr, *, with_intra):
    j = pl.program_id(0)

    @pl.when(j == 0)
    def _():
        st_scr[...] = s0_ref[...]

    half = RT_DK // 2
    cos = jnp.concatenate([cos_ref[...]] * 2, axis=1)
    sin = jnp.concatenate([sin_ref[...]] * 2, axis=1)

    def rope(x):
        rot = jnp.concatenate([-x[:, half:], x[:, :half]], axis=1)
        return x * cos + rot * sin

    sk = [slice(h * RT_DK, (h + 1) * RT_DK) for h in range(RT_HEADS)]
    sv = [slice(h * RT_DV, (h + 1) * RT_DV) for h in range(RT_HEADS)]
    items = [(b, h) for b in range(q_ref.shape[0]) for h in range(RT_HEADS)]
    qh = [rope(q_ref[b, :, sk[h]].astype(F32)) for b, h in items]
    kh = [rope(k_ref[b, :, sk[h]].astype(F32) * (RT_DK ** -0.5)) for b, h in items]
    vh = [v_ref[b, :, sv[h]] for b, h in items]
    st = [st_scr[b, h] for b, h in items]
    o = [_bdot(qh[i] * qin_ref[:, sk[h]], st[i]) for i, (b, h) in enumerate(items)]
    if with_intra:
        s = [_bdot_nt(qh[i], kh[i]) * dm_ref[h] for i, (b, h) in enumerate(items)]
        o = [o[i] + _bdot(s[i], vh[i]) for i in range(len(items))]
    for i, (b, h) in enumerate(items):
        o_ref[b, :, sv[h]] = o[i].astype(o_ref.dtype)
        st_scr[b, h] = car_ref[:, sv[h]] * st[i] + _bdot_tn(kh[i] * kout_ref[:, sk[h]], vh[i])

    @pl.when(j == pl.num_programs(0) - 1)
    def _():
        sf_ref[...] = st_scr[...]


def _ret_tables(decay_exp, c):
    lg = jnp.log1p(-jnp.exp2(decay_exp.astype(F32)))
    idx = jnp.arange(c, dtype=F32)
    pos = jnp.stack([idx, c - 1.0 - idx])
    rel = pos[:, :, None] - pos[:, None, :]
    lgh = lg[:, :, None, None]
    dm = jnp.where(rel[:, None] >= 0, jnp.exp(lgh * jnp.maximum(rel[:, None], 0.0)), 0.0)
    qin = jnp.exp(lg[:, None, :] * (pos[:, :, None] + 1.0))
    kout = jnp.exp(lg[:, None, :] * (c - 1.0 - pos[:, :, None]))
    car = jnp.exp(lg * c)
    qin = jnp.repeat(qin, RT_DK, axis=-1)
    kout = jnp.repeat(kout, RT_DK, axis=-1)
    car = jnp.repeat(car, RT_DV, axis=-1)[:, None, :]
    return dm, qin, kout, car


def _ret_scan(p, cos_t, sin_t, decay_exp, s0, *, bsz, seq, name):
    c = min(RT_CHUNK, seq)
    nc = seq // c
    dm, qin, kout, car = _ret_tables(decay_exp, c)
    dm_both = dm[0] + dm[1]
    st_shape = (RT_HEADS, RT_DK, RT_DV)
    p3 = p.reshape(bsz, seq, -1)
    outs, finals = [], []
    for d in range(2):
        crow = (lambda j: nc - 1 - j) if d else (lambda j: j)
        o, sf = pl.pallas_call(
            functools.partial(_ret_scan_kernel, with_intra=(d == 0)),
            out_shape=(jax.ShapeDtypeStruct((bsz, seq, RT_V_DIM), BF16),
                       jax.ShapeDtypeStruct((bsz,) + st_shape, F32)),
            grid=(nc,),
            in_specs=[pl.BlockSpec((bsz, c, D_MODEL), lambda j, crow=crow: (0, crow(j), 0)),
                      pl.BlockSpec((bsz, c, D_MODEL), lambda j, crow=crow: (0, crow(j), 1)),
                      pl.BlockSpec((bsz, c, RT_V_DIM), lambda j, crow=crow: (0, crow(j), 1)),
                      pl.BlockSpec((c, RT_DK // 2), lambda j, crow=crow: (crow(j), 0)),
                      pl.BlockSpec((c, RT_DK // 2), lambda j, crow=crow: (crow(j), 0)),
                      pl.BlockSpec((RT_HEADS, c, c), lambda j: (0, 0, 0)),
                      pl.BlockSpec((None, c, D_MODEL), lambda j, d=d: (d, 0, 0)),
                      pl.BlockSpec((None, c, D_MODEL), lambda j, d=d: (d, 0, 0)),
                      pl.BlockSpec((None, 1, RT_V_DIM), lambda j, d=d: (d, 0, 0)),
                      pl.BlockSpec((bsz, None) + st_shape, lambda j, d=d: (0, d, 0, 0, 0))],
            out_specs=(pl.BlockSpec((bsz, c, RT_V_DIM), lambda j, crow=crow: (0, crow(j), 0)),
                       pl.BlockSpec((bsz,) + st_shape, lambda j: (0, 0, 0, 0))),
            scratch_shapes=[pltpu.VMEM((bsz,) + st_shape, F32)],
            compiler_params=_cp("arbitrary"),
            name=f"{name}_d{d}",
        )(p3, p3, p3, cos_t, sin_t, dm_both, qin, kout, car, s0)
        outs.append(o.reshape(bsz * seq, RT_V_DIM))
        finals.append(sf)
    return outs, jnp.stack(finals, axis=1)


def _ret_out_kernel(o0_ref, o1_ref, gate_ref, w_ref, x_ref, g_ref, out_ref, act_scr):
    for h in range(RT_HEADS):
        sv = slice(h * RT_DV, (h + 1) * RT_DV)
        o = o0_ref[:, sv].astype(F32) + o1_ref[:, sv].astype(F32)
        act_scr[:, sv] = (_silu(gate_ref[:, sv].astype(F32)) * _rms(o)).astype(BF16)
    out_ref[...] = x_ref[...] + g_ref[...] * jnp.dot(act_scr[...], w_ref[...], preferred_element_type=F32)


def _ret_out(o, p, w_out, x2d, gate, *, seq, tm, name):
    m = x2d.shape[0]
    tpb = seq // tm
    return pl.pallas_call(
        _ret_out_kernel,
        out_shape=jax.ShapeDtypeStruct((m, D_MODEL), F32),
        grid=(m // tm,),
        in_specs=[pl.BlockSpec((tm, RT_V_DIM), lambda i: (i, 0)),
                  pl.BlockSpec((tm, RT_V_DIM), lambda i: (i, 0)),
                  pl.BlockSpec((tm, RT_V_DIM), lambda i: (i, 2)),
                  pl.BlockSpec((RT_V_DIM, D_MODEL), lambda i: (0, 0)),
                  pl.BlockSpec((tm, D_MODEL), lambda i: (i, 0)),
                  pl.BlockSpec((None, 1, D_MODEL), lambda i: (i // tpb, 0, 0))],
        out_specs=pl.BlockSpec((tm, D_MODEL), lambda i: (i, 0)),
        scratch_shapes=[pltpu.VMEM((tm, RT_V_DIM), BF16)],
        compiler_params=_cp("parallel"),
        name=name,
    )(o[0], o[1], p, w_out, x2d, gate)


def _seg_sum(x, e_down, e_up):
    s = jnp.dot(x.astype(BF16), e_down, preferred_element_type=F32)
    s_hi = s.astype(BF16)
    s_lo = (s - s_hi.astype(F32)).astype(BF16)
    return jnp.dot(s_hi, e_up, preferred_element_type=F32) + jnp.dot(s_lo, e_up, preferred_element_type=F32)


def _rw_proj_kernel(x_ref, xp_ref, xn_ref, mul_ref, add_ref, mix_ref, wrkv_ref, w0_ref, w1_ref, w2_ref,
                    a0_ref, a1_ref, a2_ref, g1_ref, g2_ref, kk_ref, ka_ref, rk_ref, ed_ref, eu_ref,
                    r_out, v_out, kn_out, gate_out, bonus_out, lw_out, kd_out, bb_out,
                    *, tm, width, tpi, quarters):
    i = pl.program_id(0)
    first = (i % tpi) == 0
    last = (i % tpi) == tpi - 1
    mul = mul_ref[...]
    add = add_ref[...]
    hmod = lambda x: _rms(x) * mul + add
    hm = hmod(x_ref[...])
    col = lax.broadcasted_iota(jnp.int32, (tm, 1), 0) & (width - 1)
    srcs = {}
    if 'l' in quarters:
        srcs['l'] = jnp.where(col != 0, pltpu.roll(hm, 1, 0), 0.0)
    if 'r' in quarters:
        srcs['r'] = jnp.where(col != width - 1, pltpu.roll(hm, tm - 1, 0), 0.0)
    if 'u' in quarters:
        hp = jnp.where(first, 0.0, hmod(xp_ref[...]))
        srcs['u'] = jnp.concatenate([hp, hm[: tm - width]], axis=0) if tm > width else hp
    if 'd' in quarters:
        hn = jnp.where(last, 0.0, hmod(xn_ref[...]))
        srcs['d'] = jnp.concatenate([hm[width:], hn], axis=0) if tm > width else hn
    qd = D_MODEL // 4
    shifted = jnp.concatenate([srcs[q][:, n * qd:(n + 1) * qd] for n, q in enumerate(quarters)], axis=1)
    xx = shifted - hm
    mixed = lambda n: hm + xx * mix_ref[n:n + 1, :]
    r = _bdot(mixed(0), wrkv_ref[0])
    k = _bdot(mixed(2), wrkv_ref[1])
    v = _bdot(mixed(3), wrkv_ref[2])
    xw = mixed(1)
    xa = mixed(4)
    ed = ed_ref[...]
    eu = eu_ref[...]
    kkh = k * kk_ref[...]
    nrm = jnp.sqrt(_seg_sum(kkh * kkh, ed, eu))
    kn = kkh / jnp.maximum(nrm, 1e-12)
    ksum = jnp.zeros_like(k)
    for z in range(2):
        w_raw = w0_ref[z] + _bdot(jnp.tanh(_bdot(xw, w1_ref[z])), w2_ref[z])
        lw_out[z] = -math.exp(-0.5) * _sigmoid(w_raw)
        a = _sigmoid(a0_ref[z] + _bdot(_bdot(xa, a1_ref[z]), a2_ref[z]))
        kd = k * (1.0 + (a - 1.0) * ka_ref[...])
        kd_out[z] = kd.astype(BF16)
        bb_out[z] = (kn * a).astype(BF16)
        ksum = ksum + kd
    r_out[...] = r.astype(BF16)
    v_out[...] = v.astype(BF16)
    kn_out[...] = kn.astype(BF16)
    gate_out[...] = _bdot(_sigmoid(_bdot(mixed(5), g1_ref[...])), g2_ref[...]).astype(BF16)
    bonus_out[...] = (_seg_sum(r * ksum * rk_ref[...], ed, eu) * v).astype(BF16)


def _seg_mats():
    hid = np.arange(D_MODEL) // RW_HEAD
    e_down = (hid[:, None] == np.arange(128)[None, :]).astype(np.float32)
    return jnp.asarray(e_down, BF16), jnp.asarray(e_down.T, BF16)


def _rw_proj(x2d, mul, add, wts, *, seq, width, tm, quarters, name):
    m = x2d.shape[0]
    tpi = seq // tm
    rpt = tm // width
    nrow = m // width
    e_down, e_up = _seg_mats()
    full = lambda a: pl.BlockSpec(a.shape, lambda i: (0,) * a.ndim)
    consts = [wts['mix'], wts['w_rkv'], wts['w0'], wts['w1'], wts['w2'], wts['a0'], wts['a1'], wts['a2'],
              wts['g1'], wts['g2'], wts['k_k'], wts['k_a'], wts['r_k'], e_down, e_up]
    kern = functools.partial(_rw_proj_kernel, tm=tm, width=width, tpi=tpi, quarters=quarters)
    one = jax.ShapeDtypeStruct((m, D_MODEL), BF16)
    two = jax.ShapeDtypeStruct((2, m, D_MODEL), BF16)
    two_f32 = jax.ShapeDtypeStruct((2, m, D_MODEL), F32)
    ospec1 = pl.BlockSpec((tm, D_MODEL), lambda i: (i, 0))
    ospec2 = pl.BlockSpec((2, tm, D_MODEL), lambda i: (0, i, 0))
    return pl.pallas_call(
        kern,
        out_shape=(one, one, one, one, one, two_f32, two, two),
        grid=(m // tm,),
        in_specs=[pl.BlockSpec((tm, D_MODEL), lambda i: (i, 0)),
                  pl.BlockSpec((width, D_MODEL), lambda i: (jnp.maximum(i * rpt - 1, 0), 0)),
                  pl.BlockSpec((width, D_MODEL), lambda i: (jnp.minimum((i + 1) * rpt, nrow - 1), 0)),
                  pl.BlockSpec((None, 1, D_MODEL), lambda i: (i // tpi, 0, 0)),
                  pl.BlockSpec((None, 1, D_MODEL), lambda i: (i // tpi, 0, 0))] + [full(a) for a in consts],
        out_specs=(ospec1, ospec1, ospec1, ospec1, ospec1, ospec2, ospec2, ospec2),
        compiler_params=_cp("parallel"),
        name=name,
    )(x2d, x2d, x2d, mul, add, *consts)


def _rw_tables(c):
    t = np.arange(c)[:, None]
    s = np.arange(c * RW_GROUP_HEADS)[None, :] % c
    ms = np.stack([s < t, s > t]).astype(np.float32)
    mi = np.stack([s <= t, s >= t]).astype(np.float32)
    tri0 = np.tril(np.ones((c, c)))
    tri = np.stack([tri0, tri0.T])
    hrow = np.arange(c * RW_GROUP_HEADS) // c
    lane_h = np.arange(RW_GROUP) // RW_HEAD
    hmask = hrow[:, None] == lane_h[None, :]
    bdc = hrow[:, None] == hrow[None, :]
    bdm = lane_h[:, None] == lane_h[None, :]
    return (jnp.asarray(tri, BF16), jnp.asarray(ms), jnp.asarray(mi), jnp.asarray(hmask.astype(np.float32), BF16),
            jnp.asarray(bdc.astype(np.float32), BF16), jnp.asarray(bdm.astype(np.float32)))


def _rw_scan_kernel(r_ref, kn_ref, v_ref, lw_ref, kd_ref, bb_ref, tri_ref, ms_ref, mi_ref, hm_ref, bdc_ref, bdm_ref,
                    s0_ref, y_ref, sf_ref, st_scr, *, c):
    j = pl.program_id(1)

    @pl.when(j == 0)
    def _():
        st_scr[...] = s0_ref[...]

    hm = hm_ref[...]
    ms = ms_ref[...]
    mi = mi_ref[...]
    bdc = bdc_ref[...]
    bdm = bdm_ref[...]
    tri = tri_ref[...]
    ncat = c * RW_GROUP_HEADS
    eye = (lax.broadcasted_iota(jnp.int32, (c, ncat), 0)
           == (lax.broadcasted_iota(jnp.int32, (c, ncat), 1) & (c - 1))).astype(F32)
    tile = lambda x: jnp.concatenate([x] * RW_GROUP_HEADS, axis=0)
    stacked = lambda x: tile(x) * hm
    blockdiag = lambda x: tile(x) * bdc
    rows = lambda a, b: jnp.concatenate([a, b], axis=0)
    nsq = int(math.log2(c)) - 1
    b16 = lambda x: x.astype(BF16)
    dot = lambda a, b: jnp.dot(a, b, preferred_element_type=F32)
    dot_nt = lambda a, b: lax.dot_general(a, b, (((1,), (1,)), ((), ())), preferred_element_type=F32)
    sls = [slice(gi * RW_GROUP, (gi + 1) * RW_GROUP) for gi in range(RW_NGROUPS)]

    pre = []
    for b in range(r_ref.shape[0]):
        lw = lw_ref[b]
        g = _exact_dot(tri, lw)
        gtot = jnp.sum(lw, axis=0, keepdims=True)
        e_inv = jnp.exp(-g)
        e_end = jnp.exp(gtot - g)
        kd = kd_ref[b].astype(F32)
        bb = bb_ref[b].astype(F32)
        pre.append(dict(kt=b16(kn_ref[b].astype(F32) * jnp.exp(g - lw)), rt=b16(r_ref[b].astype(F32) * jnp.exp(g)),
                        kh=b16(kd * e_inv), bh=b16(bb * e_inv), ke=b16(kd * e_end), be=b16(bb * e_end),
                        v=v_ref[b], dec=jnp.exp(gtot)))

    items = [(b, gi) for b in range(len(pre)) for gi in range(RW_NGROUPS)]
    ids = range(len(items))
    part = lambda name: [pre[b][name][:, sls[gi]] for b, gi in items]
    kr = [rows(a, b) for a, b in zip(part('kt'), part('rt'))]
    yb = [stacked(x) for x in part('bh')]
    yk = [stacked(x) for x in part('kh')]
    vm = [stacked(x) for x in part('v')]
    by_b = [dot_nt(kr[i], yb[i]) for i in ids]
    by_k = [dot_nt(kr[i], yk[i]) for i in ids]
    n = [-(by_b[i][:c] * ms) for i in ids]
    m_b = [b16(by_b[i][c:] * mi) for i in ids]
    lm_k = [rows(b16(by_k[i][:c] * ms), b16(by_k[i][c:] * mi)) for i in ids]
    p = [eye + n[i] for i in ids]
    nb = [b16(n[i]) for i in ids]
    n = [dot(nb[i], blockdiag(nb[i])) for i in ids]
    for lvl in range(nsq):
        nb = [b16(n[i]) for i in ids]
        nbd = [blockdiag(nb[i]) for i in ids]
        if lvl + 1 < nsq:
            both = [dot(rows(nb[i], b16(p[i])), nbd[i]) for i in ids]
            n = [both[i][:c] for i in ids]
            p = [p[i] + both[i][c:] for i in ids]
        else:
            p = [p[i] + dot(b16(p[i]), nbd[i]) for i in ids]
    st = [st_scr[b, gi] for b, gi in items]
    by_s = [dot_nt(kr[i], b16(st[i])) for i in ids]
    by_v = [dot(lm_k[i], vm[i]) for i in ids]
    u = [dot(b16(p[i]), stacked(b16(by_s[i][:c] + by_v[i][:c]))) for i in ids]
    y = [by_s[i][c:] + by_v[i][c:] - dot(m_b[i], stacked(b16(u[i]))) for i in ids]
    for i, (b, gi) in enumerate(items):
        sl = sls[gi]
        y_ref[b, :, sl] = y[i].astype(y_ref.dtype)
        lhs = rows(pre[b]['v'][:, sl], b16(-u[i]))
        rhs_s = rows(pre[b]['ke'][:, sl], pre[b]['be'][:, sl])
        st_scr[b, gi] = st[i] * pre[b]['dec'][:, sl] + bdm * _bdot_tn(lhs, rhs_s)

    @pl.when(j == pl.num_programs(1) - 1)
    def _():
        sf_ref[...] = st_scr[...]


def _rw_scan(r, kn, v, lw, kd, bb, s0, *, bsz, seq, name):
    c = min(RW_CHUNK, seq)
    nc = seq // c
    tri, ms, mi, hmask, bdc, bdm = _rw_tables(c)
    crow = lambda d, j: j + d * (nc - 1 - 2 * j)
    st_shape = (RW_NGROUPS, RW_GROUP, RW_GROUP)
    shared = pl.BlockSpec((bsz, c, D_MODEL), lambda d, j: (0, crow(d, j), 0))
    perdir = pl.BlockSpec((None, bsz, c, D_MODEL), lambda d, j: (d, 0, crow(d, j), 0))
    stspec = pl.BlockSpec((bsz, None) + st_shape, lambda d, j: (0, d, 0, 0, 0))
    nrow = c * RW_GROUP_HEADS
    b3 = lambda a: a.reshape(bsz, seq, D_MODEL)
    b4 = lambda a: a.reshape(2, bsz, seq, D_MODEL)
    y, sf = pl.pallas_call(
        functools.partial(_rw_scan_kernel, c=c),
        out_shape=(jax.ShapeDtypeStruct((2, bsz, seq, D_MODEL), BF16),
                   jax.ShapeDtypeStruct((bsz, 2) + st_shape, F32)),
        grid=(2, nc),
        in_specs=[shared, shared, shared, perdir, perdir, perdir,
                  pl.BlockSpec((None, c, c), lambda d, j: (d, 0, 0)),
                  pl.BlockSpec((None, c, nrow), lambda d, j: (d, 0, 0)),
                  pl.BlockSpec((None, c, nrow), lambda d, j: (d, 0, 0)),
                  pl.BlockSpec((nrow, RW_GROUP), lambda d, j: (0, 0)),
                  pl.BlockSpec((nrow, nrow), lambda d, j: (0, 0)),
                  pl.BlockSpec((RW_GROUP, RW_GROUP), lambda d, j: (0, 0)),
                  stspec],
        out_specs=(perdir, stspec),
        scratch_shapes=[pltpu.VMEM((bsz,) + st_shape, F32)],
        compiler_params=_cp("parallel", "arbitrary"),
        name=name,
    )(b3(r), b3(kn), b3(v), b4(lw), b4(kd), b4(bb), tri, ms, mi, hmask, bdc, bdm, s0)
    return y.reshape(2, bsz * seq, D_MODEL), sf


def _rw_out_kernel(y0_ref, y1_ref, bonus_ref, gate_ref, lnw_ref, lnb_ref, ed_ref, eu_ref, w_ref, x_ref, g_ref,
                   out_ref):
    y = y0_ref[...].astype(F32) + y1_ref[...].astype(F32)
    ed = ed_ref[...]
    eu = eu_ref[...]
    mu = _seg_sum(y, ed, eu) * (1.0 / RW_HEAD)
    dlt = y - mu
    var = _seg_sum(dlt * dlt, ed, eu) * (1.0 / RW_HEAD)
    yn = dlt * lax.rsqrt(var + RW_GN_EPS) * lnw_ref[...] + lnb_ref[...] + bonus_ref[...].astype(F32)
    out_ref[...] = x_ref[...] + g_ref[...] * _bdot(yn * gate_ref[...].astype(F32), w_ref[...])


def _rw_out(y, bonus, gate_act, ln_w, ln_b, w_out, x2d, gate, *, seq, tm, name):
    m = x2d.shape[0]
    tpb = seq // tm
    e_down, e_up = _seg_mats()
    rowspec = pl.BlockSpec((tm, D_MODEL), lambda i: (i, 0))
    full = lambda a: pl.BlockSpec(a.shape, lambda i: (0,) * a.ndim)
    return pl.pallas_call(
        _rw_out_kernel,
        out_shape=jax.ShapeDtypeStruct((m, D_MODEL), F32),
        grid=(m // tm,),
        in_specs=[pl.BlockSpec((None, tm, D_MODEL), lambda i: (0, i, 0)),
                  pl.BlockSpec((None, tm, D_MODEL), lambda i: (1, i, 0)),
                  rowspec, rowspec, full(ln_w), full(ln_b), full(e_down), full(e_up), full(w_out), rowspec,
                  pl.BlockSpec((None, 1, D_MODEL), lambda i: (i // tpb, 0, 0))],
        out_specs=rowspec,
        compiler_params=_cp("parallel"),
        name=name,
    )(y, y, bonus, gate_act, ln_w, ln_b, e_down, e_up, w_out, x2d, gate)


def _rotate_half_cols(w, n_seg):
    sh = w.shape
    ws = w.reshape(sh[:-1] + (n_seg, 2, sh[-1] // (2 * n_seg)))
    return jnp.concatenate([-ws[..., 1:, :], ws[..., :1, :]], axis=-2).reshape(sh)


def _axial_tables(n_tokens):
    t = jnp.arange(n_tokens)
    row = (t // GRID_W).astype(F32)
    col = (t % GRID_W).astype(F32)
    inv = ROPE_BASE ** (-jnp.arange(0, AXIS_DIM, 2, dtype=F32) / AXIS_DIM)
    ang_r = row[:, None] * inv[None, :]
    ang_c = col[:, None] * inv[None, :]
    ang = jnp.concatenate([ang_r, ang_r, ang_c, ang_c], axis=-1)
    return jnp.tile(jnp.cos(ang), (1, 2)), jnp.tile(jnp.sin(ang), (1, 2))


def _retention_tables(n_tokens):
    t = jnp.arange(n_tokens, dtype=F32)
    inv = ROPE_BASE ** (-jnp.linspace(0.0, 1.0, RT_DK // 2, dtype=F32))
    ang = t[:, None] * inv[None, :]
    return jnp.cos(ang), jnp.sin(ang)


def _forward(x, c, ctx, c_ctx, ada_w, ada_b, norm1_g, norm2_g, ffn_w_up, ffn_conv_w, ffn_conv_b, ffn_w_down,
             hg_w_in, hg_lb_logits, hg_norm_g, hg_w_out, at_w_qkv, at_b_qkv, at_sink, at_w_out,
             rt_w_in, rt_decay_exp, rt_w_out, rw_mix, rw_w_rkv, rw_w0, rw_w1, rw_w2, rw_a0, rw_a1, rw_a2,
             rw_g1, rw_g2, rw_k_k, rw_k_a, rw_r_k, rw_ln_w, rw_ln_b, rw_w_out, final_norm_g, depth):
    bsz, seq, d = x.shape
    nctx = ctx.shape[1]
    tm_l = 512 if seq % 512 == 0 else seq
    tm_f = 512 if seq % 512 == 0 else seq
    tm_ffn = 1024 if seq % 1024 == 0 else tm_l
    tm_c = nctx

    cvecs = jnp.zeros((8, d), F32).at[:bsz].set(c).at[bsz].set(c_ctx)
    mod = _adaln(cvecs, ada_w[:depth], ada_b[:depth])
    lb_all = jnp.cumsum(jax.nn.softmax(hg_lb_logits.astype(F32), axis=0), axis=0)
    zero_b = lambda n: jnp.zeros((1, n), F32)

    xl = x.reshape(bsz * seq, d)
    xc = ctx.reshape(bsz * nctx, d)
    for i in range(depth):
        kind, j = i % 4, i // 4
        need_ctx = i < depth - 1
        ml = mod[i, :bsz].reshape(bsz, 6, 1, d)
        mc = jnp.broadcast_to(mod[i, bsz].reshape(1, 6, 1, d), (bsz, 6, 1, d))
        sh1, sc1, g1, sh2, sc2, g2 = (ml[:, n] for n in range(6))
        csh1, csc1, cg1, csh2, csc2, cg2 = (mc[:, n] for n in range(6))
        mul1, cmul1 = norm1_g[i] * (1.0 + sc1), norm1_g[i] * (1.0 + csc1)
        mul2, cmul2 = norm2_g[i] * (1.0 + sc2), norm2_g[i] * (1.0 + csc2)

        if kind == 0:
            w_in = hg_w_in[j].astype(BF16)
            w_out = hg_w_out[j].astype(BF16)
            lb_row = lb_all[i].reshape(1, d)
            ng_row = jnp.tile(hg_norm_g[j], HG_HEADS).reshape(1, d)
            pc, zc = _proj(xc, cmul1, csh1, w_in, zero_b(5 * d), seq=nctx, tm=tm_c, name="hg_proj_ctx",
                           n_f32_tail=2 * d)
            pL, zl = _proj(xl, mul1, sh1, w_in, zero_b(5 * d), seq=seq, tm=tm_l, name="hg_proj", n_f32_tail=2 * d)
            s0 = jnp.zeros((bsz, 2, HG_HEADS, HG_DK, HG_DK), F32)
            oc, s_ctx = _hgrn_scan(pc, zc, lb_row, s0, bsz=bsz, seq=nctx, name="hg_scan_ctx")
            ol, _ = _hgrn_scan(pL, zl, lb_row, s_ctx, bsz=bsz, seq=seq, name="hg_scan")
            xl = _hgrn_out(ol, pL, ng_row, w_out, xl, g1, seq=seq, tm=tm_l, name="hg_out")
            if need_ctx:
                xc = _hgrn_out(oc, pc, ng_row, w_out, xc, cg1, seq=nctx, tm=tm_c, name="hg_out_ctx")
        elif kind == 1:
            wq = at_w_qkv[j]
            bq = at_b_qkv[j]
            qd = AT_HEADS * AT_HD
            wq_h = wq[:, :qd].reshape(d, AT_HEADS, AT_HD)
            wk_h = wq[:, qd:qd + AT_KV_DIM].reshape(d, AT_KV_HEADS, AT_HD)
            bq_h = bq[:qd].reshape(AT_HEADS, AT_HD)
            bk_h = bq[qd:qd + AT_KV_DIM].reshape(AT_KV_HEADS, AT_HD)
            w_ext = jnp.concatenate([wq[:, :qd], _rotate_half_cols(wq_h, 2).reshape(d, qd), wq[:, qd:],
                                     _rotate_half_cols(wk_h, 2).reshape(d, AT_KV_DIM)], axis=1).astype(BF16)
            b_ext = jnp.concatenate([bq[:qd], _rotate_half_cols(bq_h, 2).reshape(qd), bq[qd:],
                                     _rotate_half_cols(bk_h, 2).reshape(AT_KV_DIM)]).reshape(1, -1)
            w_out = at_w_out[j].astype(BF16)
            sink = at_sink[j].astype(F32)
            cos2, sin2 = _axial_tables(seq)
            pc = _proj(xc, cmul1, csh1, w_ext, b_ext, seq=nctx, tm=tm_c, name="at_proj_ctx")
            pL = _proj(xl, mul1, sh1, w_ext, b_ext, seq=seq, tm=tm_l, name="at_proj")
            ol = _attn(sink, pL, pc, cos2, sin2, bsz=bsz, seq=seq, nctx=nctx)
            xl = _mm_res(ol, w_out, xl, g1, seq=seq, tm=tm_l, name="at_out")
            if need_ctx:
                oc = _attn_ctx(sink, pc, bsz=bsz, nctx=nctx)
                xc = _mm_res(oc, w_out, xc, cg1, seq=nctx, tm=tm_c, name="at_out_ctx")
        elif kind == 2:
            w_in = rt_w_in[j].astype(BF16)
            w_out = rt_w_out[j].astype(BF16)
            cos1, sin1 = _retention_tables(seq)
            ones_t, zeros_t = jnp.ones((nctx, RT_DK // 2), F32), jnp.zeros((nctx, RT_DK // 2), F32)
            pc = _proj(xc, cmul1, csh1, w_in, zero_b(6 * d), seq=nctx, tm=tm_c, name="rt_proj_ctx")
            pL = _proj(xl, mul1, sh1, w_in, zero_b(6 * d), seq=seq, tm=tm_l, name="rt_proj")
            s0 = jnp.zeros((bsz, 2, RT_HEADS, RT_DK, RT_DV), F32)
            oc, s_ctx = _ret_scan(pc, ones_t, zeros_t, rt_decay_exp[j], s0, bsz=bsz, seq=nctx, name="rt_scan_ctx")
            ol, _ = _ret_scan(pL, cos1, sin1, rt_decay_exp[j], s_ctx, bsz=bsz, seq=seq, name="rt_scan")
            xl = _ret_out(ol, pL, w_out, xl, g1, seq=seq, tm=tm_l, name="rt_out")
            if need_ctx:
                xc = _ret_out(oc, pc, w_out, xc, cg1, seq=nctx, tm=tm_c, name="rt_out_ctx")
        else:
            wts = dict(mix=rw_mix[j], w_rkv=rw_w_rkv[j].astype(BF16), w0=rw_w0[j].reshape(2, 1, d),
                       w1=rw_w1[j].astype(BF16), w2=rw_w2[j].astype(BF16), a0=rw_a0[j].reshape(2, 1, d),
                       a1=rw_a1[j].astype(BF16), a2=rw_a2[j].astype(BF16), g1=rw_g1[j].astype(BF16),
                       g2=rw_g2[j].astype(BF16), k_k=rw_k_k[j].reshape(1, d), k_a=rw_k_a[j].reshape(1, d),
                       r_k=rw_r_k[j].reshape(1, d))
            w_out = rw_w_out[j].astype(BF16)
            ln_w, ln_b = rw_ln_w[j].reshape(1, d), rw_ln_b[j].reshape(1, d)
            rc = _rw_proj(xc, cmul1, csh1, wts, seq=nctx, width=nctx, tm=tm_c, quarters=('l', 'l', 'r', 'r'),
                          name="rw_proj_ctx")
            rl = _rw_proj(xl, mul1, sh1, wts, seq=seq, width=GRID_W, tm=tm_f, quarters=('l', 'r', 'u', 'd'),
                          name="rw_proj")
            s0 = jnp.zeros((bsz, 2, RW_NGROUPS, RW_GROUP, RW_GROUP), F32)
            scan_in = lambda t: (t[0], t[2], t[1], t[5], t[6], t[7])
            yc, s_ctx = _rw_scan(*scan_in(rc), s0, bsz=bsz, seq=nctx, name="rw_scan_ctx")
            yl, _ = _rw_scan(*scan_in(rl), s_ctx, bsz=bsz, seq=seq, name="rw_scan")
            xl = _rw_out(yl, rl[4], rl[3], ln_w, ln_b, w_out, xl, g1, seq=seq, tm=tm_l, name="rw_out")
            if need_ctx:
                xc = _rw_out(yc, rc[4], rc[3], ln_w, ln_b, w_out, xc, cg1, seq=nctx, tm=tm_c, name="rw_out_ctx")

        w_up = ffn_w_up[i].astype(BF16)
        w_down = ffn_w_down[i].astype(BF16)
        conv_w = ffn_conv_w[i].reshape(9, D_FF)
        conv_b = ffn_conv_b[i].reshape(1, D_FF)
        fin_g = final_norm_g.reshape(1, d)
        xl = _ffn(xl, mul2, sh2, w_up, conv_w, conv_b, w_down, g2, fin_g, seq=seq, width=GRID_W, tm=tm_ffn,
                  final_norm=(i == depth - 1), name="ffn")
        if need_ctx:
            xc = _ffn(xc, cmul2, csh2, w_up, conv_w, conv_b, w_down, cg2, fin_g, seq=nctx, width=nctx, tm=tm_c,
                      final_norm=False, name="ffn_ctx")
    return xl.reshape(bsz, seq, d)


def kernel(x, c, ctx, c_ctx, ada_w, ada_b, norm1_g, norm2_g, ffn_w_up, ffn_conv_w, ffn_conv_b, ffn_w_down, hg_w_in, hg_lb_logits, hg_norm_g, hg_w_out, at_w_qkv, at_b_qkv, at_sink, at_w_out, rt_w_in, rt_decay_exp, rt_w_out, rw_mix, rw_w_rkv, rw_w0, rw_w1, rw_w2, rw_a0, rw_a1, rw_a2, rw_g1, rw_g2, rw_k_k, rw_k_a, rw_r_k, rw_ln_w, rw_ln_b, rw_w_out, final_norm_g):
    return _forward(x, c, ctx, c_ctx, ada_w, ada_b, norm1_g, norm2_g, ffn_w_up, ffn_conv_w, ffn_conv_b, ffn_w_down,
                    hg_w_in, hg_lb_logits, hg_norm_g, hg_w_out, at_w_qkv, at_b_qkv, at_sink, at_w_out,
                    rt_w_in, rt_decay_exp, rt_w_out, rw_mix, rw_w_rkv, rw_w0, rw_w1, rw_w2, rw_a0, rw_a1, rw_a2,
                    rw_g1, rw_g2, rw_k_k, rw_k_a, rw_r_k, rw_ln_w, rw_ln_b, rw_w_out, final_norm_g, DEPTH)
```

```python
import functools
import math

import numpy as np
import jax
import jax.numpy as jnp
from jax import lax
from jax.experimental import pallas as pl
from jax.experimental.pallas import tpu as pltpu

F32 = jnp.float32
BF16 = jnp.bfloat16

D_MODEL = 1024
DEPTH = 4
GRID_W = 64
NORM_EPS = 1e-6

HG_DK = 128
HG_HEADS = D_MODEL // HG_DK
HG_CHUNK = 128

AT_HD = 64
AT_HEADS = D_MODEL // AT_HD
AT_KV_HEADS = AT_HEADS // 4
AT_GROUP = 4
AT_KV_DIM = AT_KV_HEADS * AT_HD
AT_BLOCK = 128
ROPE_BASE = 10000.0
AXIS_DIM = AT_HD // 2

RT_DK = 256
RT_HEADS = D_MODEL // RT_DK
RT_DV = 2 * RT_DK
RT_V_DIM = RT_HEADS * RT_DV
RT_CHUNK = 256

RW_HEAD = 64
RW_HEADS = D_MODEL // RW_HEAD
RW_GN_EPS = 64e-5
RW_CHUNK = 64
RW_GROUP_HEADS = 4
RW_GROUP = RW_GROUP_HEADS * RW_HEAD
RW_NGROUPS = D_MODEL // RW_GROUP

D_FF = 2816
FF_CHUNK = 256
FF_DOWN_PARTS = 2

VMEM_LIMIT = 56 * 1024 * 1024


def _cp(*sem):
    return pltpu.CompilerParams(dimension_semantics=sem, vmem_limit_bytes=VMEM_LIMIT)


def _bdot(a, b):
    return jnp.dot(a.astype(BF16), b.astype(BF16), preferred_element_type=F32)


def _bdot_nt(a, b):
    return lax.dot_general(a.astype(BF16), b.astype(BF16), (((1,), (1,)), ((), ())), preferred_element_type=F32)


def _bdot_tn(a, b):
    return lax.dot_general(a.astype(BF16), b.astype(BF16), (((0,), (0,)), ((), ())), preferred_element_type=F32)


def _split3(x):
    hi = x.astype(BF16)
    r1 = x - hi.astype(F32)
    mid = r1.astype(BF16)
    lo = (r1 - mid.astype(F32)).astype(BF16)
    return hi, mid, lo


def _exact_dot(m_bf16, x):
    hi, mid, lo = _split3(x)
    d = lambda v: jnp.dot(m_bf16, v, preferred_element_type=F32)
    return d(hi) + d(mid) + d(lo)


def _exact_dot2(m_bf16, x):
    hi = x.astype(BF16)
    lo = (x - hi.astype(F32)).astype(BF16)
    return jnp.dot(m_bf16, hi, preferred_element_type=F32) + jnp.dot(m_bf16, lo, preferred_element_type=F32)


def _neg_abs(x):
    bits = lax.bitcast_convert_type(x, jnp.uint32) | jnp.uint32(0x80000000)
    return lax.bitcast_convert_type(bits, F32)


def _sigmoid(x):
    return 1.0 / (1.0 + jnp.exp(-x))


def _silu(x):
    return x * _sigmoid(x)


def _rms(x):
    return x * lax.rsqrt(jnp.mean(x * x, axis=-1, keepdims=True) + NORM_EPS)


def _adaln_kernel(c_ref, w_ref, b_ref, o_ref):
    o_ref[...] = _bdot(_silu(c_ref[...]), w_ref[...]) + b_ref[...]


def _adaln(cvecs, ada_w, ada_b):
    depth, d, n = ada_w.shape
    tn = 1536
    return pl.pallas_call(
        _adaln_kernel,
        out_shape=jax.ShapeDtypeStruct((depth, 8, n), F32),
        grid=(depth, n // tn),
        in_specs=[pl.BlockSpec((8, d), lambda l, j: (0, 0)),
                  pl.BlockSpec((None, d, tn), lambda l, j: (l, 0, j)),
                  pl.BlockSpec((None, 1, tn), lambda l, j: (l, 0, j))],
        out_specs=pl.BlockSpec((None, 8, tn), lambda l, j: (l, 0, j)),
        compiler_params=_cp("parallel", "parallel"),
        name="adaln",
    )(cvecs, ada_w, ada_b.reshape(depth, 1, n))


def _proj_kernel(x_ref, mul_ref, add_ref, w_ref, b_ref, *out_refs, n_lo, tn):
    h = (_rms(x_ref[...]) * mul_ref[...] + add_ref[...]).astype(BF16)
    for n0 in range(0, w_ref.shape[1], tn):
        y = jnp.dot(h, w_ref[:, n0:n0 + tn], preferred_element_type=F32) + b_ref[:, n0:n0 + tn]
        if n0 < n_lo:
            out_refs[0][:, n0:n0 + tn] = y.astype(BF16)
        else:
            out_refs[1][:, n0 - n_lo:n0 - n_lo + tn] = y


def _proj(x2d, mul, add, w, bias, *, seq, tm, name, n_f32_tail=0):
    m, d = x2d.shape
    n = w.shape[1]
    n_lo = n - n_f32_tail
    tn = 512 if (n % 512 == 0 and n_lo % 512 == 0) else 256
    tpb = seq // tm
    out_shape = [jax.ShapeDtypeStruct((m, n_lo), BF16)]
    out_specs = [pl.BlockSpec((tm, n_lo), lambda i: (i, 0))]
    if n_f32_tail:
        out_shape.append(jax.ShapeDtypeStruct((m, n_f32_tail), F32))
        out_specs.append(pl.BlockSpec((tm, n_f32_tail), lambda i: (i, 0)))
    res = pl.pallas_call(
        functools.partial(_proj_kernel, n_lo=n_lo, tn=tn),
        out_shape=tuple(out_shape),
        grid=(m // tm,),
        in_specs=[pl.BlockSpec((tm, d), lambda i: (i, 0)),
                  pl.BlockSpec((None, 1, d), lambda i: (i // tpb, 0, 0)),
                  pl.BlockSpec((None, 1, d), lambda i: (i // tpb, 0, 0)),
                  pl.BlockSpec((d, n), lambda i: (0, 0), pipeline_mode=pl.Buffered(1)),
                  pl.BlockSpec((1, n), lambda i: (0, 0))],
        out_specs=tuple(out_specs),
        compiler_params=_cp("parallel"),
        name=name,
    )(x2d, mul, add, w, bias)
    return res if n_f32_tail else res[0]


def _mm_res_kernel(a_ref, w_ref, x_ref, g_ref, o_ref):
    o_ref[...] = x_ref[...] + g_ref[...] * _bdot(a_ref[...], w_ref[...])


def _mm_res(act, w, x2d, gate, *, seq, tm, name):
    m, k = act.shape
    d = w.shape[1]
    tpb = seq // tm
    return pl.pallas_call(
        _mm_res_kernel,
        out_shape=jax.ShapeDtypeStruct((m, d), F32),
        grid=(m // tm,),
        in_specs=[pl.BlockSpec((tm, k), lambda i: (i, 0)),
                  pl.BlockSpec((k, d), lambda i: (0, 0)),
                  pl.BlockSpec((tm, d), lambda i: (i, 0)),
                  pl.BlockSpec((None, 1, d), lambda i: (i // tpb, 0, 0))],
        out_specs=pl.BlockSpec((tm, d), lambda i: (i, 0)),
        compiler_params=_cp("parallel"),
        name=name,
    )(act, w, x2d, gate)


def _ffn_kernel(x_ref, xp_ref, xn_ref, mul_ref, add_ref, wu_ref, cw_ref, cb_ref, wd_ref, g_ref, fg_ref, o_ref,
                act_scr, *, tm, width, tpi, has_rows, final_norm):
    i = pl.program_id(0)
    mul = mul_ref[...]
    add = add_ref[...]
    hmod = lambda x: (_rms(x) * mul + add).astype(BF16)
    x = x_ref[...]
    hm = hmod(x)
    if has_rows:
        first = (i % tpi) == 0
        last = (i % tpi) == tpi - 1
        hp = jnp.where(first, jnp.zeros((), BF16), hmod(xp_ref[...]))
        hn = jnp.where(last, jnp.zeros((), BF16), hmod(xn_ref[...]))
        h_ext = jnp.concatenate([hp, hm, hn], axis=0)
    else:
        h_ext = hm
    col = lax.broadcasted_iota(jnp.int32, (tm, 1), 0) & (width - 1)
    not_left = col != 0
    not_right = col != width - 1
    inv_sqrt2 = 1.0 / math.sqrt(2.0)
    def up(c0):
        return (jnp.dot(h_ext, wu_ref[:, c0:c0 + FF_CHUNK], preferred_element_type=F32),
                jnp.dot(hm, wu_ref[:, D_FF + c0:D_FF + c0 + FF_CHUNK], preferred_element_type=F32))

    starts = list(range(0, D_FF, FF_CHUNK))
    cuts = [starts[(len(starts) * n) // FF_DOWN_PARTS] for n in range(1, FF_DOWN_PARTS)]
    down = lambda lo, hi: jnp.dot(act_scr[:, lo:hi], wd_ref[lo:hi, :], preferred_element_type=F32)
    nxt = up(starts[0])
    y_mm = None
    done = 0
    for n_c, c0 in enumerate(starts):
        sl = slice(c0, c0 + FF_CHUNK)
        u_ext, vv = nxt
        if n_c + 1 < len(starts):
            nxt = up(starts[n_c + 1])
        if c0 in cuts:
            part = down(done, c0)
            y_mm = part if y_mm is None else y_mm + part
            done = c0
        if has_rows:
            rows = (u_ext[:tm], u_ext[width:width + tm], u_ext[2 * width:2 * width + tm])
            colsum = [cw_ref[b:b + 1, sl] * rows[0] + cw_ref[3 + b:4 + b, sl] * rows[1]
                      + cw_ref[6 + b:7 + b, sl] * rows[2] for b in range(3)]
        else:
            colsum = [cw_ref[3 + b:4 + b, sl] * u_ext for b in range(3)]
        acc = (colsum[1] + cb_ref[:, sl]
               + jnp.where(not_left, pltpu.roll(colsum[0], 1, 0), 0.0)
               + jnp.where(not_right, pltpu.roll(colsum[2], tm - 1, 0), 0.0))
        gelu = 0.5 * acc * (1.0 + lax.erf(acc * inv_sqrt2))
        act_scr[:, sl] = (gelu * vv).astype(BF16)
    y = x + g_ref[...] * (y_mm + down(done, D_FF))
    if final_norm:
        y = _rms(y) * fg_ref[...]
    o_ref[...] = y


def _ffn(x2d, mul, add, w_up, conv_w, conv_b, w_down, gate, final_g, *, seq, width, tm, final_norm, name):
    m = x2d.shape[0]
    tpi = seq // tm
    rpt = tm // width
    nrow = m // width
    has_rows = seq > width
    kern = functools.partial(_ffn_kernel, tm=tm, width=width, tpi=tpi, has_rows=has_rows, final_norm=final_norm)
    resident = lambda shape: pl.BlockSpec(shape, lambda i: (0,) * len(shape), pipeline_mode=pl.Buffered(1))
    return pl.pallas_call(
        kern,
        out_shape=jax.ShapeDtypeStruct((m, D_MODEL), F32),
        grid=(m // tm,),
        in_specs=[pl.BlockSpec((tm, D_MODEL), lambda i: (i, 0)),
                  pl.BlockSpec((width, D_MODEL), lambda i: (jnp.maximum(i * rpt - 1, 0), 0)),
                  pl.BlockSpec((width, D_MODEL), lambda i: (jnp.minimum((i + 1) * rpt, nrow - 1), 0)),
                  pl.BlockSpec((None, 1, D_MODEL), lambda i: (i // tpi, 0, 0)),
                  pl.BlockSpec((None, 1, D_MODEL), lambda i: (i // tpi, 0, 0)),
                  resident((D_MODEL, 2 * D_FF)),
                  pl.BlockSpec((9, D_FF), lambda i: (0, 0)),
                  pl.BlockSpec((1, D_FF), lambda i: (0, 0)),
                  resident((D_FF, D_MODEL)),
                  pl.BlockSpec((None, 1, D_MODEL), lambda i: (i // tpi, 0, 0)),
                  pl.BlockSpec((1, D_MODEL), lambda i: (0, 0))],
        out_specs=pl.BlockSpec((tm, D_MODEL), lambda i: (i, 0)),
        scratch_shapes=[pltpu.VMEM((tm, D_FF), BF16)],
        compiler_params=_cp("parallel"),
        name=name,
    )(x2d, x2d, x2d, mul, add, w_up, conv_w, conv_b, w_down, gate, final_g)


def _hgrn_levels(c):
    levels = []
    m = c // 2
    while m >= 1:
        levels.append(m)
        m //= 2
    return levels


def _hgrn_tables(c, rev):
    levels = _hgrn_levels(c)
    msk = np.zeros((len(levels) + 1, c, c), np.float32)
    t = np.arange(c)
    for l, m in enumerate(levels):
        blk = t // (2 * m)
        second = (t % (2 * m)) >= m
        msk[l] = (blk[:, None] == blk[None, :]) & second[:, None] & (~second)[None, :]
    msk[len(levels)] = np.eye(c)
    tri = np.tril(np.ones((c, c), np.float32))
    if rev:
        msk = msk[:, ::-1, ::-1]
        tri = tri.T
    return jnp.asarray(tri, BF16), jnp.asarray(np.ascontiguousarray(msk), F32)


def _level_ref(bcum, m, rev, row):
    c = bcum.shape[0]
    r = m if rev else m - 1
    if 2 * m >= 8:
        return jnp.concatenate([jnp.broadcast_to(bcum[s + r:s + r + 1, :], (2 * m, bcum.shape[1]))
                                for s in range(0, c, 2 * m)], axis=0)
    phase = row & (2 * m - 1)
    out = bcum
    for ph in range(2 * m):
        if ph != r:
            out = jnp.where(phase == ph, pltpu.roll(bcum, (ph - r) % c, 0), out)
    return out


def _hgrn_scan_kernel(q_ref, v_ref, z_ref, lb_ref, tri_ref, m_ref, s0_ref, o_ref, sf_ref, st_scr, *, c, rev):
    j = pl.program_id(0)

    @pl.when(j == 0)
    def _():
        st_scr[...] = s0_ref[...]

    levels = _hgrn_levels(c)
    nl = len(levels)
    lb = lb_ref[...]
    tri = tri_ref[...]
    row = lax.broadcasted_iota(jnp.int32, (c, 1), 0)
    sls = [slice(h * HG_DK, (h + 1) * HG_DK) for h in range(HG_HEADS)]
    nbat = q_ref.shape[0]
    items = [(b, h) for b in range(nbat) for h in range(HG_HEADS)]
    q, kin, v, bcum, btot = [], [], [], [], []
    for b in range(nbat):
        z = z_ref[b]
        logf = jnp.log(lb + (1.0 - lb) * _sigmoid(z))
        kin.append((1.0 - lb) * _sigmoid(-z))
        log2f = logf * (1.0 / math.log(2.0))
        bcum.append(_exact_dot2(tri, log2f))
        btot.append(jnp.sum(log2f, axis=0, keepdims=True))
        q.append(q_ref[b].astype(F32))
        v.append(v_ref[b])
    kin16 = [x.astype(BF16) for x in kin]
    a = [m_ref[nl] * _bdot_nt(q[b][:, sls[h]], kin[b][:, sls[h]]) for b, h in items]
    for l, m in enumerate(levels):
        qf, kf = [], []
        for b in range(nbat):
            f = jnp.exp2(_neg_abs(bcum[b] - _level_ref(bcum[b], m, rev, row))).astype(BF16)
            qf.append(q_ref[b] * f)
            kf.append(kin16[b] * f)
        a = [a[i] + m_ref[l] * _bdot_nt(qf[b][:, sls[h]], kf[b][:, sls[h]]) for i, (b, h) in enumerate(items)]
    qtop = [(q[b] * jnp.exp2(bcum[b])).astype(BF16) for b in range(nbat)]
    kend = [(kin[b] * jnp.exp2(btot[b] - bcum[b])).astype(BF16) for b in range(nbat)]
    dec_tot = [jnp.exp2(btot[b]) for b in range(nbat)]
    for i, (b, h) in enumerate(items):
        sl = sls[h]
        st = st_scr[b, h]
        o_ref[b, :, sl] = (_bdot_nt(qtop[b][:, sl], st) + _bdot(a[i], v[b][:, sl])).astype(o_ref.dtype)
        st_scr[b, h] = st * dec_tot[b][:, sl] + _bdot_tn(v[b][:, sl], kend[b][:, sl])

    @pl.when(j == pl.num_programs(0) - 1)
    def _():
        sf_ref[...] = st_scr[...]


def _hgrn_scan(p, zf, lb_row, s0, *, bsz, seq, name):
    c = min(HG_CHUNK, seq)
    nc = seq // c
    st_shape = (HG_HEADS, HG_DK, HG_DK)
    p3 = p.reshape(bsz, seq, -1)
    z3 = zf.reshape(bsz, seq, -1)
    outs, finals = [], []
    for d in range(2):
        tri, masks = _hgrn_tables(c, rev=bool(d))
        crow = (lambda j: nc - 1 - j) if d else (lambda j: j)
        blk = lambda col, crow=crow: pl.BlockSpec((bsz, c, D_MODEL), lambda j: (0, crow(j), col))
        o, sf = pl.pallas_call(
            functools.partial(_hgrn_scan_kernel, c=c, rev=bool(d)),
            out_shape=(jax.ShapeDtypeStruct((bsz, seq, D_MODEL), BF16),
                       jax.ShapeDtypeStruct((bsz,) + st_shape, F32)),
            grid=(nc,),
            in_specs=[blk(0), blk(1), blk(d),
                      pl.BlockSpec((1, D_MODEL), lambda j: (0, 0)),
                      pl.BlockSpec(tri.shape, lambda j: (0, 0)),
                      pl.BlockSpec(masks.shape, lambda j: (0, 0, 0)),
                      pl.BlockSpec((bsz, None) + st_shape, lambda j, d=d: (0, d, 0, 0, 0))],
            out_specs=(blk(0), pl.BlockSpec((bsz,) + st_shape, lambda j: (0, 0, 0, 0))),
            scratch_shapes=[pltpu.VMEM((bsz,) + st_shape, F32)],
            compiler_params=_cp("arbitrary"),
            name=f"{name}_d{d}",
        )(p3, p3, z3, lb_row, tri, masks, s0)
        outs.append(o.reshape(bsz * seq, D_MODEL))
        finals.append(sf)
    return outs, jnp.stack(finals, axis=1)


def _hgrn_out_kernel(o0_ref, o1_ref, gate_ref, ng_ref, w_ref, x_ref, g_ref, out_ref, act_scr):
    o = o0_ref[...].astype(F32) + o1_ref[...].astype(F32)
    gate = gate_ref[...].astype(F32)
    ng = ng_ref[...]
    for h in range(HG_HEADS):
        sl = slice(h * HG_DK, (h + 1) * HG_DK)
        act_scr[:, sl] = (_rms(o[:, sl]) * ng[:, sl] * _silu(gate[:, sl])).astype(BF16)
    out_ref[...] = x_ref[...] + g_ref[...] * jnp.dot(act_scr[...], w_ref[...], preferred_element_type=F32)


def _hgrn_out(o, p, norm_row, w_out, x2d, gate, *, seq, tm, name):
    m = x2d.shape[0]
    tpb = seq // tm
    return pl.pallas_call(
        _hgrn_out_kernel,
        out_shape=jax.ShapeDtypeStruct((m, D_MODEL), F32),
        grid=(m // tm,),
        in_specs=[pl.BlockSpec((tm, D_MODEL), lambda i: (i, 0)),
                  pl.BlockSpec((tm, D_MODEL), lambda i: (i, 0)),
                  pl.BlockSpec((tm, D_MODEL), lambda i: (i, 2)),
                  pl.BlockSpec((1, D_MODEL), lambda i: (0, 0)),
                  pl.BlockSpec((D_MODEL, D_MODEL), lambda i: (0, 0)),
                  pl.BlockSpec((tm, D_MODEL), lambda i: (i, 0)),
                  pl.BlockSpec((None, 1, D_MODEL), lambda i: (i // tpb, 0, 0))],
        out_specs=pl.BlockSpec((tm, D_MODEL), lambda i: (i, 0)),
        scratch_shapes=[pltpu.VMEM((tm, D_MODEL), BF16)],
        compiler_params=_cp("parallel"),
        name=name,
    )(o[0], o[1], p, norm_row, w_out, x2d, gate)


def _sink_attend(qs, keys, vals, mask, sink_ref, o_ref):
    t = qs[0].shape[0]
    hsl = lambda hd: slice(hd * AT_HD, (hd + 1) * AT_HD)
    items = [(n, g) for n in range(len(qs)) for g in range(AT_KV_HEADS)]
    q4 = [jnp.concatenate([qs[n][:, hsl(g * AT_GROUP + r)] for r in range(AT_GROUP)], axis=0) for n, g in items]
    logits = [_bdot_nt(q4[i], keys[n][:, hsl(g)]) for i, (n, g) in enumerate(items)]
    if mask is not None:
        pen = jnp.where(mask, 0.0, -jnp.inf)
        pen4 = jnp.concatenate([pen] * AT_GROUP, axis=0)
        logits = [lg + pen4 for lg in logits]
    s = [jnp.concatenate([jnp.full((t, 1), sink_ref[g * AT_GROUP + r], F32) for r in range(AT_GROUP)], axis=0)
         for n, g in items]
    mx = [jnp.maximum(jnp.max(lg, axis=-1, keepdims=True), s[i]) for i, lg in enumerate(logits)]
    p = [jnp.exp(lg - mx[i]) for i, lg in enumerate(logits)]
    denom = [jnp.sum(p[i], axis=-1, keepdims=True) + jnp.exp(s[i] - mx[i]) for i in range(len(items))]
    o4 = [_bdot(p[i], vals[n][:, hsl(g)]) / denom[i] for i, (n, g) in enumerate(items)]
    for i, (n, g) in enumerate(items):
        for r in range(AT_GROUP):
            o_ref[n, :, hsl(g * AT_GROUP + r)] = o4[i][r * t:(r + 1) * t].astype(o_ref.dtype)


def _attn_kernel(sink_ref, q_ref, qr_ref, kp_ref, kc_ref, kn_ref, krp_ref, krc_ref, krn_ref, vp_ref, vc_ref, vn_ref,
                 cosp_ref, cosc_ref, cosn_ref, sinp_ref, sinc_ref, sinn_ref, ck_ref, cv_ref, o_ref, *, nctx):
    i = pl.program_id(0)
    nb = pl.num_programs(0)
    bs = range(q_ref.shape[0])
    f32 = lambda x: x.astype(F32)
    wide = lambda t, n: jnp.concatenate([t] * n, axis=1)
    cos_q, sin_q = wide(cosc_ref[...], D_MODEL // 128), wide(sinc_ref[...], D_MODEL // 128)
    rope_k = lambda k_ref, kr_ref, c_ref, s_ref, n: (
        f32(k_ref[n]) * wide(c_ref[...], AT_KV_DIM // 128) + f32(kr_ref[n]) * wide(s_ref[...], AT_KV_DIM // 128)
    ).astype(BF16)
    qs = [((f32(q_ref[n]) * cos_q + f32(qr_ref[n]) * sin_q) * (AT_HD ** -0.5)).astype(BF16) for n in bs]
    keys = [jnp.concatenate([ck_ref[n], rope_k(kp_ref, krp_ref, cosp_ref, sinp_ref, n),
                             rope_k(kc_ref, krc_ref, cosc_ref, sinc_ref, n),
                             rope_k(kn_ref, krn_ref, cosn_ref, sinn_ref, n)], axis=0) for n in bs]
    vals = [jnp.concatenate([cv_ref[n], vp_ref[n], vc_ref[n], vn_ref[n]], axis=0) for n in bs]
    nk = nctx + 3 * AT_BLOCK
    ti = lax.broadcasted_iota(jnp.int32, (AT_BLOCK, nk), 0)
    kj = lax.broadcasted_iota(jnp.int32, (AT_BLOCK, nk), 1) - nctx
    lo = jnp.where(i > 0, 0, AT_BLOCK)
    hi = jnp.where(i < nb - 1, 3 * AT_BLOCK, 2 * AT_BLOCK)
    rel = kj - ti
    mask = (kj < 0) | ((rel >= 0) & (rel <= 2 * AT_BLOCK) & (kj >= lo) & (kj < hi))
    _sink_attend(qs, keys, vals, mask, sink_ref, o_ref)


def _attn(sink, p, p_ctx, cos_t, sin_t, *, bsz, seq, nctx):
    nb = seq // AT_BLOCK
    cur = lambda i: i
    prv = lambda i: jnp.maximum(i - 1, 0)
    nxt = lambda i: jnp.minimum(i + 1, nb - 1)
    kvspec = lambda f, col: pl.BlockSpec((bsz, AT_BLOCK, AT_KV_DIM), lambda i: (0, f(i), col))
    tspec = lambda f: pl.BlockSpec((AT_BLOCK, 128), lambda i: (f(i), 0))
    three = lambda mk: [mk(prv), mk(cur), mk(nxt)]
    p3 = p.reshape(bsz, seq, -1)
    c3 = p_ctx.reshape(bsz, nctx, -1)
    out = pl.pallas_call(
        functools.partial(_attn_kernel, nctx=nctx),
        out_shape=jax.ShapeDtypeStruct((bsz, seq, D_MODEL), BF16),
        grid=(nb,),
        in_specs=[pl.BlockSpec(memory_space=pltpu.SMEM),
                  pl.BlockSpec((bsz, AT_BLOCK, D_MODEL), lambda i: (0, i, 0)),
                  pl.BlockSpec((bsz, AT_BLOCK, D_MODEL), lambda i: (0, i, 1))]
                 + three(lambda f: kvspec(f, 8)) + three(lambda f: kvspec(f, 10)) + three(lambda f: kvspec(f, 9))
                 + three(tspec) + three(tspec)
                 + [pl.BlockSpec((bsz, nctx, AT_KV_DIM), lambda i: (0, 0, 8)),
                    pl.BlockSpec((bsz, nctx, AT_KV_DIM), lambda i: (0, 0, 9))],
        out_specs=pl.BlockSpec((bsz, AT_BLOCK, D_MODEL), lambda i: (0, i, 0)),
        compiler_params=_cp("parallel"),
        name="at_window",
    )(sink, p3, p3, p3, p3, p3, p3, p3, p3, p3, p3, p3, cos_t, cos_t, cos_t, sin_t, sin_t, sin_t, c3, c3)
    return out.reshape(bsz * seq, D_MODEL)


def _attn_ctx_kernel(sink_ref, q_ref, k_ref, v_ref, o_ref):
    _sink_attend([q_ref[0] * (AT_HD ** -0.5)], [k_ref[0]], [v_ref[0]], None, sink_ref, o_ref)


def _attn_ctx(sink, p_ctx, *, bsz, nctx):
    c3 = p_ctx.reshape(bsz, nctx, -1)
    out = pl.pallas_call(
        _attn_ctx_kernel,
        out_shape=jax.ShapeDtypeStruct((bsz, nctx, D_MODEL), BF16),
        grid=(bsz,),
        in_specs=[pl.BlockSpec(memory_space=pltpu.SMEM),
                  pl.BlockSpec((1, nctx, D_MODEL), lambda b: (b, 0, 0)),
                  pl.BlockSpec((1, nctx, AT_KV_DIM), lambda b: (b, 0, 8)),
                  pl.BlockSpec((1, nctx, AT_KV_DIM), lambda b: (b, 0, 9))],
        out_specs=pl.BlockSpec((1, nctx, D_MODEL), lambda b: (b, 0, 0)),
        compiler_params=_cp("parallel"),
        name="at_ctx",
    )(sink, c3, c3, c3)
    return out.reshape(bsz * nctx, D_MODEL)


def _ret_scan_kernel(q_ref, k_ref, v_ref, cos_ref, sin_ref, dm_ref, qin_ref, kout_ref, car_ref, s0_ref,
                     o_ref, sf_ref, st_scr, *, with_intra):
    j = pl.program_id(0)

    @pl.when(j == 0)
    def _():
        st_scr[...] = s0_ref[...]

    half = RT_DK // 2
    cos = jnp.concatenate([cos_ref[...]] * 2, axis=1)
    sin = jnp.concatenate([sin_ref[...]] * 2, axis=1)

    def rope(x):
        rot = jnp.concatenate([-x[:, half:], x[:, :half]], axis=1)
        return x * cos + rot * sin

    sk = [slice(h * RT_DK, (h + 1) * RT_DK) for h in range(RT_HEADS)]
    sv = [slice(h * RT_DV, (h + 1) * RT_DV) for h in range(RT_HEADS)]
    items = [(b, h) for b in range(q_ref.shape[0]) for h in range(RT_HEADS)]
    qh = [rope(q_ref[b, :, sk[h]].astype(F32)) for b, h in items]
    kh = [rope(k_ref[b, :, sk[h]].astype(F32) * (RT_DK ** -0.5)) for b, h in items]
    vh = [v_ref[b, :, sv[h]] for b, h in items]
    st = [st_scr[b, h] for b, h in items]
    o = [_bdot(qh[i] * qin_ref[:, sk[h]], st[i]) for i, (b, h) in enumerate(items)]
    if with_intra:
        s = [_bdot_nt(qh[i], kh[i]) * dm_ref[h] for i, (b, h) in enumerate(items)]
        o = [o[i] + _bdot(s[i], vh[i]) for i in range(len(items))]
    for i, (b, h) in enumerate(items):
        o_ref[b, :, sv[h]] = o[i].astype(o_ref.dtype)
        st_scr[b, h] = car_ref[:, sv[h]] * st[i] + _bdot_tn(kh[i] * kout_ref[:, sk[h]], vh[i])

    @pl.when(j == pl.num_programs(0) - 1)
    def _():
        sf_ref[...] = st_scr[...]


def _ret_tables(decay_exp, c):
    lg = jnp.log1p(-jnp.exp2(decay_exp.astype(F32)))
    idx = jnp.arange(c, dtype=F32)
    pos = jnp.stack([idx, c - 1.0 - idx])
    rel = pos[:, :, None] - pos[:, None, :]
    lgh = lg[:, :, None, None]
    dm = jnp.where(rel[:, None] >= 0, jnp.exp(lgh * jnp.maximum(rel[:, None], 0.0)), 0.0)
    qin = jnp.exp(lg[:, None, :] * (pos[:, :, None] + 1.0))
    kout = jnp.exp(lg[:, None, :] * (c - 1.0 - pos[:, :, None]))
    car = jnp.exp(lg * c)
    qin = jnp.repeat(qin, RT_DK, axis=-1)
    kout = jnp.repeat(kout, RT_DK, axis=-1)
    car = jnp.repeat(car, RT_DV, axis=-1)[:, None, :]
    return dm, qin, kout, car


def _ret_scan(p, cos_t, sin_t, decay_exp, s0, *, bsz, seq, name):
    c = min(RT_CHUNK, seq)
    nc = seq // c
    dm, qin, kout, car = _ret_tables(decay_exp, c)
    dm_both = dm[0] + dm[1]
    st_shape = (RT_HEADS, RT_DK, RT_DV)
    p3 = p.reshape(bsz, seq, -1)
    outs, finals = [], []
    for d in range(2):
        crow = (lambda j: nc - 1 - j) if d else (lambda j: j)
        o, sf = pl.pallas_call(
            functools.partial(_ret_scan_kernel, with_intra=(d == 0)),
            out_shape=(jax.ShapeDtypeStruct((bsz, seq, RT_V_DIM), BF16),
                       jax.ShapeDtypeStruct((bsz,) + st_shape, F32)),
            grid=(nc,),
            in_specs=[pl.BlockSpec((bsz, c, D_MODEL), lambda j, crow=crow: (0, crow(j), 0)),
                      pl.BlockSpec((bsz, c, D_MODEL), lambda j, crow=crow: (0, crow(j), 1)),
                      pl.BlockSpec((bsz, c, RT_V_DIM), lambda j, crow=crow: (0, crow(j), 1)),
                      pl.BlockSpec((c, RT_DK // 2), lambda j, crow=crow: (crow(j), 0)),
                      pl.BlockSpec((c, RT_DK // 2), lambda j, crow=crow: (crow(j), 0)),
                      pl.BlockSpec((RT_HEADS, c, c), lambda j: (0, 0, 0)),
                      pl.BlockSpec((None, c, D_MODEL), lambda j, d=d: (d, 0, 0)),
                      pl.BlockSpec((None, c, D_MODEL), lambda j, d=d: (d, 0, 0)),
                      pl.BlockSpec((None, 1, RT_V_DIM), lambda j, d=d: (d, 0, 0)),
                      pl.BlockSpec((bsz, None) + st_shape, lambda j, d=d: (0, d, 0, 0, 0))],
            out_specs=(pl.BlockSpec((bsz, c, RT_V_DIM), lambda j, crow=crow: (0, crow(j), 0)),
                       pl.BlockSpec((bsz,) + st_shape, lambda j: (0, 0, 0, 0))),
            scratch_shapes=[pltpu.VMEM((bsz,) + st_shape, F32)],
            compiler_params=_cp("arbitrary"),
            name=f"{name}_d{d}",
        )(p3, p3, p3, cos_t, sin_t, dm_both, qin, kout, car, s0)
        outs.append(o.reshape(bsz * seq, RT_V_DIM))
        finals.append(sf)
    return outs, jnp.stack(finals, axis=1)


def _ret_out_kernel(o0_ref, o1_ref, gate_ref, w_ref, x_ref, g_ref, out_ref, act_scr):
    for h in range(RT_HEADS):
        sv = slice(h * RT_DV, (h + 1) * RT_DV)
        o = o0_ref[:, sv].astype(F32) + o1_ref[:, sv].astype(F32)
        act_scr[:, sv] = (_silu(gate_ref[:, sv].astype(F32)) * _rms(o)).astype(BF16)
    out_ref[...] = x_ref[...] + g_ref[...] * jnp.dot(act_scr[...], w_ref[...], preferred_element_type=F32)


def _ret_out(o, p, w_out, x2d, gate, *, seq, tm, name):
    m = x2d.shape[0]
    tpb = seq // tm
    return pl.pallas_call(
        _ret_out_kernel,
        out_shape=jax.ShapeDtypeStruct((m, D_MODEL), F32),
        grid=(m // tm,),
        in_specs=[pl.BlockSpec((tm, RT_V_DIM), lambda i: (i, 0)),
                  pl.BlockSpec((tm, RT_V_DIM), lambda i: (i, 0)),
                  pl.BlockSpec((tm, RT_V_DIM), lambda i: (i, 2)),
                  pl.BlockSpec((RT_V_DIM, D_MODEL), lambda i: (0, 0)),
                  pl.BlockSpec((tm, D_MODEL), lambda i: (i, 0)),
                  pl.BlockSpec((None, 1, D_MODEL), lambda i: (i // tpb, 0, 0))],
        out_specs=pl.BlockSpec((tm, D_MODEL), lambda i: (i, 0)),
        scratch_shapes=[pltpu.VMEM((tm, RT_V_DIM), BF16)],
        compiler_params=_cp("parallel"),
        name=name,
    )(o[0], o[1], p, w_out, x2d, gate)


def _seg_sum(x, e_down, e_up):
    s = jnp.dot(x.astype(BF16), e_down, preferred_element_type=F32)
    s_hi = s.astype(BF16)
    s_lo = (s - s_hi.astype(F32)).astype(BF16)
    return jnp.dot(s_hi, e_up, preferred_element_type=F32) + jnp.dot(s_lo, e_up, preferred_element_type=F32)


def _rw_proj_kernel(x_ref, xp_ref, xn_ref, mul_ref, add_ref, mix_ref, wrkv_ref, w0_ref, w1_ref, w2_ref,
                    a0_ref, a1_ref, a2_ref, g1_ref, g2_ref, kk_ref, ka_ref, rk_ref, ed_ref, eu_ref,
                    r_out, v_out, kn_out, gate_out, bonus_out, lw_out, kd_out, bb_out,
                    *, tm, width, tpi, quarters):
    i = pl.program_id(0)
    first = (i % tpi) == 0
    last = (i % tpi) == tpi - 1
    mul = mul_ref[...]
    add = add_ref[...]
    hmod = lambda x: _rms(x) * mul + add
    hm = hmod(x_ref[...])
    col = lax.broadcasted_iota(jnp.int32, (tm, 1), 0) & (width - 1)
    srcs = {}
    if 'l' in quarters:
        srcs['l'] = jnp.where(col != 0, pltpu.roll(hm, 1, 0), 0.0)
    if 'r' in quarters:
        srcs['r'] = jnp.where(col != width - 1, pltpu.roll(hm, tm - 1, 0), 0.0)
    if 'u' in quarters:
        hp = jnp.where(first, 0.0, hmod(xp_ref[...]))
        srcs['u'] = jnp.concatenate([hp, hm[: tm - width]], axis=0) if tm > width else hp
    if 'd' in quarters:
        hn = jnp.where(last, 0.0, hmod(xn_ref[...]))
        srcs['d'] = jnp.concatenate([hm[width:], hn], axis=0) if tm > width else hn
    qd = D_MODEL // 4
    shifted = jnp.concatenate([srcs[q][:, n * qd:(n + 1) * qd] for n, q in enumerate(quarters)], axis=1)
    xx = shifted - hm
    mixed = lambda n: hm + xx * mix_ref[n:n + 1, :]
    r = _bdot(mixed(0), wrkv_ref[0])
    k = _bdot(mixed(2), wrkv_ref[1])
    v = _bdot(mixed(3), wrkv_ref[2])
    xw = mixed(1)
    xa = mixed(4)
    ed = ed_ref[...]
    eu = eu_ref[...]
    kkh = k * kk_ref[...]
    nrm = jnp.sqrt(_seg_sum(kkh * kkh, ed, eu))
    kn = kkh / jnp.maximum(nrm, 1e-12)
    ksum = jnp.zeros_like(k)
    for z in range(2):
        w_raw = w0_ref[z] + _bdot(jnp.tanh(_bdot(xw, w1_ref[z])), w2_ref[z])
        lw_out[z] = -math.exp(-0.5) * _sigmoid(w_raw)
        a = _sigmoid(a0_ref[z] + _bdot(_bdot(xa, a1_ref[z]), a2_ref[z]))
        kd = k * (1.0 + (a - 1.0) * ka_ref[...])
        kd_out[z] = kd.astype(BF16)
        bb_out[z] = (kn * a).astype(BF16)
        ksum = ksum + kd
    r_out[...] = r.astype(BF16)
    v_out[...] = v.astype(BF16)
    kn_out[...] = kn.astype(BF16)
    gate_out[...] = _bdot(_sigmoid(_bdot(mixed(5), g1_ref[...])), g2_ref[...]).astype(BF16)
    bonus_out[...] = (_seg_sum(r * ksum * rk_ref[...], ed, eu) * v).astype(BF16)


def _seg_mats():
    hid = np.arange(D_MODEL) // RW_HEAD
    e_down = (hid[:, None] == np.arange(128)[None, :]).astype(np.float32)
    return jnp.asarray(e_down, BF16), jnp.asarray(e_down.T, BF16)


def _rw_proj(x2d, mul, add, wts, *, seq, width, tm, quarters, name):
    m = x2d.shape[0]
    tpi = seq // tm
    rpt = tm // width
    nrow = m // width
    e_down, e_up = _seg_mats()
    full = lambda a: pl.BlockSpec(a.shape, lambda i: (0,) * a.ndim)
    consts = [wts['mix'], wts['w_rkv'], wts['w0'], wts['w1'], wts['w2'], wts['a0'], wts['a1'], wts['a2'],
              wts['g1'], wts['g2'], wts['k_k'], wts['k_a'], wts['r_k'], e_down, e_up]
    kern = functools.partial(_rw_proj_kernel, tm=tm, width=width, tpi=tpi, quarters=quarters)
    one = jax.ShapeDtypeStruct((m, D_MODEL), BF16)
    two = jax.ShapeDtypeStruct((2, m, D_MODEL), BF16)
    two_f32 = jax.ShapeDtypeStruct((2, m, D_MODEL), F32)
    ospec1 = pl.BlockSpec((tm, D_MODEL), lambda i: (i, 0))
    ospec2 = pl.BlockSpec((2, tm, D_MODEL), lambda i: (0, i, 0))
    return pl.pallas_call(
        kern,
        out_shape=(one, one, one, one, one, two_f32, two, two),
        grid=(m // tm,),
        in_specs=[pl.BlockSpec((tm, D_MODEL), lambda i: (i, 0)),
                  pl.BlockSpec((width, D_MODEL), lambda i: (jnp.maximum(i * rpt - 1, 0), 0)),
                  pl.BlockSpec((width, D_MODEL), lambda i: (jnp.minimum((i + 1) * rpt, nrow - 1), 0)),
                  pl.BlockSpec((None, 1, D_MODEL), lambda i: (i // tpi, 0, 0)),
                  pl.BlockSpec((None, 1, D_MODEL), lambda i: (i // tpi, 0, 0))] + [full(a) for a in consts],
        out_specs=(ospec1, ospec1, ospec1, ospec1, ospec1, ospec2, ospec2, ospec2),
        compiler_params=_cp("parallel"),
        name=name,
    )(x2d, x2d, x2d, mul, add, *consts)


def _rw_tables(c):
    t = np.arange(c)[:, None]
    s = np.arange(c * RW_GROUP_HEADS)[None, :] % c
    ms = np.stack([s < t, s > t]).astype(np.float32)
    mi = np.stack([s <= t, s >= t]).astype(np.float32)
    tri0 = np.tril(np.ones((c, c)))
    tri = np.stack([tri0, tri0.T])
    hrow = np.arange(c * RW_GROUP_HEADS) // c
    lane_h = np.arange(RW_GROUP) // RW_HEAD
    hmask = hrow[:, None] == lane_h[None, :]
    bdc = hrow[:, None] == hrow[None, :]
    bdm = lane_h[:, None] == lane_h[None, :]
    return (jnp.asarray(tri, BF16), jnp.asarray(ms), jnp.asarray(mi), jnp.asarray(hmask.astype(np.float32), BF16),
            jnp.asarray(bdc.astype(np.float32), BF16), jnp.asarray(bdm.astype(np.float32)))


def _rw_scan_kernel(r_ref, kn_ref, v_ref, lw_ref, kd_ref, bb_ref, tri_ref, ms_ref, mi_ref, hm_ref, bdc_ref, bdm_ref,
                    s0_ref, y_ref, sf_ref, st_scr, *, c):
    j = pl.program_id(1)

    @pl.when(j == 0)
    def _():
        st_scr[...] = s0_ref[...]

    hm = hm_ref[...]
    ms = ms_ref[...]
    mi = mi_ref[...]
    bdc = bdc_ref[...]
    bdm = bdm_ref[...]
    tri = tri_ref[...]
    ncat = c * RW_GROUP_HEADS
    eye = (lax.broadcasted_iota(jnp.int32, (c, ncat), 0)
           == (lax.broadcasted_iota(jnp.int32, (c, ncat), 1) & (c - 1))).astype(F32)
    tile = lambda x: jnp.concatenate([x] * RW_GROUP_HEADS, axis=0)
    stacked = lambda x: tile(x) * hm
    blockdiag = lambda x: tile(x) * bdc
    rows = lambda a, b: jnp.concatenate([a, b], axis=0)
    nsq = int(math.log2(c)) - 1
    b16 = lambda x: x.astype(BF16)
    dot = lambda a, b: jnp.dot(a, b, preferred_element_type=F32)
    dot_nt = lambda a, b: lax.dot_general(a, b, (((1,), (1,)), ((), ())), preferred_element_type=F32)
    sls = [slice(gi * RW_GROUP, (gi + 1) * RW_GROUP) for gi in range(RW_NGROUPS)]

    pre = []
    for b in range(r_ref.shape[0]):
        lw = lw_ref[b]
        g = _exact_dot(tri, lw)
        gtot = jnp.sum(lw, axis=0, keepdims=True)
        e_inv = jnp.exp(-g)
        e_end = jnp.exp(gtot - g)
        kd = kd_ref[b].astype(F32)
        bb = bb_ref[b].astype(F32)
        pre.append(dict(kt=b16(kn_ref[b].astype(F32) * jnp.exp(g - lw)), rt=b16(r_ref[b].astype(F32) * jnp.exp(g)),
                        kh=b16(kd * e_inv), bh=b16(bb * e_inv), ke=b16(kd * e_end), be=b16(bb * e_end),
                        v=v_ref[b], dec=jnp.exp(gtot)))

    items = [(b, gi) for b in range(len(pre)) for gi in range(RW_NGROUPS)]
    ids = range(len(items))
    part = lambda name: [pre[b][name][:, sls[gi]] for b, gi in items]
    kr = [rows(a, b) for a, b in zip(part('kt'), part('rt'))]
    yb = [stacked(x) for x in part('bh')]
    yk = [stacked(x) for x in part('kh')]
    vm = [stacked(x) for x in part('v')]
    by_b = [dot_nt(kr[i], yb[i]) for i in ids]
    by_k = [dot_nt(kr[i], yk[i]) for i in ids]
    n = [-(by_b[i][:c] * ms) for i in ids]
    m_b = [b16(by_b[i][c:] * mi) for i in ids]
    lm_k = [rows(b16(by_k[i][:c] * ms), b16(by_k[i][c:] * mi)) for i in ids]
    p = [eye + n[i] for i in ids]
    nb = [b16(n[i]) for i in ids]
    n = [dot(nb[i], blockdiag(nb[i])) for i in ids]
    for lvl in range(nsq):
        nb = [b16(n[i]) for i in ids]
        nbd = [blockdiag(nb[i]) for i in ids]
        if lvl + 1 < nsq:
            both = [dot(rows(nb[i], b16(p[i])), nbd[i]) for i in ids]
            n = [both[i][:c] for i in ids]
            p = [p[i] + both[i][c:] for i in ids]
        else:
            p = [p[i] + dot(b16(p[i]), nbd[i]) for i in ids]
    st = [st_scr[b, gi] for b, gi in items]
    by_s = [dot_nt(kr[i], b16(st[i])) for i in ids]
    by_v = [dot(lm_k[i], vm[i]) for i in ids]
    u = [dot(b16(p[i]), stacked(b16(by_s[i][:c] + by_v[i][:c]))) for i in ids]
    y = [by_s[i][c:] + by_v[i][c:] - dot(m_b[i], stacked(b16(u[i]))) for i in ids]
    for i, (b, gi) in enumerate(items):
        sl = sls[gi]
        y_ref[b, :, sl] = y[i].astype(y_ref.dtype)
        lhs = rows(pre[b]['v'][:, sl], b16(-u[i]))
        rhs_s = rows(pre[b]['ke'][:, sl], pre[b]['be'][:, sl])
        st_scr[b, gi] = st[i] * pre[b]['dec'][:, sl] + bdm * _bdot_tn(lhs, rhs_s)

    @pl.when(j == pl.num_programs(1) - 1)
    def _():
        sf_ref[...] = st_scr[...]


def _rw_scan(r, kn, v, lw, kd, bb, s0, *, bsz, seq, name):
    c = min(RW_CHUNK, seq)
    nc = seq // c
    tri, ms, mi, hmask, bdc, bdm = _rw_tables(c)
    crow = lambda d, j: j + d * (nc - 1 - 2 * j)
    st_shape = (RW_NGROUPS, RW_GROUP, RW_GROUP)
    shared = pl.BlockSpec((bsz, c, D_MODEL), lambda d, j: (0, crow(d, j), 0))
    perdir = pl.BlockSpec((None, bsz, c, D_MODEL), lambda d, j: (d, 0, crow(d, j), 0))
    stspec = pl.BlockSpec((bsz, None) + st_shape, lambda d, j: (0, d, 0, 0, 0))
    nrow = c * RW_GROUP_HEADS
    b3 = lambda a: a.reshape(bsz, seq, D_MODEL)
    b4 = lambda a: a.reshape(2, bsz, seq, D_MODEL)
    y, sf = pl.pallas_call(
        functools.partial(_rw_scan_kernel, c=c),
        out_shape=(jax.ShapeDtypeStruct((2, bsz, seq, D_MODEL), BF16),
                   jax.ShapeDtypeStruct((bsz, 2) + st_shape, F32)),
        grid=(2, nc),
        in_specs=[shared, shared, shared, perdir, perdir, perdir,
                  pl.BlockSpec((None, c, c), lambda d, j: (d, 0, 0)),
                  pl.BlockSpec((None, c, nrow), lambda d, j: (d, 0, 0)),
                  pl.BlockSpec((None, c, nrow), lambda d, j: (d, 0, 0)),
                  pl.BlockSpec((nrow, RW_GROUP), lambda d, j: (0, 0)),
                  pl.BlockSpec((nrow, nrow), lambda d, j: (0, 0)),
                  pl.BlockSpec((RW_GROUP, RW_GROUP), lambda d, j: (0, 0)),
                  stspec],
        out_specs=(perdir, stspec),
        scratch_shapes=[pltpu.VMEM((bsz,) + st_shape, F32)],
        compiler_params=_cp("parallel", "arbitrary"),
        name=name,
    )(b3(r), b3(kn), b3(v), b4(lw), b4(kd), b4(bb), tri, ms, mi, hmask, bdc, bdm, s0)
    return y.reshape(2, bsz * seq, D_MODEL), sf


def _rw_out_kernel(y0_ref, y1_ref, bonus_ref, gate_ref, lnw_ref, lnb_ref, ed_ref, eu_ref, w_ref, x_ref, g_ref,
                   out_ref):
    y = y0_ref[...].astype(F32) + y1_ref[...].astype(F32)
    ed = ed_ref[...]
    eu = eu_ref[...]
    mu = _seg_sum(y, ed, eu) * (1.0 / RW_HEAD)
    dlt = y - mu
    var = _seg_sum(dlt * dlt, ed, eu) * (1.0 / RW_HEAD)
    yn = dlt * lax.rsqrt(var + RW_GN_EPS) * lnw_ref[...] + lnb_ref[...] + bonus_ref[...].astype(F32)
    out_ref[...] = x_ref[...] + g_ref[...] * _bdot(yn * gate_ref[...].astype(F32), w_ref[...])


def _rw_out(y, bonus, gate_act, ln_w, ln_b, w_out, x2d, gate, *, seq, tm, name):
    m = x2d.shape[0]
    tpb = seq // tm
    e_down, e_up = _seg_mats()
    rowspec = pl.BlockSpec((tm, D_MODEL), lambda i: (i, 0))
    full = lambda a: pl.BlockSpec(a.shape, lambda i: (0,) * a.ndim)
    return pl.pallas_call(
        _rw_out_kernel,
        out_shape=jax.ShapeDtypeStruct((m, D_MODEL), F32),
        grid=(m // tm,),
        in_specs=[pl.BlockSpec((None, tm, D_MODEL), lambda i: (0, i, 0)),
                  pl.BlockSpec((None, tm, D_MODEL), lambda i: (1, i, 0)),
                  rowspec, rowspec, full(ln_w), full(ln_b), full(e_down), full(e_up), full(w_out), rowspec,
                  pl.BlockSpec((None, 1, D_MODEL), lambda i: (i // tpb, 0, 0))],
        out_specs=rowspec,
        compiler_params=_cp("parallel"),
        name=name,
    )(y, y, bonus, gate_act, ln_w, ln_b, e_down, e_up, w_out, x2d, gate)


def _rotate_half_cols(w, n_seg):
    sh = w.shape
    ws = w.reshape(sh[:-1] + (n_seg, 2, sh[-1] // (2 * n_seg)))
    return jnp.concatenate([-ws[..., 1:, :], ws[..., :1, :]], axis=-2).reshape(sh)


def _axial_tables(n_tokens):
    t = jnp.arange(n_tokens)
    row = (t // GRID_W).astype(F32)
    col = (t % GRID_W).astype(F32)
    inv = ROPE_BASE ** (-jnp.arange(0, AXIS_DIM, 2, dtype=F32) / AXIS_DIM)
    ang_r = row[:, None] * inv[None, :]
    ang_c = col[:, None] * inv[None, :]
    ang = jnp.concatenate([ang_r, ang_r, ang_c, ang_c], axis=-1)
    return jnp.tile(jnp.cos(ang), (1, 2)), jnp.tile(jnp.sin(ang), (1, 2))


def _retention_tables(n_tokens):
    t = jnp.arange(n_tokens, dtype=F32)
    inv = ROPE_BASE ** (-jnp.linspace(0.0, 1.0, RT_DK // 2, dtype=F32))
    ang = t[:, None] * inv[None, :]
    return jnp.cos(ang), jnp.sin(ang)


def _forward(x, c, ctx, c_ctx, ada_w, ada_b, norm1_g, norm2_g, ffn_w_up, ffn_conv_w, ffn_conv_b, ffn_w_down,
             hg_w_in, hg_lb_logits, hg_norm_g, hg_w_out, at_w_qkv, at_b_qkv, at_sink, at_w_out,
             rt_w_in, rt_decay_exp, rt_w_out, rw_mix, rw_w_rkv, rw_w0, rw_w1, rw_w2, rw_a0, rw_a1, rw_a2,
             rw_g1, rw_g2, rw_k_k, rw_k_a, rw_r_k, rw_ln_w, rw_ln_b, rw_w_out, final_norm_g, depth):
    bsz, seq, d = x.shape
    nctx = ctx.shape[1]
    tm_l = 512 if seq % 512 == 0 else seq
    tm_f = 512 if seq % 512 == 0 else seq
    tm_ffn = 1024 if seq % 1024 == 0 else tm_l
    tm_c = nctx

    cvecs = jnp.zeros((8, d), F32).at[:bsz].set(c).at[bsz].set(c_ctx)
    mod = _adaln(cvecs, ada_w[:depth], ada_b[:depth])
    lb_all = jnp.cumsum(jax.nn.softmax(hg_lb_logits.astype(F32), axis=0), axis=0)
    zero_b = lambda n: jnp.zeros((1, n), F32)

    xl = x.reshape(bsz * seq, d)
    xc = ctx.reshape(bsz * nctx, d)
    for i in range(depth):
        kind, j = i % 4, i // 4
        need_ctx = i < depth - 1
        ml = mod[i, :bsz].reshape(bsz, 6, 1, d)
        mc = jnp.broadcast_to(mod[i, bsz].reshape(1, 6, 1, d), (bsz, 6, 1, d))
        sh1, sc1, g1, sh2, sc2, g2 = (ml[:, n] for n in range(6))
        csh1, csc1, cg1, csh2, csc2, cg2 = (mc[:, n] for n in range(6))
        mul1, cmul1 = norm1_g[i] * (1.0 + sc1), norm1_g[i] * (1.0 + csc1)
        mul2, cmul2 = norm2_g[i] * (1.0 + sc2), norm2_g[i] * (1.0 + csc2)

        if kind == 0:
            w_in = hg_w_in[j].astype(BF16)
            w_out = hg_w_out[j].astype(BF16)
            lb_row = lb_all[i].reshape(1, d)
            ng_row = jnp.tile(hg_norm_g[j], HG_HEADS).reshape(1, d)
            pc, zc = _proj(xc, cmul1, csh1, w_in, zero_b(5 * d), seq=nctx, tm=tm_c, name="hg_proj_ctx",
                           n_f32_tail=2 * d)
            pL, zl = _proj(xl, mul1, sh1, w_in, zero_b(5 * d), seq=seq, tm=tm_ffn, name="hg_proj", n_f32_tail=2 * d)
            s0 = jnp.zeros((bsz, 2, HG_HEADS, HG_DK, HG_DK), F32)
            oc, s_ctx = _hgrn_scan(pc, zc, lb_row, s0, bsz=bsz, seq=nctx, name="hg_scan_ctx")
            ol, _ = _hgrn_scan(pL, zl, lb_row, s_ctx, bsz=bsz, seq=seq, name="hg_scan")
            xl = _hgrn_out(ol, pL, ng_row, w_out, xl, g1, seq=seq, tm=tm_l, name="hg_out")
            if need_ctx:
                xc = _hgrn_out(oc, pc, ng_row, w_out, xc, cg1, seq=nctx, tm=tm_c, name="hg_out_ctx")
        elif kind == 1:
            wq = at_w_qkv[j]
            bq = at_b_qkv[j]
            qd = AT_HEADS * AT_HD
            wq_h = wq[:, :qd].reshape(d, AT_HEADS, AT_HD)
            wk_h = wq[:, qd:qd + AT_KV_DIM].reshape(d, AT_KV_HEADS, AT_HD)
            bq_h = bq[:qd].reshape(AT_HEADS, AT_HD)
            bk_h = bq[qd:qd + AT_KV_DIM].reshape(AT_KV_HEADS, AT_HD)
            w_ext = jnp.concatenate([wq[:, :qd], _rotate_half_cols(wq_h, 2).reshape(d, qd), wq[:, qd:],
                                     _rotate_half_cols(wk_h, 2).reshape(d, AT_KV_DIM)], axis=1).astype(BF16)
            b_ext = jnp.concatenate([bq[:qd], _rotate_half_cols(bq_h, 2).reshape(qd), bq[qd:],
                                     _rotate_half_cols(bk_h, 2).reshape(AT_KV_DIM)]).reshape(1, -1)
            w_out = at_w_out[j].astype(BF16)
            sink = at_sink[j].astype(F32)
            cos2, sin2 = _axial_tables(seq)
            pc = _proj(xc, cmul1, csh1, w_ext, b_ext, seq=nctx, tm=tm_c, name="at_proj_ctx")
            pL = _proj(xl, mul1, sh1, w_ext, b_ext, seq=seq, tm=tm_ffn, name="at_proj")
            ol = _attn(sink, pL, pc, cos2, sin2, bsz=bsz, seq=seq, nctx=nctx)
            xl = _mm_res(ol, w_out, xl, g1, seq=seq, tm=tm_l, name="at_out")
            if need_ctx:
                oc = _attn_ctx(sink, pc, bsz=bsz, nctx=nctx)
                xc = _mm_res(oc, w_out, xc, cg1, seq=nctx, tm=tm_c, name="at_out_ctx")
        elif kind == 2:
            w_in = rt_w_in[j].astype(BF16)
            w_out = rt_w_out[j].astype(BF16)
            cos1, sin1 = _retention_tables(seq)
            ones_t, zeros_t = jnp.ones((nctx, RT_DK // 2), F32), jnp.zeros((nctx, RT_DK // 2), F32)
            pc = _proj(xc, cmul1, csh1, w_in, zero_b(6 * d), seq=nctx, tm=tm_c, name="rt_proj_ctx")
            pL = _proj(xl, mul1, sh1, w_in, zero_b(6 * d), seq=seq, tm=tm_ffn, name="rt_proj")
            s0 = jnp.zeros((bsz, 2, RT_HEADS, RT_DK, RT_DV), F32)
            oc, s_ctx = _ret_scan(pc, ones_t, zeros_t, rt_decay_exp[j], s0, bsz=bsz, seq=nctx, name="rt_scan_ctx")
            ol, _ = _ret_scan(pL, cos1, sin1, rt_decay_exp[j], s_ctx, bsz=bsz, seq=seq, name="rt_scan")
            xl = _ret_out(ol, pL, w_out, xl, g1, seq=seq, tm=tm_l, name="rt_out")
            if need_ctx:
                xc = _ret_out(oc, pc, w_out, xc, cg1, seq=nctx, tm=tm_c, name="rt_out_ctx")
        else:
            wts = dict(mix=rw_mix[j], w_rkv=rw_w_rkv[j].astype(BF16), w0=rw_w0[j].reshape(2, 1, d),
                       w1=rw_w1[j].astype(BF16), w2=rw_w2[j].astype(BF16), a0=rw_a0[j].reshape(2, 1, d),
                       a1=rw_a1[j].astype(BF16), a2=rw_a2[j].astype(BF16), g1=rw_g1[j].astype(BF16),
                       g2=rw_g2[j].astype(BF16), k_k=rw_k_k[j].reshape(1, d), k_a=rw_k_a[j].reshape(1, d),
                       r_k=rw_r_k[j].reshape(1, d))
            w_out = rw_w_out[j].astype(BF16)
            ln_w, ln_b = rw_ln_w[j].reshape(1, d), rw_ln_b[j].reshape(1, d)
            rc = _rw_proj(xc, cmul1, csh1, wts, seq=nctx, width=nctx, tm=tm_c, quarters=('l', 'l', 'r', 'r'),
                          name="rw_proj_ctx")
            rl = _rw_proj(xl, mul1, sh1, wts, seq=seq, width=GRID_W, tm=tm_f, quarters=('l', 'r', 'u', 'd'),
                          name="rw_proj")
            s0 = jnp.zeros((bsz, 2, RW_NGROUPS, RW_GROUP, RW_GROUP), F32)
            scan_in = lambda t: (t[0], t[2], t[1], t[5], t[6], t[7])
            yc, s_ctx = _rw_scan(*scan_in(rc), s0, bsz=bsz, seq=nctx, name="rw_scan_ctx")
            yl, _ = _rw_scan(*scan_in(rl), s_ctx, bsz=bsz, seq=seq, name="rw_scan")
            xl = _rw_out(yl, rl[4], rl[3], ln_w, ln_b, w_out, xl, g1, seq=seq, tm=tm_l, name="rw_out")
            if need_ctx:
                xc = _rw_out(yc, rc[4], rc[3], ln_w, ln_b, w_out, xc, cg1, seq=nctx, tm=tm_c, name="rw_out_ctx")

        w_up = ffn_w_up[i].astype(BF16)
        w_down = ffn_w_down[i].astype(BF16)
        conv_w = ffn_conv_w[i].reshape(9, D_FF)
        conv_b = ffn_conv_b[i].reshape(1, D_FF)
        fin_g = final_norm_g.reshape(1, d)
        xl = _ffn(xl, mul2, sh2, w_up, conv_w, conv_b, w_down, g2, fin_g, seq=seq, width=GRID_W, tm=tm_ffn,
                  final_norm=(i == depth - 1), name="ffn")
        if need_ctx:
            xc = _ffn(xc, cmul2, csh2, w_up, conv_w, conv_b, w_down, cg2, fin_g, seq=nctx, width=nctx, tm=tm_c,
                      final_norm=False, name="ffn_ctx")
    return xl.reshape(bsz, seq, d)


def kernel(x, c, ctx, c_ctx, ada_w, ada_b, norm1_g, norm2_g, ffn_w_up, ffn_conv_w, ffn_conv_b, ffn_w_down, hg_w_in, hg_lb_logits, hg_norm_g, hg_w_out, at_w_qkv, at_b_qkv, at_sink, at_w_out, rt_w_in, rt_decay_exp, rt_w_out, rw_mix, rw_w_rkv, rw_w0, rw_w1, rw_w2, rw_a0, rw_a1, rw_a2, rw_g1, rw_g2, rw_k_k, rw_k_a, rw_r_k, rw_ln_w, rw_ln_b, rw_w_out, final_norm_g):
    return _forward(x, c, ctx, c_ctx, ada_w, ada_b, norm1_g, norm2_g, ffn_w_up, ffn_conv_w, ffn_conv_b, ffn_w_down,
                    hg_w_in, hg_lb_logits, hg_norm_g, hg_w_out, at_w_qkv, at_b_qkv, at_sink, at_w_out,
                    rt_w_in, rt_decay_exp, rt_w_out, rw_mix, rw_w_rkv, rw_w0, rw_w1, rw_w2, rw_a0, rw_a1, rw_a2,
                    rw_g1, rw_g2, rw_k_k, rw_k_a, rw_r_k, rw_ln_w, rw_ln_b, rw_w_out, final_norm_g, DEPTH)
```
